```python
import math
import jax, jax.numpy as jnp
from jax import lax
import numpy as np

D_MODEL = 1024
BATCH = 4
SEQ = 4096
DEPTH = 2

BLOCK = 128
EPS = 1e-6
HEAD_DIM = 64
SB_HEADS = 8
SW_HEADS = 8
SW_KV_HEADS = 2
SW_GROUP = SW_HEADS // SW_KV_HEADS
WINDOW = 128
SB_WIDTH = SB_HEADS * HEAD_DIM
SW_Q_WIDTH = SW_HEADS * HEAD_DIM
SW_KV_WIDTH = SW_KV_HEADS * HEAD_DIM
EVEN_SPLITS = [SB_WIDTH, SB_WIDTH, SB_WIDTH, SW_Q_WIDTH, SW_KV_WIDTH, SW_KV_WIDTH]
EVEN_IN_WIDTH = sum(EVEN_SPLITS)
EVEN_OUT_WIDTH = SB_WIDTH + SW_Q_WIDTH
REL_BUCKETS = 32
REL_MAX_DIST = 128
MLA_HEADS = 16
MLA_NOPE_DIM = 64
MLA_ROPE_DIM = 32
MLA_V_DIM = 64
MLA_Q_RANK = 384
MLA_KV_RANK = 256
MLA_DOWN_WIDTH = MLA_Q_RANK + MLA_KV_RANK + MLA_ROPE_DIM
ROPE_THETA = 10000.0
FFN_HIDDEN = -(-8 * D_MODEL // (3 * 256)) * 256
N_EVEN = (DEPTH + 1) // 2
N_ODD = DEPTH // 2

kernel_name = "hybrid_stickbreak_swa_mla_block"

F32 = jnp.float32


def rmsnorm(x, g):
    xf = x.astype(F32)
    y = xf * lax.rsqrt(jnp.mean(xf * xf, axis=-1, keepdims=True) + EPS)
    return (y * g.astype(F32)).astype(x.dtype)


def t5_bucket(rel):
    max_exact = REL_BUCKETS // 2
    rel = jnp.maximum(rel, 0)
    relf = jnp.maximum(rel, 1).astype(F32)
    large = max_exact + (jnp.log(relf / max_exact) / math.log(REL_MAX_DIST / max_exact)
                         * (REL_BUCKETS - max_exact)).astype(jnp.int32)
    large = jnp.minimum(large, REL_BUCKETS - 1)
    return jnp.where(rel < max_exact, rel, large)


def stick_breaking_attention(q, k, v):
    B_, S, H, d = q.shape
    nb = S // BLOCK
    scale = d ** -0.5
    qb = q.reshape(B_, nb, BLOCK, H, d).transpose(1, 0, 3, 2, 4)
    key_pos = jnp.arange(S)

    def one_block(args):
        q_blk, i = args
        z = jnp.einsum('bhqd,bshd->bhqs', q_blk, k, preferred_element_type=F32) * scale
        q_pos = i * BLOCK + jnp.arange(BLOCK)
        causal = key_pos[None, :] < q_pos[:, None]
        log_beta = jax.nn.log_sigmoid(z)
        log_1m_beta = jnp.where(causal, jax.nn.log_sigmoid(-z), 0.0)
        between = lax.cumsum(log_1m_beta, axis=3, reverse=True) - log_1m_beta
        w = jnp.where(causal, jnp.exp(log_beta + between), 0.0)
        return jnp.einsum('bhqs,bshd->bqhd', w.astype(v.dtype), v)

    out = lax.map(one_block, (qb, jnp.arange(nb)))
    return out.transpose(1, 0, 2, 3, 4).reshape(B_, S, H * d)


def sliding_window_attention(q, k, v, sinks, rel_bias_table):
    B_, S, Hq, d = q.shape
    Hkv = k.shape[2]
    G = Hq // Hkv
    nb = S // BLOCK
    qb = q.reshape(B_, nb, BLOCK, Hkv, G, d)

    def banded(t):
        tb = t.reshape(B_, nb, BLOCK, Hkv, d)
        prev = jnp.concatenate([jnp.zeros_like(tb[:, :1]), tb[:, :-1]], axis=1)
        return jnp.concatenate([prev, tb], axis=2)

    kb, vb = banded(k), banded(v)
    logits = jnp.einsum('bnqkgd,bnskd->bnkgqs', qb, kb, preferred_element_type=F32) * d ** -0.5
    rel = BLOCK + jnp.arange(BLOCK)[:, None] - jnp.arange(2 * BLOCK)[None, :]
    in_window = (rel >= 0) & (rel < WINDOW)
    key_pos = (jnp.arange(nb)[:, None] - 1) * BLOCK + jnp.arange(2 * BLOCK)[None, :]
    valid = in_window[None] & (key_pos >= 0)[:, None, :]
    bias = rel_bias_table.astype(F32)[t5_bucket(rel)]
    bias = bias.transpose(2, 0, 1).reshape(Hkv, G, BLOCK, 2 * BLOCK)
    logits = jnp.where(valid[None, :, None, None], logits + bias, -jnp.inf)
    sink = sinks.astype(F32).reshape(Hkv, G)[:, :, None, None]
    m = jnp.maximum(jnp.max(logits, axis=-1, keepdims=True), sink)
    p = jnp.exp(logits - m)
    p = p / (jnp.sum(p, axis=-1, keepdims=True) + jnp.exp(sink - m))
    out = jnp.einsum('bnkgqs,bnskd->bnqkgd', p.astype(v.dtype), vb)
    return out.reshape(B_, S, Hq * d)


def rope(x, positions):
    half = x.shape[-1] // 2
    freqs = ROPE_THETA ** (-jnp.arange(half, dtype=F32) / half)
    ang = positions.astype(F32)[..., None] * freqs
    cos = jnp.cos(ang)[:, :, None, :]
    sin = jnp.sin(ang)[:, :, None, :]
    x1 = x[..., :half].astype(F32)
    x2 = x[..., half:].astype(F32)
    return jnp.concatenate([x1 * cos - x2 * sin, x2 * cos + x1 * sin], axis=-1).astype(x.dtype)


def mla_attention(q_nope, q_rope, k_nope, k_rope, v):
    B_, S, H, _ = q_nope.shape
    nb = S // BLOCK
    scale = (MLA_NOPE_DIM + MLA_ROPE_DIM) ** -0.5
    qn = q_nope.reshape(B_, nb, BLOCK, H, MLA_NOPE_DIM).transpose(1, 0, 2, 3, 4)
    qr = q_rope.reshape(B_, nb, BLOCK, H, MLA_ROPE_DIM).transpose(1, 0, 2, 3, 4)
    key_pos = jnp.arange(S)

    def one_block(args):
        qn_blk, qr_blk, i = args
        s = (jnp.einsum('bqhd,bshd->bhqs', qn_blk, k_nope, preferred_element_type=F32)
             + jnp.einsum('bqhr,bsr->bhqs', qr_blk, k_rope, preferred_element_type=F32)) * scale
        q_pos = i * BLOCK + jnp.arange(BLOCK)
        s = jnp.where(key_pos[None, :] <= q_pos[:, None], s, -jnp.inf)
        p = jax.nn.softmax(s, axis=-1)
        return jnp.einsum('bhqs,bshd->bqhd', p.astype(v.dtype), v)

    out = lax.map(one_block, (qn, qr, jnp.arange(nb)))
    return out.transpose(1, 0, 2, 3, 4).reshape(B_, S, H * MLA_V_DIM)


def even_mixer(h, w_in, sinks, rel_bias_table, w_out):
    B_, S, _ = h.shape
    proj = h @ w_in
    q_a, k_a, v_a, q_b, k_b, v_b = jnp.split(proj, list(np.cumsum(EVEN_SPLITS)[:-1]), axis=-1)
    hd = lambda t, n: t.reshape(B_, S, n, HEAD_DIM)
    o_a = stick_breaking_attention(hd(q_a, SB_HEADS), hd(k_a, SB_HEADS), hd(v_a, SB_HEADS))
    o_b = sliding_window_attention(hd(q_b, SW_HEADS), hd(k_b, SW_KV_HEADS), hd(v_b, SW_KV_HEADS),
                                   sinks, rel_bias_table)
    return jnp.concatenate([o_a, o_b], axis=-1) @ w_out


def mla_mixer(h, positions, w_down, q_norm, w_uq, kv_norm, w_ukv, w_o):
    B_, S, _ = h.shape
    down = h @ w_down
    c_q = down[..., :MLA_Q_RANK]
    c_kv = down[..., MLA_Q_RANK:MLA_Q_RANK + MLA_KV_RANK]
    k_rope = down[..., MLA_Q_RANK + MLA_KV_RANK:]
    q = (rmsnorm(c_q, q_norm) @ w_uq).reshape(B_, S, MLA_HEADS, MLA_NOPE_DIM + MLA_ROPE_DIM)
    q_nope = q[..., :MLA_NOPE_DIM]
    q_rope = rope(q[..., MLA_NOPE_DIM:], positions)
    kv = (rmsnorm(c_kv, kv_norm) @ w_ukv).reshape(B_, S, MLA_HEADS, MLA_NOPE_DIM + MLA_V_DIM)
    k_nope = kv[..., :MLA_NOPE_DIM]
    v = kv[..., MLA_NOPE_DIM:]
    k_rope = rope(k_rope[:, :, None, :], positions)[:, :, 0, :]
    return mla_attention(q_nope, q_rope, k_nope, k_rope, v) @ w_o


def swiglu(h, w_gate, w_up, w_down):
    return (jax.nn.silu(h @ w_gate) * (h @ w_up)) @ w_down


def setup_inputs(seed: int = 0) -> dict:
    key = jax.random.key(seed)
    ks = jax.random.split(key, 20)
    nrm = lambda k, shape, fan_in: jax.random.normal(k, shape, F32) * fan_in ** -0.5
    gain = lambda k, shape: 1.0 + 0.02 * jax.random.normal(k, shape, F32)
    x = jax.random.normal(ks[0], (BATCH, SEQ, D_MODEL), F32)
    offsets = jax.random.randint(ks[1], (BATCH, 1), 0, 1024, dtype=jnp.int32)
    positions = (jnp.arange(SEQ, dtype=jnp.int32)[None, :] + offsets).astype(jnp.int32)
    return {
        "x": x,
        "positions": positions,
        "attn_norm": gain(ks[2], (DEPTH, D_MODEL)),
        "ffn_norm": gain(ks[3], (DEPTH, D_MODEL)),
        "even_w_in": nrm(ks[4], (N_EVEN, D_MODEL, EVEN_IN_WIDTH), D_MODEL),
        "even_sinks": 0.5 * jax.random.normal(ks[5], (N_EVEN, SW_HEADS), F32),
        "even_w_out": nrm(ks[6], (N_EVEN, EVEN_OUT_WIDTH, D_MODEL), EVEN_OUT_WIDTH),
        "rel_bias_table": 0.5 * jax.random.normal(ks[7], (REL_BUCKETS, SW_HEADS), F32),
        "mla_w_down": nrm(ks[8], (N_ODD, D_MODEL, MLA_DOWN_WIDTH), D_MODEL),
        "mla_q_norm": gain(ks[9], (N_ODD, MLA_Q_RANK)),
        "mla_w_uq": nrm(ks[10], (N_ODD, MLA_Q_RANK, MLA_HEADS * (MLA_NOPE_DIM + MLA_ROPE_DIM)), MLA_Q_RANK),
        "mla_kv_norm": gain(ks[11], (N_ODD, MLA_KV_RANK)),
        "mla_w_ukv": nrm(ks[12], (N_ODD, MLA_KV_RANK, MLA_HEADS * (MLA_NOPE_DIM + MLA_V_DIM)), MLA_KV_RANK),
        "mla_w_o": nrm(ks[13], (N_ODD, MLA_HEADS * MLA_V_DIM, D_MODEL), MLA_HEADS * MLA_V_DIM),
        "ffn_w_gate": nrm(ks[14], (DEPTH, D_MODEL, FFN_HIDDEN), D_MODEL),
        "ffn_w_up": nrm(ks[15], (DEPTH, D_MODEL, FFN_HIDDEN), D_MODEL),
        "ffn_w_down": nrm(ks[16], (DEPTH, FFN_HIDDEN, D_MODEL), FFN_HIDDEN),
        "final_norm": gain(ks[17], (D_MODEL,)),
    }


def reference(x, positions, attn_norm, ffn_norm, even_w_in, even_sinks, even_w_out, rel_bias_table,
              mla_w_down, mla_q_norm, mla_w_uq, mla_kv_norm, mla_w_ukv, mla_w_o,
              ffn_w_gate, ffn_w_up, ffn_w_down, final_norm):
    for layer in range(DEPTH):
        h = rmsnorm(x, attn_norm[layer])
        if layer % 2 == 0:
            e = layer // 2
            x = x + even_mixer(h, even_w_in[e], even_sinks[e], rel_bias_table, even_w_out[e])
        else:
            o = layer // 2
            x = x + mla_mixer(h, positions, mla_w_down[o], mla_q_norm[o], mla_w_uq[o],
                              mla_kv_norm[o], mla_w_ukv[o], mla_w_o[o])
        h = rmsnorm(x, ffn_norm[layer])
        x = x + swiglu(h, ffn_w_gate[layer], ffn_w_up[layer], ffn_w_down[layer])
    return rmsnorm(x, final_norm)
```

```python
import functools
import math

import numpy as np
import jax
import jax.numpy as jnp
from jax import lax
from jax.experimental import pallas as pl
from jax.experimental.pallas import tpu as pltpu

F32 = jnp.float32
BF16 = jnp.bfloat16

D_MODEL = 1024
EPS = 1e-6
HEAD_DIM = 64
LANES = 128
SB_HEADS = 8
SW_HEADS = 8
SW_KV_HEADS = 2
WINDOW = 128
SB_WIDTH = SB_HEADS * HEAD_DIM
SW_Q_WIDTH = SW_HEADS * HEAD_DIM
SW_KV_WIDTH = SW_KV_HEADS * HEAD_DIM
EVEN_IN_WIDTH = 3 * SB_WIDTH + SW_Q_WIDTH + 2 * SW_KV_WIDTH
REL_BUCKETS = 32
REL_MAX_DIST = 128
MLA_HEADS = 16
MLA_NOPE_DIM = 64
MLA_ROPE_DIM = 32
MLA_V_DIM = 64
MLA_Q_RANK = 384
MLA_KV_RANK = 256
ROPE_THETA = 10000.0
LOG2E = math.log2(math.e)
VMEM_LIMIT_BYTES = 56 * 1024 * 1024


def _cparams(n_axes):
    return pltpu.CompilerParams(
        dimension_semantics=("arbitrary",) * n_axes,
        vmem_limit_bytes=VMEM_LIMIT_BYTES)


def _rms(x, g):
    return x * lax.rsqrt(jnp.mean(x * x, axis=-1, keepdims=True) + EPS) * g


def _dot(a, b):
    return jnp.dot(a, b, preferred_element_type=F32)


def _dot_nt(a, b):
    return lax.dot_general(a, b, (((1,), (1,)), ((), ())), preferred_element_type=F32)


def _norm_proj_kernel(x_ref, g_ref, w_ref, o_ref):
    h = _rms(x_ref[...], g_ref[...]).astype(BF16)
    o_ref[...] = _dot(h, w_ref[...]).astype(o_ref.dtype)


def _norm_proj(x2, g, w, *, tm):
    t, d = x2.shape
    n = w.shape[1]
    return pl.pallas_call(
        _norm_proj_kernel,
        out_shape=jax.ShapeDtypeStruct((t, n), BF16),
        grid=(t // tm,),
        in_specs=[pl.BlockSpec((tm, d), lambda i: (i, 0)),
                  pl.BlockSpec((1, d), lambda i: (0, 0)),
                  pl.BlockSpec((d, n), lambda i: (0, 0))],
        out_specs=pl.BlockSpec((tm, n), lambda i: (i, 0)),
        compiler_params=_cparams(1),
        name="norm_proj",
    )(x2, g, w)


def _softplus(z):
    return jnp.maximum(z, 0.0) + jnp.log(1.0 + jnp.exp(-jnp.abs(z)))


def _sb_kernel(q_ref, k_ref, v_ref, o_ref, *, tq):
    qi = pl.program_id(2)
    q2 = q_ref[0]
    lane = lax.broadcasted_iota(jnp.int32, (tq, LANES), 1)
    low = lane < HEAD_DIM
    zero = jnp.zeros_like(q2)
    scale = HEAD_DIM ** -0.5
    qm = (jnp.where(low, q2, zero) * scale, jnp.where(low, zero, q2) * scale)

    row = lax.broadcasted_iota(jnp.int32, (tq, tq), 0)
    col = lax.broadcasted_iota(jnp.int32, (tq, tq), 1)
    tri = (row >= col).astype(BF16)
    causal = col < row

    def head_block(qh, kblk, vblk, r, acc, masked):
        z = _dot_nt(qh, kblk)
        sp = _softplus(z)
        if masked:
            sp = jnp.where(causal, sp, 0.0)
        hi = sp.astype(BF16)
        lo = (sp - hi.astype(F32)).astype(BF16)
        c = _dot(hi, tri) + _dot(lo, tri) + r
        w = jnp.exp(z - c)
        if masked:
            w = jnp.where(causal, w, 0.0)
        acc = acc + _dot(w.astype(BF16), vblk)
        return c[:, 0:1], acc

    def step(j, carry, masked):
        start = pl.multiple_of(j * tq, tq)
        kblk = k_ref[0, pl.ds(start, tq), :]
        vblk = v_ref[0, pl.ds(start, tq), :]
        r0, a0, r1, a1 = carry
        r0, a0 = head_block(qm[0], kblk, vblk, r0, a0, masked)
        r1, a1 = head_block(qm[1], kblk, vblk, r1, a1, masked)
        return r0, a0, r1, a1

    r_init = jnp.zeros((tq, 1), F32)
    a_init = jnp.zeros((tq, LANES), F32)
    carry = step(qi, (r_init, a_init, r_init, a_init), True)
    carry = lax.fori_loop(0, qi, lambda jj, c: step(qi - 1 - jj, c, False), carry)
    _, a0, _, a1 = carry
    o_ref[0] = jnp.where(low, a0, a1).astype(o_ref.dtype)


def _sb_attention(proj3, *, tq):
    b, s, _ = proj3.shape
    n_pairs = SB_WIDTH // LANES
    k_off = SB_WIDTH // LANES
    v_off = 2 * SB_WIDTH // LANES
    return pl.pallas_call(
        functools.partial(_sb_kernel, tq=tq),
        out_shape=jax.ShapeDtypeStruct((b, s, SB_WIDTH), BF16),
        grid=(b, n_pairs, s // tq),
        in_specs=[pl.BlockSpec((1, tq, LANES), lambda bi, p, i: (bi, i, p)),
                  pl.BlockSpec((1, s, LANES), lambda bi, p, i: (bi, 0, k_off + p)),
                  pl.BlockSpec((1, s, LANES), lambda bi, p, i: (bi, 0, v_off + p))],
        out_specs=pl.BlockSpec((1, tq, LANES), lambda bi, p, i: (bi, i, p)),
        compiler_params=_cparams(3),
        name="sb_attention",
    )(proj3, proj3, proj3)


def _t5_bucket_table():
    max_exact = REL_BUCKETS // 2
    rel = WINDOW + np.arange(WINDOW)[:, None] - np.arange(2 * WINDOW)[None, :]
    rel = np.maximum(rel, 0)
    relf = np.maximum(rel, 1).astype(np.float32)
    large = max_exact + (np.log(relf / np.float32(max_exact)) / np.float32(math.log(REL_MAX_DIST / max_exact))
                         * np.float32(REL_BUCKETS - max_exact)).astype(np.int32)
    large = np.minimum(large, REL_BUCKETS - 1)
    return np.where(rel < max_exact, rel, large).astype(np.int32)


def _swap_halves(x):
    return jnp.concatenate([x[:, HEAD_DIM:], x[:, :HEAD_DIM]], axis=1)


def _swa_kernel(table_ref, sinks_ref, bucket_ref, q_ref, kp_ref, kc_ref, vp_ref, vc_ref,
                o_ref, bias_ref):
    bi = pl.program_id(0)
    n = pl.program_id(1)
    blk = WINDOW

    @pl.when(jnp.logical_and(bi == 0, n == 0))
    def _():
        bucket = bucket_ref[...]
        for h in range(SW_HEADS):
            bias = jnp.zeros((blk, 2 * blk), F32)
            for bk in range(REL_BUCKETS):
                bias = jnp.where(bucket == bk, table_ref[bk, h], bias)
            bias_ref[h] = bias

    t = lax.broadcasted_iota(jnp.int32, (blk, 2 * blk), 0)
    s = lax.broadcasted_iota(jnp.int32, (blk, 2 * blk), 1)
    rel = blk + t - s
    valid = (rel >= 0) & (rel < WINDOW) & ((s >= blk) | (n > 0))

    kk = jnp.concatenate([kp_ref[0], kc_ref[0]], axis=0)
    vv = jnp.concatenate([vp_ref[0], vc_ref[0]], axis=0)
    kk_sw = _swap_halves(kk)
    vv_sw = _swap_halves(vv)
    lane = lax.broadcasted_iota(jnp.int32, (blk, LANES), 1)
    low = lane < HEAD_DIM
    scale = HEAD_DIM ** -0.5
    group = SW_HEADS // SW_KV_HEADS

    for p in range(SW_HEADS // 2):
        q2 = q_ref[0, :, p * LANES:(p + 1) * LANES]
        zero = jnp.zeros_like(q2)
        outs = []
        for half in range(2):
            h = 2 * p + half
            kv = h // group
            qh = (jnp.where(low, q2, zero) if half == 0 else jnp.where(low, zero, q2)) * scale
            k_use = kk if kv == half else kk_sw
            v_use = vv if kv == half else vv_sw
            logits = _dot_nt(qh, k_use) + bias_ref[h]
            sink = sinks_ref[0, h]
            m = jnp.maximum(jnp.max(jnp.where(valid, logits, -jnp.inf), axis=-1, keepdims=True), sink)
            pr = jnp.where(valid, jnp.exp(logits - m), 0.0)
            denom = jnp.sum(pr, axis=-1, keepdims=True) + jnp.exp(sink - m)
            outs.append(_dot(pr.astype(BF16), v_use) / denom)
        o_ref[0, :, p * LANES:(p + 1) * LANES] = jnp.where(low, outs[0], outs[1]).astype(o_ref.dtype)


def _swa_attention(proj3, table, sinks):
    b, s, _ = proj3.shape
    nb = s // WINDOW
    q_blk = (3 * SB_WIDTH) // SW_Q_WIDTH
    k_blk = (3 * SB_WIDTH + SW_Q_WIDTH) // LANES
    v_blk = k_blk + 1
    bucket = jnp.asarray(_t5_bucket_table())
    smem = pl.BlockSpec(memory_space=pltpu.SMEM)
    return pl.pallas_call(
        _swa_kernel,
        out_shape=jax.ShapeDtypeStruct((b, s, SW_Q_WIDTH), BF16),
        grid=(b, nb),
        in_specs=[smem, smem,
                  pl.BlockSpec((WINDOW, 2 * WINDOW), lambda bi, n: (0, 0)),
                  pl.BlockSpec((1, WINDOW, SW_Q_WIDTH), lambda bi, n: (bi, n, q_blk)),
                  pl.BlockSpec((1, WINDOW, LANES), lambda bi, n: (bi, jnp.maximum(n - 1, 0), k_blk)),
                  pl.BlockSpec((1, WINDOW, LANES), lambda bi, n: (bi, n, k_blk)),
                  pl.BlockSpec((1, WINDOW, LANES), lambda bi, n: (bi, jnp.maximum(n - 1, 0), v_blk)),
                  pl.BlockSpec((1, WINDOW, LANES), lambda bi, n: (bi, n, v_blk))],
        out_specs=pl.BlockSpec((1, WINDOW, SW_Q_WIDTH), lambda bi, n: (bi, n, 0)),
        scratch_shapes=[pltpu.VMEM((SW_HEADS, WINDOW, 2 * WINDOW), F32)],
        compiler_params=_cparams(2),
        name="swa_attention",
    )(table, sinks, bucket, proj3, proj3, proj3, proj3, proj3)


def _out_proj_kernel(*refs, n_parts):
    x_ref = refs[0]
    part_refs = refs[1:1 + n_parts]
    w_refs = refs[1 + n_parts:1 + 2 * n_parts]
    o_ref = refs[1 + 2 * n_parts]
    acc = x_ref[...]
    for a_ref, w_ref in zip(part_refs, w_refs):
        acc = acc + _dot(a_ref[...], w_ref[...])
    o_ref[...] = acc


def _out_proj(x2, parts, w, *, tm):
    t, d = x2.shape
    n_parts = len(parts)
    kp = w.shape[0] // n_parts
    in_specs = [pl.BlockSpec((tm, d), lambda i: (i, 0))]
    in_specs += [pl.BlockSpec((tm, kp), lambda i: (i, 0)) for _ in parts]
    in_specs += [pl.BlockSpec((kp, d), functools.partial(lambda i, c: (c, 0), c=c)) for c in range(n_parts)]
    return pl.pallas_call(
        functools.partial(_out_proj_kernel, n_parts=n_parts),
        out_shape=jax.ShapeDtypeStruct((t, d), F32),
        grid=(t // tm,),
        in_specs=in_specs,
        out_specs=pl.BlockSpec((tm, d), lambda i: (i, 0)),
        compiler_params=_cparams(1),
        name="out_proj",
    )(x2, *parts, *([w] * n_parts))


def _ffn_kernel(x_ref, g_ref, wg_ref, wu_ref, wd_ref, fg_ref, o_ref, *, final_norm):
    x = x_ref[...]
    h = _rms(x, g_ref[...]).astype(BF16)
    gate = _dot(h, wg_ref[...])
    up = _dot(h, wu_ref[...])
    act = (gate / (1.0 + jnp.exp(-gate)) * up).astype(BF16)
    y = x + _dot(act, wd_ref[...])
    if final_norm:
        y = _rms(y, fg_ref[...])
    o_ref[...] = y


def _ffn(x2, g, wg, wu, wd, fg, *, tm, final_norm):
    t, d = x2.shape
    hid = wg.shape[1]
    resident = dict(pipeline_mode=pl.Buffered(1))
    return pl.pallas_call(
        functools.partial(_ffn_kernel, final_norm=final_norm),
        out_shape=jax.ShapeDtypeStruct((t, d), F32),
        grid=(t // tm,),
        in_specs=[pl.BlockSpec((tm, d), lambda i: (i, 0)),
                  pl.BlockSpec((1, d), lambda i: (0, 0)),
                  pl.BlockSpec((d, hid), lambda i: (0, 0), **resident),
                  pl.BlockSpec((d, hid), lambda i: (0, 0), **resident),
                  pl.BlockSpec((hid, d), lambda i: (0, 0), **resident),
                  pl.BlockSpec((1, d), lambda i: (0, 0))],
        out_specs=pl.BlockSpec((tm, d), lambda i: (i, 0)),
        compiler_params=_cparams(1),
        name="ffn_final" if final_norm else "ffn",
    )(x2, g, wg, wu, wd, fg)


def _mla_proj_kernel(x_ref, pos_ref, freq_ref, g_ref, wd_ref, qg_ref, wqa_ref, wqb_ref,
                     kg_ref, wk_ref, wv_ref, q_ref, k_ref, v_ref):
    tm = x_ref.shape[0]
    h = _rms(x_ref[...], g_ref[...]).astype(BF16)
    down = _dot(h, wd_ref[...])
    c_q = down[:, :MLA_Q_RANK]
    c_kv = down[:, MLA_Q_RANK:MLA_Q_RANK + MLA_KV_RANK]
    kr = down[:, MLA_Q_RANK + MLA_KV_RANK:MLA_Q_RANK + MLA_KV_RANK + LANES]
    kr_rot = down[:, MLA_Q_RANK + MLA_KV_RANK + LANES:]

    lane = lax.broadcasted_iota(jnp.int32, (tm, LANES), 1)
    is_rope = (lane >= MLA_NOPE_DIM) & (lane < MLA_NOPE_DIM + MLA_ROPE_DIM)
    ang = pos_ref[...].astype(F32) * freq_ref[...]
    cos_p = jnp.where(is_rope, jnp.cos(ang), 1.0)
    sin_p = jnp.where(is_rope, jnp.sin(ang), 0.0)

    cqn = _rms(c_q, qg_ref[...]).astype(BF16)
    ckvn = _rms(c_kv, kg_ref[...]).astype(BF16)
    k_rope = kr * cos_p + kr_rot * sin_p
    qa = _dot(cqn, wqa_ref[...])
    qb = _dot(cqn, wqb_ref[...])
    kn = _dot(ckvn, wk_ref[...])
    for hd in range(MLA_HEADS):
        sl = slice(hd * LANES, (hd + 1) * LANES)
        q_ref[:, sl] = (qa[:, sl] * cos_p + qb[:, sl] * sin_p).astype(q_ref.dtype)
        k_ref[:, sl] = (kn[:, sl] + k_rope).astype(k_ref.dtype)
    v_ref[...] = _dot(ckvn, wv_ref[...]).astype(v_ref.dtype)


def _mla_proj(x2, pos2, freq, g, wd, qg, wqa, wqb, kg, wk, wv, *, tm):
    t, d = x2.shape
    full = lambda a: pl.BlockSpec(a.shape, lambda i: (0,) * a.ndim)
    qk_w = MLA_HEADS * LANES
    v_w = MLA_HEADS * MLA_V_DIM
    return pl.pallas_call(
        _mla_proj_kernel,
        out_shape=(jax.ShapeDtypeStruct((t, qk_w), BF16),
                   jax.ShapeDtypeStruct((t, qk_w), BF16),
                   jax.ShapeDtypeStruct((t, v_w), BF16)),
        grid=(t // tm,),
        in_specs=[pl.BlockSpec((tm, d), lambda i: (i, 0)),
                  pl.BlockSpec((tm, 1), lambda i: (i, 0)),
                  full(freq), full(g), full(wd), full(qg), full(wqa), full(wqb),
                  full(kg), full(wk), full(wv)],
        out_specs=(pl.BlockSpec((tm, qk_w), lambda i: (i, 0)),
                   pl.BlockSpec((tm, qk_w), lambda i: (i, 0)),
                   pl.BlockSpec((tm, v_w), lambda i: (i, 0))),
        compiler_params=_cparams(1),
        name="mla_proj",
    )(x2, pos2, freq, g, wd, qg, wqa, wqb, kg, wk, wv)


def _mla_attn_kernel(q_ref, k_ref, v_ref, o_ref, *, tq):
    qi = pl.program_id(2)
    row = lax.broadcasted_iota(jnp.int32, (tq, tq), 0)
    col = lax.broadcasted_iota(jnp.int32, (tq, tq), 1)
    causal = col <= row
    lane = lax.broadcasted_iota(jnp.int32, (tq, LANES), 1)
    low = lane < MLA_V_DIM
    c2 = (MLA_NOPE_DIM + MLA_ROPE_DIM) ** -0.5 * LOG2E
    qh = (q_ref[0, :, :LANES], q_ref[0, :, LANES:])

    def head_block(q, kblk, vblk, m, l, acc, masked):
        s2 = _dot_nt(q, kblk) * c2
        if masked:
            s2 = jnp.where(causal, s2, -jnp.inf)
        m_new = jnp.maximum(m, jnp.max(s2, axis=-1, keepdims=True))
        alpha = jnp.exp2(m - m_new)
        p = jnp.exp2(s2 - m_new)
        l = alpha * l + jnp.sum(p, axis=-1, keepdims=True)
        acc = alpha * acc + _dot(p.astype(BF16), vblk)
        return m_new, l, acc

    def step(j, carry, masked):
        start = pl.multiple_of(j * tq, tq)
        vblk = v_ref[0, pl.ds(start, tq), :]
        out = []
        for hh in range(2):
            kblk = k_ref[0, pl.ds(start, tq), hh * LANES:(hh + 1) * LANES]
            out.append(head_block(qh[hh], kblk, vblk, *carry[hh], masked))
        return tuple(out)

    m0 = jnp.full((tq, 1), -jnp.inf, F32)
    l0 = jnp.zeros((tq, 1), F32)
    a0 = jnp.zeros((tq, LANES), F32)
    carry = step(qi, ((m0, l0, a0), (m0, l0, a0)), True)
    carry = lax.fori_loop(0, qi, lambda j, c: step(j, c, False), carry)
    (_, l_a, acc_a), (_, l_b, acc_b) = carry
    o_ref[0] = jnp.where(low, acc_a / l_a, acc_b / l_b).astype(o_ref.dtype)


def _mla_attention(q3, k3, v3, *, tq):
    b, s, _ = q3.shape
    n_pairs = MLA_HEADS // 2
    return pl.pallas_call(
        functools.partial(_mla_attn_kernel, tq=tq),
        out_shape=jax.ShapeDtypeStruct((b, s, MLA_HEADS * MLA_V_DIM), BF16),
        grid=(b, n_pairs, s // tq),
        in_specs=[pl.BlockSpec((1, tq, 2 * LANES), lambda bi, p, i: (bi, i, p)),
                  pl.BlockSpec((1, s, 2 * LANES), lambda bi, p, i: (bi, 0, p)),
                  pl.BlockSpec((1, s, LANES), lambda bi, p, i: (bi, 0, p))],
        out_specs=pl.BlockSpec((1, tq, LANES), lambda bi, p, i: (bi, i, p)),
        compiler_params=_cparams(3),
        name="mla_attention",
    )(q3, k3, v3)


def _rotate_half_cols(w):
    half = MLA_ROPE_DIM // 2
    return jnp.concatenate([-w[..., half:], w[..., :half]], axis=-1)


def _mla_weight_layout(w_down, w_uq, w_ukv):
    d = w_down.shape[0]
    w_kr = w_down[:, MLA_Q_RANK + MLA_KV_RANK:]
    pad_lo = jnp.zeros((d, MLA_NOPE_DIM), F32)
    pad_hi = jnp.zeros((d, LANES - MLA_NOPE_DIM - MLA_ROPE_DIM), F32)
    wd = jnp.concatenate([w_down[:, :MLA_Q_RANK + MLA_KV_RANK],
                          pad_lo, w_kr, pad_hi,
                          pad_lo, _rotate_half_cols(w_kr), pad_hi], axis=1)

    wq = w_uq.reshape(MLA_Q_RANK, MLA_HEADS, MLA_NOPE_DIM + MLA_ROPE_DIM)
    wq_nope, wq_rope = wq[..., :MLA_NOPE_DIM], wq[..., MLA_NOPE_DIM:]
    zq = jnp.zeros((MLA_Q_RANK, MLA_HEADS, LANES - MLA_NOPE_DIM - MLA_ROPE_DIM), F32)
    wqa = jnp.concatenate([wq_nope, wq_rope, zq], axis=-1).reshape(MLA_Q_RANK, MLA_HEADS * LANES)
    wqb = jnp.concatenate([jnp.zeros_like(wq_nope), _rotate_half_cols(wq_rope), zq],
                          axis=-1).reshape(MLA_Q_RANK, MLA_HEADS * LANES)

    wkv = w_ukv.reshape(MLA_KV_RANK, MLA_HEADS, MLA_NOPE_DIM + MLA_V_DIM)
    wk_nope, wv = wkv[..., :MLA_NOPE_DIM], wkv[..., MLA_NOPE_DIM:]
    wk = jnp.concatenate([wk_nope, jnp.zeros((MLA_KV_RANK, MLA_HEADS, LANES - MLA_NOPE_DIM), F32)],
                         axis=-1).reshape(MLA_KV_RANK, MLA_HEADS * LANES)
    wv = wv.reshape(MLA_KV_RANK, MLA_HEADS * MLA_V_DIM)
    return wd.astype(BF16), wqa.astype(BF16), wqb.astype(BF16), wk.astype(BF16), wv.astype(BF16)


def _rope_lane_freqs():
    half = MLA_ROPE_DIM // 2
    freqs = ROPE_THETA ** (-jnp.arange(half, dtype=F32) / half)
    lane = np.arange(LANES)
    idx = np.clip(lane - MLA_NOPE_DIM, 0, MLA_ROPE_DIM - 1) % half
    return freqs[idx][None, :]


def kernel(x, positions, attn_norm, ffn_norm, even_w_in, even_sinks, even_w_out, rel_bias_table, mla_w_down, mla_q_norm, mla_w_uq, mla_kv_norm, mla_w_ukv, mla_w_o, ffn_w_gate, ffn_w_up, ffn_w_down, final_norm):
    b, s, d = x.shape
    t = b * s
    x2 = x.reshape(t, d)
    row = lambda v: v.reshape(1, -1)

    proj = _norm_proj(x2, row(attn_norm[0]), even_w_in[0].astype(BF16), tm=1024)
    proj3 = proj.reshape(b, s, EVEN_IN_WIDTH)
    o_a = _sb_attention(proj3, tq=256)
    o_b = _swa_attention(proj3, rel_bias_table, even_sinks[0].reshape(1, SW_HEADS))
    x2 = _out_proj(x2, [o_a.reshape(t, SB_WIDTH), o_b.reshape(t, SW_Q_WIDTH)],
                   even_w_out[0].astype(BF16), tm=1024)
    x2 = _ffn(x2, row(ffn_norm[0]), ffn_w_gate[0].astype(BF16), ffn_w_up[0].astype(BF16),
              ffn_w_down[0].astype(BF16), row(final_norm), tm=512, final_norm=False)

    wd, wqa, wqb, wk, wv = _mla_weight_layout(mla_w_down[0], mla_w_uq[0], mla_w_ukv[0])
    q, k, v = _mla_proj(x2, positions.reshape(t, 1), _rope_lane_freqs(), row(attn_norm[1]), wd,
                        row(mla_q_norm[0]), wqa, wqb, row(mla_kv_norm[0]), wk, wv, tm=512)
    o = _mla_attention(q.reshape(b, s, -1), k.reshape(b, s, -1), v.reshape(b, s, -1), tq=256)
    x2 = _out_proj(x2, [o.reshape(t, MLA_HEADS * MLA_V_DIM)], mla_w_o[0].astype(BF16), tm=1024)
    x2 = _ffn(x2, row(ffn_norm[1]), ffn_w_gate[1].astype(BF16), ffn_w_up[1].astype(BF16),
              ffn_w_down[1].astype(BF16), row(final_norm), tm=512, final_norm=True)
    return x2.reshape(b, s, d)
```

```python
import functools
import math

import numpy as np
import jax
import jax.numpy as jnp
from jax import lax
from jax.experimental import pallas as pl
from jax.experimental.pallas import tpu as pltpu

F32 = jnp.float32
BF16 = jnp.bfloat16

D_MODEL = 1024
EPS = 1e-6
HEAD_DIM = 64
LANES = 128
SB_HEADS = 8
SW_HEADS = 8
SW_KV_HEADS = 2
WINDOW = 128
SB_WIDTH = SB_HEADS * HEAD_DIM
SW_Q_WIDTH = SW_HEADS * HEAD_DIM
SW_KV_WIDTH = SW_KV_HEADS * HEAD_DIM
EVEN_IN_WIDTH = 3 * SB_WIDTH + SW_Q_WIDTH + 2 * SW_KV_WIDTH
REL_BUCKETS = 32
REL_MAX_DIST = 128
MLA_HEADS = 16
MLA_NOPE_DIM = 64
MLA_ROPE_DIM = 32
MLA_V_DIM = 64
MLA_Q_RANK = 384
MLA_KV_RANK = 256
ROPE_THETA = 10000.0
LOG2E = math.log2(math.e)
VMEM_LIMIT_BYTES = 56 * 1024 * 1024


def _cparams(n_axes):
    return pltpu.CompilerParams(
        dimension_semantics=("arbitrary",) * n_axes,
        vmem_limit_bytes=VMEM_LIMIT_BYTES)


def _rms(x, g):
    return x * lax.rsqrt(jnp.mean(x * x, axis=-1, keepdims=True) + EPS) * g


def _dot(a, b):
    return jnp.dot(a, b, preferred_element_type=F32)


def _dot_nt(a, b):
    return lax.dot_general(a, b, (((1,), (1,)), ((), ())), preferred_element_type=F32)


def _norm_proj_kernel(x_ref, g_ref, w_ref, o_ref):
    h = _rms(x_ref[...], g_ref[...]).astype(BF16)
    o_ref[...] = _dot(h, w_ref[...]).astype(o_ref.dtype)


def _norm_proj(x2, g, w, *, tm):
    t, d = x2.shape
    n = w.shape[1]
    return pl.pallas_call(
        _norm_proj_kernel,
        out_shape=jax.ShapeDtypeStruct((t, n), BF16),
        grid=(t // tm,),
        in_specs=[pl.BlockSpec((tm, d), lambda i: (i, 0)),
                  pl.BlockSpec((1, d), lambda i: (0, 0)),
                  pl.BlockSpec((d, n), lambda i: (0, 0))],
        out_specs=pl.BlockSpec((tm, n), lambda i: (i, 0)),
        compiler_params=_cparams(1),
        name="norm_proj",
    )(x2, g, w)


def _softplus(z):
    return jnp.maximum(z, 0.0) + jnp.log(1.0 + jnp.exp(-jnp.abs(z)))


def _sb_kernel(q_ref, k_ref, v_ref, o_ref, *, tq, sub):
    qi = pl.program_id(2)
    n_sub = tq // sub
    q2 = q_ref[0]
    lane = lax.broadcasted_iota(jnp.int32, (tq, LANES), 1)
    low = lane < HEAD_DIM
    zero = jnp.zeros_like(q2)
    scale = HEAD_DIM ** -0.5
    qm = (jnp.where(low, q2, zero) * scale, jnp.where(low, zero, q2) * scale)

    tri_row = lax.broadcasted_iota(jnp.int32, (sub, sub), 0)
    tri_col = lax.broadcasted_iota(jnp.int32, (sub, sub), 1)
    tri = (tri_row >= tri_col).astype(BF16)
    row = lax.broadcasted_iota(jnp.int32, (tq, sub), 0)
    col = lax.broadcasted_iota(jnp.int32, (tq, sub), 1)

    def head_sub_block(qh, kblk, vblk, r, acc, causal):
        z = _dot_nt(qh, kblk)
        sp = _softplus(z)
        if causal is not None:
            sp = jnp.where(causal, sp, 0.0)
        hi = sp.astype(BF16)
        lo = (sp - hi.astype(F32)).astype(BF16)
        c = _dot(hi, tri) + _dot(lo, tri) + r
        w = jnp.exp(z - c)
        if causal is not None:
            w = jnp.where(causal, w, 0.0)
        acc = acc + _dot(w.astype(BF16), vblk)
        return c[:, 0:1], acc

    def step(j, carry, masked):
        r0, a0, r1, a1 = carry
        for u in reversed(range(n_sub)):
            start = pl.multiple_of(j * tq + u * sub, sub)
            kblk = k_ref[0, pl.ds(start, sub), :]
            vblk = v_ref[0, pl.ds(start, sub), :]
            causal = (col + u * sub < row) if masked else None
            r0, a0 = head_sub_block(qm[0], kblk, vblk, r0, a0, causal)
            r1, a1 = head_sub_block(qm[1], kblk, vblk, r1, a1, causal)
        return r0, a0, r1, a1

    r_init = jnp.zeros((tq, 1), F32)
    a_init = jnp.zeros((tq, LANES), F32)
    carry = step(qi, (r_init, a_init, r_init, a_init), True)
    carry = lax.fori_loop(0, qi, lambda jj, c: step(qi - 1 - jj, c, False), carry)
    _, a0, _, a1 = carry
    o_ref[0] = jnp.where(low, a0, a1).astype(o_ref.dtype)


def _sb_attention(proj3, *, tq, sub):
    b, s, _ = proj3.shape
    n_pairs = SB_WIDTH // LANES
    k_off = SB_WIDTH // LANES
    v_off = 2 * SB_WIDTH // LANES
    return pl.pallas_call(
        functools.partial(_sb_kernel, tq=tq, sub=sub),
        out_shape=jax.ShapeDtypeStruct((b, s, SB_WIDTH), BF16),
        grid=(b, n_pairs, s // tq),
        in_specs=[pl.BlockSpec((1, tq, LANES), lambda bi, p, i: (bi, i, p)),
                  pl.BlockSpec((1, s, LANES), lambda bi, p, i: (bi, 0, k_off + p)),
                  pl.BlockSpec((1, s, LANES), lambda bi, p, i: (bi, 0, v_off + p))],
        out_specs=pl.BlockSpec((1, tq, LANES), lambda bi, p, i: (bi, i, p)),
        compiler_params=_cparams(3),
        name="sb_attention",
    )(proj3, proj3, proj3)


def _t5_bucket_table():
    max_exact = REL_BUCKETS // 2
    rel = WINDOW + np.arange(WINDOW)[:, None] - np.arange(2 * WINDOW)[None, :]
    rel = np.maximum(rel, 0)
    relf = np.maximum(rel, 1).astype(np.float32)
    large = max_exact + (np.log(relf / np.float32(max_exact)) / np.float32(math.log(REL_MAX_DIST / max_exact))
                         * np.float32(REL_BUCKETS - max_exact)).astype(np.int32)
    large = np.minimum(large, REL_BUCKETS - 1)
    return np.where(rel < max_exact, rel, large).astype(np.int32)


def _swap_halves(x):
    return jnp.concatenate([x[:, HEAD_DIM:], x[:, :HEAD_DIM]], axis=1)


def _swa_kernel(table_ref, sinks_ref, bucket_ref, q_ref, kp_ref, kc_ref, vp_ref, vc_ref,
                o_ref, bias_ref):
    bi = pl.program_id(0)
    n = pl.program_id(1)
    blk = WINDOW

    @pl.when(jnp.logical_and(bi == 0, n == 0))
    def _():
        bucket = bucket_ref[...]
        for h in range(SW_HEADS):
            bias = jnp.zeros((blk, 2 * blk), F32)
            for bk in range(REL_BUCKETS):
                bias = jnp.where(bucket == bk, table_ref[bk, h], bias)
            bias_ref[h] = bias

    t = lax.broadcasted_iota(jnp.int32, (blk, 2 * blk), 0)
    s = lax.broadcasted_iota(jnp.int32, (blk, 2 * blk), 1)
    rel = blk + t - s
    valid = (rel >= 0) & (rel < WINDOW) & ((s >= blk) | (n > 0))

    kk = jnp.concatenate([kp_ref[0], kc_ref[0]], axis=0)
    vv = jnp.concatenate([vp_ref[0], vc_ref[0]], axis=0)
    kk_sw = _swap_halves(kk)
    vv_sw = _swap_halves(vv)
    lane = lax.broadcasted_iota(jnp.int32, (blk, LANES), 1)
    low = lane < HEAD_DIM
    scale = HEAD_DIM ** -0.5
    group = SW_HEADS // SW_KV_HEADS

    for p in range(SW_HEADS // 2):
        q2 = q_ref[0, :, p * LANES:(p + 1) * LANES]
        zero = jnp.zeros_like(q2)
        outs = []
        for half in range(2):
            h = 2 * p + half
            kv = h // group
            qh = (jnp.where(low, q2, zero) if half == 0 else jnp.where(low, zero, q2)) * scale
            k_use = kk if kv == half else kk_sw
            v_use = vv if kv == half else vv_sw
            logits = _dot_nt(qh, k_use) + bias_ref[h]
            sink = sinks_ref[0, h]
            m = jnp.maximum(jnp.max(jnp.where(valid, logits, -jnp.inf), axis=-1, keepdims=True), sink)
            pr = jnp.where(valid, jnp.exp(logits - m), 0.0)
            denom = jnp.sum(pr, axis=-1, keepdims=True) + jnp.exp(sink - m)
            outs.append(_dot(pr.astype(BF16), v_use) / denom)
        o_ref[0, :, p * LANES:(p + 1) * LANES] = jnp.where(low, outs[0], outs[1]).astype(o_ref.dtype)


def _swa_attention(proj3, table, sinks):
    b, s, _ = proj3.shape
    nb = s // WINDOW
    q_blk = (3 * SB_WIDTH) // SW_Q_WIDTH
    k_blk = (3 * SB_WIDTH + SW_Q_WIDTH) // LANES
    v_blk = k_blk + 1
    bucket = jnp.asarray(_t5_bucket_table())
    smem = pl.BlockSpec(memory_space=pltpu.SMEM)
    return pl.pallas_call(
        _swa_kernel,
        out_shape=jax.ShapeDtypeStruct((b, s, SW_Q_WIDTH), BF16),
        grid=(b, nb),
        in_specs=[smem, smem,
                  pl.BlockSpec((WINDOW, 2 * WINDOW), lambda bi, n: (0, 0)),
                  pl.BlockSpec((1, WINDOW, SW_Q_WIDTH), lambda bi, n: (bi, n, q_blk)),
                  pl.BlockSpec((1, WINDOW, LANES), lambda bi, n: (bi, jnp.maximum(n - 1, 0), k_blk)),
                  pl.BlockSpec((1, WINDOW, LANES), lambda bi, n: (bi, n, k_blk)),
                  pl.BlockSpec((1, WINDOW, LANES), lambda bi, n: (bi, jnp.maximum(n - 1, 0), v_blk)),
                  pl.BlockSpec((1, WINDOW, LANES), lambda bi, n: (bi, n, v_blk))],
        out_specs=pl.BlockSpec((1, WINDOW, SW_Q_WIDTH), lambda bi, n: (bi, n, 0)),
        scratch_shapes=[pltpu.VMEM((SW_HEADS, WINDOW, 2 * WINDOW), F32)],
        compiler_params=_cparams(2),
        name="swa_attention",
    )(table, sinks, bucket, proj3, proj3, proj3, proj3, proj3)


def _out_proj_kernel(*refs, n_parts):
    x_ref = refs[0]
    part_refs = refs[1:1 + n_parts]
    w_refs = refs[1 + n_parts:1 + 2 * n_parts]
    o_ref = refs[1 + 2 * n_parts]
    acc = x_ref[...]
    for a_ref, w_ref in zip(part_refs, w_refs):
        acc = acc + _dot(a_ref[...], w_ref[...])
    o_ref[...] = acc


def _out_proj(x2, parts, w, *, tm):
    t, d = x2.shape
    n_parts = len(parts)
    kp = w.shape[0] // n_parts
    in_specs = [pl.BlockSpec((tm, d), lambda i: (i, 0))]
    in_specs += [pl.BlockSpec((tm, kp), lambda i: (i, 0)) for _ in parts]
    in_specs += [pl.BlockSpec((kp, d), functools.partial(lambda i, c: (c, 0), c=c)) for c in range(n_parts)]
    return pl.pallas_call(
        functools.partial(_out_proj_kernel, n_parts=n_parts),
        out_shape=jax.ShapeDtypeStruct((t, d), F32),
        grid=(t // tm,),
        in_specs=in_specs,
        out_specs=pl.BlockSpec((tm, d), lambda i: (i, 0)),
        compiler_params=_cparams(1),
        name="out_proj",
    )(x2, *parts, *([w] * n_parts))


def _ffn_kernel(x_ref, g_ref, wg_ref, wu_ref, wd_ref, fg_ref, o_ref, *, final_norm):
    x = x_ref[...]
    h = _rms(x, g_ref[...]).astype(BF16)
    gate = _dot(h, wg_ref[...])
    up = _dot(h, wu_ref[...])
    act = (gate / (1.0 + jnp.exp(-gate)) * up).astype(BF16)
    y = x + _dot(act, wd_ref[...])
    if final_norm:
        y = _rms(y, fg_ref[...])
    o_ref[...] = y


def _ffn(x2, g, wg, wu, wd, fg, *, tm, final_norm):
    t, d = x2.shape
    hid = wg.shape[1]
    resident = dict(pipeline_mode=pl.Buffered(1))
    return pl.pallas_call(
        functools.partial(_ffn_kernel, final_norm=final_norm),
        out_shape=jax.ShapeDtypeStruct((t, d), F32),
        grid=(t // tm,),
        in_specs=[pl.BlockSpec((tm, d), lambda i: (i, 0)),
                  pl.BlockSpec((1, d), lambda i: (0, 0)),
                  pl.BlockSpec((d, hid), lambda i: (0, 0), **resident),
                  pl.BlockSpec((d, hid), lambda i: (0, 0), **resident),
                  pl.BlockSpec((hid, d), lambda i: (0, 0), **resident),
                  pl.BlockSpec((1, d), lambda i: (0, 0))],
        out_specs=pl.BlockSpec((tm, d), lambda i: (i, 0)),
        compiler_params=_cparams(1),
        name="ffn_final" if final_norm else "ffn",
    )(x2, g, wg, wu, wd, fg)


def _mla_proj_kernel(x_ref, pos_ref, freq_ref, g_ref, wd_ref, qg_ref, wqa_ref, wqb_ref,
                     kg_ref, wk_ref, wv_ref, q_ref, k_ref, v_ref):
    tm = x_ref.shape[0]
    h = _rms(x_ref[...], g_ref[...]).astype(BF16)
    down = _dot(h, wd_ref[...])
    c_q = down[:, :MLA_Q_RANK]
    c_kv = down[:, MLA_Q_RANK:MLA_Q_RANK + MLA_KV_RANK]
    kr = down[:, MLA_Q_RANK + MLA_KV_RANK:MLA_Q_RANK + MLA_KV_RANK + LANES]
    kr_rot = down[:, MLA_Q_RANK + MLA_KV_RANK + LANES:]

    lane = lax.broadcasted_iota(jnp.int32, (tm, LANES), 1)
    is_rope = (lane >= MLA_NOPE_DIM) & (lane < MLA_NOPE_DIM + MLA_ROPE_DIM)
    ang = pos_ref[...].astype(F32) * freq_ref[...]
    cos_p = jnp.where(is_rope, jnp.cos(ang), 1.0)
    sin_p = jnp.where(is_rope, jnp.sin(ang), 0.0)

    cqn = _rms(c_q, qg_ref[...]).astype(BF16)
    ckvn = _rms(c_kv, kg_ref[...]).astype(BF16)
    k_rope = kr * cos_p + kr_rot * sin_p
    qa = _dot(cqn, wqa_ref[...])
    qb = _dot(cqn, wqb_ref[...])
    kn = _dot(ckvn, wk_ref[...])
    for hd in range(MLA_HEADS):
        sl = slice(hd * LANES, (hd + 1) * LANES)
        q_ref[:, sl] = (qa[:, sl] * cos_p + qb[:, sl] * sin_p).astype(q_ref.dtype)
        k_ref[:, sl] = (kn[:, sl] + k_rope).astype(k_ref.dtype)
    v_ref[...] = _dot(ckvn, wv_ref[...]).astype(v_ref.dtype)


def _mla_proj(x2, pos2, freq, g, wd, qg, wqa, wqb, kg, wk, wv, *, tm):
    t, d = x2.shape
    full = lambda a: pl.BlockSpec(a.shape, lambda i: (0,) * a.ndim)
    qk_w = MLA_HEADS * LANES
    v_w = MLA_HEADS * MLA_V_DIM
    return pl.pallas_call(
        _mla_proj_kernel,
        out_shape=(jax.ShapeDtypeStruct((t, qk_w), BF16),
                   jax.ShapeDtypeStruct((t, qk_w), BF16),
                   jax.ShapeDtypeStruct((t, v_w), BF16)),
        grid=(t // tm,),
        in_specs=[pl.BlockSpec((tm, d), lambda i: (i, 0)),
                  pl.BlockSpec((tm, 1), lambda i: (i, 0)),
                  full(freq), full(g), full(wd), full(qg), full(wqa), full(wqb),
                  full(kg), full(wk), full(wv)],
        out_specs=(pl.BlockSpec((tm, qk_w), lambda i: (i, 0)),
                   pl.BlockSpec((tm, qk_w), lambda i: (i, 0)),
                   pl.BlockSpec((tm, v_w), lambda i: (i, 0))),
        compiler_params=_cparams(1),
        name="mla_proj",
    )(x2, pos2, freq, g, wd, qg, wqa, wqb, kg, wk, wv)


def _mla_attn_kernel(q_ref, k_ref, v_ref, o_ref, va_ref, vb_ref, *, tq):
    qi = pl.program_id(2)
    lane_s = lax.broadcasted_iota(jnp.int32, va_ref.shape, 1)

    @pl.when(qi == 0)
    def _():
        v_all = v_ref[0]
        one = jnp.ones_like(v_all)
        va_ref[...] = jnp.where(lane_s < MLA_V_DIM, v_all, one)
        vb_ref[...] = jnp.where(lane_s < MLA_V_DIM, one, v_all)

    row = lax.broadcasted_iota(jnp.int32, (tq, tq), 0)
    col = lax.broadcasted_iota(jnp.int32, (tq, tq), 1)
    causal = col <= row
    lane = lax.broadcasted_iota(jnp.int32, (tq, LANES), 1)
    low = lane < MLA_V_DIM
    c2 = (MLA_NOPE_DIM + MLA_ROPE_DIM) ** -0.5 * LOG2E
    qh = (q_ref[0, :, :LANES], q_ref[0, :, LANES:])
    vaug = (va_ref, vb_ref)

    def head_block(q, kblk, vblk, m, acc, masked):
        s2 = _dot_nt(q, kblk) * c2
        if masked:
            s2 = jnp.where(causal, s2, -jnp.inf)
        m_new = jnp.maximum(m, jnp.max(s2, axis=-1, keepdims=True))
        alpha = jnp.exp2(m - m_new)
        p = jnp.exp2(s2 - m_new)
        acc = alpha * acc + _dot(p.astype(BF16), vblk)
        return m_new, acc

    def step(j, carry, masked):
        start = pl.multiple_of(j * tq, tq)
        out = []
        for hh in range(2):
            kblk = k_ref[0, pl.ds(start, tq), hh * LANES:(hh + 1) * LANES]
            vblk = vaug[hh][pl.ds(start, tq), :]
            out.append(head_block(qh[hh], kblk, vblk, *carry[hh], masked))
        return tuple(out)

    m0 = jnp.full((tq, 1), -jnp.inf, F32)
    a0 = jnp.zeros((tq, LANES), F32)
    carry = step(qi, ((m0, a0), (m0, a0)), True)
    carry = lax.fori_loop(0, qi, lambda j, c: step(j, c, False), carry)
    (_, acc_a), (_, acc_b) = carry
    out_a = acc_a / pltpu.roll(acc_a, MLA_V_DIM, 1)
    out_b = acc_b / pltpu.roll(acc_b, MLA_V_DIM, 1)
    o_ref[0] = jnp.where(low, out_a, out_b).astype(o_ref.dtype)


def _mla_attention(q3, k3, v3, *, tq):
    b, s, _ = q3.shape
    n_pairs = MLA_HEADS // 2
    return pl.pallas_call(
        functools.partial(_mla_attn_kernel, tq=tq),
        out_shape=jax.ShapeDtypeStruct((b, s, MLA_HEADS * MLA_V_DIM), BF16),
        grid=(b, n_pairs, s // tq),
        in_specs=[pl.BlockSpec((1, tq, 2 * LANES), lambda bi, p, i: (bi, i, p)),
                  pl.BlockSpec((1, s, 2 * LANES), lambda bi, p, i: (bi, 0, p)),
                  pl.BlockSpec((1, s, LANES), lambda bi, p, i: (bi, 0, p))],
        out_specs=pl.BlockSpec((1, tq, LANES), lambda bi, p, i: (bi, i, p)),
        scratch_shapes=[pltpu.VMEM((s, LANES), BF16), pltpu.VMEM((s, LANES), BF16)],
        compiler_params=_cparams(3),
        name="mla_attention",
    )(q3, k3, v3)


def _rotate_half_cols(w):
    half = MLA_ROPE_DIM // 2
    return jnp.concatenate([-w[..., half:], w[..., :half]], axis=-1)


def _mla_weight_layout(w_down, w_uq, w_ukv):
    d = w_down.shape[0]
    w_kr = w_down[:, MLA_Q_RANK + MLA_KV_RANK:]
    pad_lo = jnp.zeros((d, MLA_NOPE_DIM), F32)
    pad_hi = jnp.zeros((d, LANES - MLA_NOPE_DIM - MLA_ROPE_DIM), F32)
    wd = jnp.concatenate([w_down[:, :MLA_Q_RANK + MLA_KV_RANK],
                          pad_lo, w_kr, pad_hi,
                          pad_lo, _rotate_half_cols(w_kr), pad_hi], axis=1)

    wq = w_uq.reshape(MLA_Q_RANK, MLA_HEADS, MLA_NOPE_DIM + MLA_ROPE_DIM)
    wq_nope, wq_rope = wq[..., :MLA_NOPE_DIM], wq[..., MLA_NOPE_DIM:]
    zq = jnp.zeros((MLA_Q_RANK, MLA_HEADS, LANES - MLA_NOPE_DIM - MLA_ROPE_DIM), F32)
    wqa = jnp.concatenate([wq_nope, wq_rope, zq], axis=-1).reshape(MLA_Q_RANK, MLA_HEADS * LANES)
    wqb = jnp.concatenate([jnp.zeros_like(wq_nope), _rotate_half_cols(wq_rope), zq],
                          axis=-1).reshape(MLA_Q_RANK, MLA_HEADS * LANES)

    wkv = w_ukv.reshape(MLA_KV_RANK, MLA_HEADS, MLA_NOPE_DIM + MLA_V_DIM)
    wk_nope, wv = wkv[..., :MLA_NOPE_DIM], wkv[..., MLA_NOPE_DIM:]
    wk = jnp.concatenate([wk_nope, jnp.zeros((MLA_KV_RANK, MLA_HEADS, LANES - MLA_NOPE_DIM), F32)],
                         axis=-1).reshape(MLA_KV_RANK, MLA_HEADS * LANES)
    wv = wv.reshape(MLA_KV_RANK, MLA_HEADS * MLA_V_DIM)
    return wd.astype(BF16), wqa.astype(BF16), wqb.astype(BF16), wk.astype(BF16), wv.astype(BF16)


def _rope_lane_freqs():
    half = MLA_ROPE_DIM // 2
    freqs = ROPE_THETA ** (-jnp.arange(half, dtype=F32) / half)
    lane = np.arange(LANES)
    idx = np.clip(lane - MLA_NOPE_DIM, 0, MLA_ROPE_DIM - 1) % half
    return freqs[idx][None, :]


def kernel(x, positions, attn_norm, ffn_norm, even_w_in, even_sinks, even_w_out, rel_bias_table, mla_w_down, mla_q_norm, mla_w_uq, mla_kv_norm, mla_w_ukv, mla_w_o, ffn_w_gate, ffn_w_up, ffn_w_down, final_norm):
    b, s, d = x.shape
    t = b * s
    x2 = x.reshape(t, d)
    row = lambda v: v.reshape(1, -1)

    proj = _norm_proj(x2, row(attn_norm[0]), even_w_in[0].astype(BF16), tm=1024)
    proj3 = proj.reshape(b, s, EVEN_IN_WIDTH)
    o_a = _sb_attention(proj3, tq=512, sub=256)
    o_b = _swa_attention(proj3, rel_bias_table, even_sinks[0].reshape(1, SW_HEADS))
    x2 = _out_proj(x2, [o_a.reshape(t, SB_WIDTH), o_b.reshape(t, SW_Q_WIDTH)],
                   even_w_out[0].astype(BF16), tm=1024)
    x2 = _ffn(x2, row(ffn_norm[0]), ffn_w_gate[0].astype(BF16), ffn_w_up[0].astype(BF16),
              ffn_w_down[0].astype(BF16), row(final_norm), tm=512, final_norm=False)

    wd, wqa, wqb, wk, wv = _mla_weight_layout(mla_w_down[0], mla_w_uq[0], mla_w_ukv[0])
    q, k, v = _mla_proj(x2, positions.reshape(t, 1), _rope_lane_freqs(), row(attn_norm[1]), wd,
                        row(mla_q_norm[0]), wqa, wqb, row(mla_kv_norm[0]), wk, wv, tm=512)
    o = _mla_attention(q.reshape(b, s, -1), k.reshape(b, s, -1), v.reshape(b, s, -1), tq=512)
    x2 = _out_proj(x2, [o.reshape(t, MLA_HEADS * MLA_V_DIM)], mla_w_o[0].astype(BF16), tm=1024)
    x2 = _ffn(x2, row(ffn_norm[1]), ffn_w_gate[1].astype(BF16), ffn_w_up[1].astype(BF16),
              ffn_w_down[1].astype(BF16), row(final_norm), tm=512, final_norm=True)
    return x2.reshape(b, s, d)
```

```python
import functools
import math

import numpy as np
import jax
import jax.numpy as jnp
from jax import lax
from jax.experimental import pallas as pl
from jax.experimental.pallas import tpu as pltpu

F32 = jnp.float32
BF16 = jnp.bfloat16

D_MODEL = 1024
EPS = 1e-6
HEAD_DIM = 64
LANES = 128
SB_HEADS = 8
SW_HEADS = 8
SW_KV_HEADS = 2
WINDOW = 128
SB_WIDTH = SB_HEADS * HEAD_DIM
SW_Q_WIDTH = SW_HEADS * HEAD_DIM
SW_KV_WIDTH = SW_KV_HEADS * HEAD_DIM
EVEN_IN_WIDTH = 3 * SB_WIDTH + SW_Q_WIDTH + 2 * SW_KV_WIDTH
REL_BUCKETS = 32
REL_MAX_DIST = 128
MLA_HEADS = 16
MLA_NOPE_DIM = 64
MLA_ROPE_DIM = 32
MLA_V_DIM = 64
MLA_Q_RANK = 384
MLA_KV_RANK = 256
ROPE_THETA = 10000.0
LOG2E = math.log2(math.e)
VMEM_LIMIT_BYTES = 56 * 1024 * 1024


def _cparams(n_axes):
    return pltpu.CompilerParams(
        dimension_semantics=("arbitrary",) * n_axes,
        vmem_limit_bytes=VMEM_LIMIT_BYTES)


def _rms(x, g):
    return x * lax.rsqrt(jnp.mean(x * x, axis=-1, keepdims=True) + EPS) * g


def _dot(a, b):
    return jnp.dot(a, b, preferred_element_type=F32)


def _dot_nt(a, b):
    return lax.dot_general(a, b, (((1,), (1,)), ((), ())), preferred_element_type=F32)


def _norm_proj_kernel(x_ref, g_ref, w_ref, o_ref):
    h = _rms(x_ref[...], g_ref[...]).astype(BF16)
    o_ref[...] = _dot(h, w_ref[...]).astype(o_ref.dtype)


def _norm_proj(x2, g, w, *, tm):
    t, d = x2.shape
    n = w.shape[1]
    return pl.pallas_call(
        _norm_proj_kernel,
        out_shape=jax.ShapeDtypeStruct((t, n), BF16),
        grid=(t // tm,),
        in_specs=[pl.BlockSpec((tm, d), lambda i: (i, 0)),
                  pl.BlockSpec((1, d), lambda i: (0, 0)),
                  pl.BlockSpec((d, n), lambda i: (0, 0))],
        out_specs=pl.BlockSpec((tm, n), lambda i: (i, 0)),
        compiler_params=_cparams(1),
        name="norm_proj",
    )(x2, g, w)


def _softplus(z):
    neg_abs = lax.bitcast_convert_type(
        lax.bitcast_convert_type(z, jnp.uint32) | jnp.uint32(0x80000000), F32)
    return jnp.maximum(z, 0.0) + jnp.log(1.0 + jnp.exp(neg_abs))


def _sb_kernel(q_ref, k_ref, v_ref, o_ref, r_ref, acc_ref, *, tq, sub, unroll):
    qi = pl.program_id(2)
    r = tq // sub
    q2 = q_ref[0]
    lane = lax.broadcasted_iota(jnp.int32, (tq, LANES), 1)
    low = lane < HEAD_DIM
    zero = jnp.zeros_like(q2)
    scale = HEAD_DIM ** -0.5
    qm = (jnp.where(low, q2, zero) * scale, jnp.where(low, zero, q2) * scale)

    tri_row = lax.broadcasted_iota(jnp.int32, (sub, sub), 0)
    tri_col = lax.broadcasted_iota(jnp.int32, (sub, sub), 1)
    tri = (tri_row >= tri_col).astype(BF16)

    r_ref[...] = jnp.zeros(r_ref.shape, F32)
    acc_ref[...] = jnp.zeros(acc_ref.shape, F32)

    def tile(hh, r0, nrows, j, masked):
        rows = pl.ds(r0, nrows)
        start = pl.multiple_of(j * sub, sub)
        kblk = k_ref[0, pl.ds(start, sub), :]
        vblk = v_ref[0, pl.ds(start, sub), :]
        z = _dot_nt(qm[hh][r0:r0 + nrows], kblk)
        sp = _softplus(z)
        if masked:
            row = lax.broadcasted_iota(jnp.int32, (nrows, sub), 0)
            col = lax.broadcasted_iota(jnp.int32, (nrows, sub), 1)
            causal = col < row
            sp = jnp.where(causal, sp, 0.0)
        later = r_ref[hh, rows, :]
        c = _dot(sp.astype(BF16), tri) + jnp.tile(later, (1, sub // LANES))
        w = jnp.exp(z - c)
        if masked:
            w = jnp.where(causal, w, 0.0)
        acc_ref[hh, rows, :] += _dot(w.astype(BF16), vblk)
        r_ref[hh, rows, :] = jnp.broadcast_to(c[:, 0:1], (nrows, LANES))

    for u in reversed(range(r)):
        for hh in range(2):
            tile(hh, u * sub, tq - u * sub, qi * r + u, True)

    def body(jj, carry):
        for t in range(unroll):
            for hh in range(2):
                tile(hh, 0, tq, qi * r - 1 - (jj * unroll + t), False)
        return carry

    lax.fori_loop(0, (qi * r) // unroll, body, 0)
    o_ref[0] = jnp.where(low, acc_ref[0], acc_ref[1]).astype(o_ref.dtype)


def _sb_attention(proj3, *, tq, sub, unroll):
    b, s, _ = proj3.shape
    assert (tq // sub) % unroll == 0
    n_pairs = SB_WIDTH // LANES
    k_off = SB_WIDTH // LANES
    v_off = 2 * SB_WIDTH // LANES
    return pl.pallas_call(
        functools.partial(_sb_kernel, tq=tq, sub=sub, unroll=unroll),
        out_shape=jax.ShapeDtypeStruct((b, s, SB_WIDTH), BF16),
        grid=(b, n_pairs, s // tq),
        in_specs=[pl.BlockSpec((1, tq, LANES), lambda bi, p, i: (bi, i, p)),
                  pl.BlockSpec((1, s, LANES), lambda bi, p, i: (bi, 0, k_off + p)),
                  pl.BlockSpec((1, s, LANES), lambda bi, p, i: (bi, 0, v_off + p))],
        out_specs=pl.BlockSpec((1, tq, LANES), lambda bi, p, i: (bi, i, p)),
        scratch_shapes=[pltpu.VMEM((2, tq, LANES), F32), pltpu.VMEM((2, tq, LANES), F32)],
        compiler_params=_cparams(3),
        name="sb_attention",
    )(proj3, proj3, proj3)


def _t5_bucket_table():
    max_exact = REL_BUCKETS // 2
    rel = WINDOW + np.arange(WINDOW)[:, None] - np.arange(2 * WINDOW)[None, :]
    rel = np.maximum(rel, 0)
    relf = np.maximum(rel, 1).astype(np.float32)
    large = max_exact + (np.log(relf / np.float32(max_exact)) / np.float32(math.log(REL_MAX_DIST / max_exact))
                         * np.float32(REL_BUCKETS - max_exact)).astype(np.int32)
    large = np.minimum(large, REL_BUCKETS - 1)
    return np.where(rel < max_exact, rel, large).astype(np.int32)


def _swap_halves(x):
    return jnp.concatenate([x[:, HEAD_DIM:], x[:, :HEAD_DIM]], axis=1)


def _swa_kernel(table_ref, sinks_ref, bucket_ref, q_ref, kp_ref, kc_ref, vp_ref, vc_ref,
                o_ref, bias_ref):
    bi = pl.program_id(0)
    n = pl.program_id(1)
    blk = WINDOW

    @pl.when(jnp.logical_and(bi == 0, n == 0))
    def _():
        bucket = bucket_ref[...]
        for h in range(SW_HEADS):
            bias = jnp.zeros((blk, 2 * blk), F32)
            for bk in range(REL_BUCKETS):
                bias = jnp.where(bucket == bk, table_ref[bk, h], bias)
            bias_ref[h] = bias

    t = lax.broadcasted_iota(jnp.int32, (blk, 2 * blk), 0)
    s = lax.broadcasted_iota(jnp.int32, (blk, 2 * blk), 1)
    rel = blk + t - s
    valid = (rel >= 0) & (rel < WINDOW) & ((s >= blk) | (n > 0))

    kk = jnp.concatenate([kp_ref[0], kc_ref[0]], axis=0)
    vv = jnp.concatenate([vp_ref[0], vc_ref[0]], axis=0)
    kk_sw = _swap_halves(kk)
    vv_sw = _swap_halves(vv)
    lane = lax.broadcasted_iota(jnp.int32, (blk, LANES), 1)
    low = lane < HEAD_DIM
    scale = HEAD_DIM ** -0.5
    group = SW_HEADS // SW_KV_HEADS

    for p in range(SW_HEADS // 2):
        q2 = q_ref[0, :, p * LANES:(p + 1) * LANES]
        zero = jnp.zeros_like(q2)
        outs = []
        for half in range(2):
            h = 2 * p + half
            kv = h // group
            qh = (jnp.where(low, q2, zero) if half == 0 else jnp.where(low, zero, q2)) * scale
            k_use = kk if kv == half else kk_sw
            v_use = vv if kv == half else vv_sw
            logits = _dot_nt(qh, k_use) + bias_ref[h]
            sink = sinks_ref[0, h]
            m = jnp.maximum(jnp.max(jnp.where(valid, logits, -jnp.inf), axis=-1, keepdims=True), sink)
            pr = jnp.where(valid, jnp.exp(logits - m), 0.0)
            denom = jnp.sum(pr, axis=-1, keepdims=True) + jnp.exp(sink - m)
            outs.append(_dot(pr.astype(BF16), v_use) / denom)
        o_ref[0, :, p * LANES:(p + 1) * LANES] = jnp.where(low, outs[0], outs[1]).astype(o_ref.dtype)


def _swa_attention(proj3, table, sinks):
    b, s, _ = proj3.shape
    nb = s // WINDOW
    q_blk = (3 * SB_WIDTH) // SW_Q_WIDTH
    k_blk = (3 * SB_WIDTH + SW_Q_WIDTH) // LANES
    v_blk = k_blk + 1
    bucket = jnp.asarray(_t5_bucket_table())
    smem = pl.BlockSpec(memory_space=pltpu.SMEM)
    return pl.pallas_call(
        _swa_kernel,
        out_shape=jax.ShapeDtypeStruct((b, s, SW_Q_WIDTH), BF16),
        grid=(b, nb),
        in_specs=[smem, smem,
                  pl.BlockSpec((WINDOW, 2 * WINDOW), lambda bi, n: (0, 0)),
                  pl.BlockSpec((1, WINDOW, SW_Q_WIDTH), lambda bi, n: (bi, n, q_blk)),
                  pl.BlockSpec((1, WINDOW, LANES), lambda bi, n: (bi, jnp.maximum(n - 1, 0), k_blk)),
                  pl.BlockSpec((1, WINDOW, LANES), lambda bi, n: (bi, n, k_blk)),
                  pl.BlockSpec((1, WINDOW, LANES), lambda bi, n: (bi, jnp.maximum(n - 1, 0), v_blk)),
                  pl.BlockSpec((1, WINDOW, LANES), lambda bi, n: (bi, n, v_blk))],
        out_specs=pl.BlockSpec((1, WINDOW, SW_Q_WIDTH), lambda bi, n: (bi, n, 0)),
        scratch_shapes=[pltpu.VMEM((SW_HEADS, WINDOW, 2 * WINDOW), F32)],
        compiler_params=_cparams(2),
        name="swa_attention",
    )(table, sinks, bucket, proj3, proj3, proj3, proj3, proj3)


def _out_proj_kernel(*refs, n_parts):
    x_ref = refs[0]
    part_refs = refs[1:1 + n_parts]
    w_refs = refs[1 + n_parts:1 + 2 * n_parts]
    o_ref = refs[1 + 2 * n_parts]
    acc = x_ref[...]
    for a_ref, w_ref in zip(part_refs, w_refs):
        acc = acc + _dot(a_ref[...], w_ref[...])
    o_ref[...] = acc


def _out_proj(x2, parts, w, *, tm):
    t, d = x2.shape
    n_parts = len(parts)
    kp = w.shape[0] // n_parts
    in_specs = [pl.BlockSpec((tm, d), lambda i: (i, 0))]
    in_specs += [pl.BlockSpec((tm, kp), lambda i: (i, 0)) for _ in parts]
    in_specs += [pl.BlockSpec((kp, d), functools.partial(lambda i, c: (c, 0), c=c)) for c in range(n_parts)]
    return pl.pallas_call(
        functools.partial(_out_proj_kernel, n_parts=n_parts),
        out_shape=jax.ShapeDtypeStruct((t, d), F32),
        grid=(t // tm,),
        in_specs=in_specs,
        out_specs=pl.BlockSpec((tm, d), lambda i: (i, 0)),
        compiler_params=_cparams(1),
        name="out_proj",
    )(x2, *parts, *([w] * n_parts))


def _ffn_kernel(x_ref, g_ref, wg_ref, wu_ref, wd_ref, fg_ref, o_ref, *, final_norm):
    x = x_ref[...]
    h = _rms(x, g_ref[...]).astype(BF16)
    gate = _dot(h, wg_ref[...])
    up = _dot(h, wu_ref[...])
    act = (gate / (1.0 + jnp.exp(-gate)) * up).astype(BF16)
    y = x + _dot(act, wd_ref[...])
    if final_norm:
        y = _rms(y, fg_ref[...])
    o_ref[...] = y


def _ffn(x2, g, wg, wu, wd, fg, *, tm, final_norm):
    t, d = x2.shape
    hid = wg.shape[1]
    resident = dict(pipeline_mode=pl.Buffered(1))
    return pl.pallas_call(
        functools.partial(_ffn_kernel, final_norm=final_norm),
        out_shape=jax.ShapeDtypeStruct((t, d), F32),
        grid=(t // tm,),
        in_specs=[pl.BlockSpec((tm, d), lambda i: (i, 0)),
                  pl.BlockSpec((1, d), lambda i: (0, 0)),
                  pl.BlockSpec((d, hid), lambda i: (0, 0), **resident),
                  pl.BlockSpec((d, hid), lambda i: (0, 0), **resident),
                  pl.BlockSpec((hid, d), lambda i: (0, 0), **resident),
                  pl.BlockSpec((1, d), lambda i: (0, 0))],
        out_specs=pl.BlockSpec((tm, d), lambda i: (i, 0)),
        compiler_params=_cparams(1),
        name="ffn_final" if final_norm else "ffn",
    )(x2, g, wg, wu, wd, fg)


def _mla_proj_kernel(x_ref, pos_ref, freq_ref, g_ref, wd_ref, qg_ref, wqa_ref, wqb_ref,
                     kg_ref, wk_ref, wv_ref, q_ref, k_ref, v_ref):
    tm = x_ref.shape[0]
    h = _rms(x_ref[...], g_ref[...]).astype(BF16)
    down = _dot(h, wd_ref[...])
    c_q = down[:, :MLA_Q_RANK]
    c_kv = down[:, MLA_Q_RANK:MLA_Q_RANK + MLA_KV_RANK]
    kr = down[:, MLA_Q_RANK + MLA_KV_RANK:MLA_Q_RANK + MLA_KV_RANK + LANES]
    kr_rot = down[:, MLA_Q_RANK + MLA_KV_RANK + LANES:]

    lane = lax.broadcasted_iota(jnp.int32, (tm, LANES), 1)
    is_rope = (lane >= MLA_NOPE_DIM) & (lane < MLA_NOPE_DIM + MLA_ROPE_DIM)
    ang = pos_ref[...].astype(F32) * freq_ref[...]
    cos_p = jnp.where(is_rope, jnp.cos(ang), 1.0)
    sin_p = jnp.where(is_rope, jnp.sin(ang), 0.0)

    cqn = _rms(c_q, qg_ref[...]).astype(BF16)
    ckvn = _rms(c_kv, kg_ref[...]).astype(BF16)
    k_rope = kr * cos_p + kr_rot * sin_p
    qa = _dot(cqn, wqa_ref[...])
    qb = _dot(cqn, wqb_ref[...])
    kn = _dot(ckvn, wk_ref[...])
    for hd in range(MLA_HEADS):
        sl = slice(hd * LANES, (hd + 1) * LANES)
        q_ref[:, sl] = (qa[:, sl] * cos_p + qb[:, sl] * sin_p).astype(q_ref.dtype)
        k_ref[:, sl] = (kn[:, sl] + k_rope).astype(k_ref.dtype)
    v_ref[...] = _dot(ckvn, wv_ref[...]).astype(v_ref.dtype)


def _mla_proj(x2, pos2, freq, g, wd, qg, wqa, wqb, kg, wk, wv, *, tm):
    t, d = x2.shape
    full = lambda a: pl.BlockSpec(a.shape, lambda i: (0,) * a.ndim)
    qk_w = MLA_HEADS * LANES
    v_w = MLA_HEADS * MLA_V_DIM
    return pl.pallas_call(
        _mla_proj_kernel,
        out_shape=(jax.ShapeDtypeStruct((t, qk_w), BF16),
                   jax.ShapeDtypeStruct((t, qk_w), BF16),
                   jax.ShapeDtypeStruct((t, v_w), BF16)),
        grid=(t // tm,),
        in_specs=[pl.BlockSpec((tm, d), lambda i: (i, 0)),
                  pl.BlockSpec((tm, 1), lambda i: (i, 0)),
                  full(freq), full(g), full(wd), full(qg), full(wqa), full(wqb),
                  full(kg), full(wk), full(wv)],
        out_specs=(pl.BlockSpec((tm, qk_w), lambda i: (i, 0)),
                   pl.BlockSpec((tm, qk_w), lambda i: (i, 0)),
                   pl.BlockSpec((tm, v_w), lambda i: (i, 0))),
        compiler_params=_cparams(1),
        name="mla_proj",
    )(x2, pos2, freq, g, wd, qg, wqa, wqb, kg, wk, wv)


def _mla_attn_kernel(q_ref, k_ref, v_ref, o_ref, va_ref, vb_ref, m_ref, acc_ref, *, tq, tk, unroll):
    qi = pl.program_id(2)
    lane_s = lax.broadcasted_iota(jnp.int32, va_ref.shape, 1)

    @pl.when(qi == 0)
    def _():
        v_all = v_ref[0]
        one = jnp.ones_like(v_all)
        va_ref[...] = jnp.where(lane_s < MLA_V_DIM, v_all, one)
        vb_ref[...] = jnp.where(lane_s < MLA_V_DIM, one, v_all)

    r = tq // tk
    c2 = (MLA_NOPE_DIM + MLA_ROPE_DIM) ** -0.5 * LOG2E
    vaug = (va_ref, vb_ref)

    m_ref[...] = jnp.full(m_ref.shape, -jnp.inf, F32)
    acc_ref[...] = jnp.zeros(acc_ref.shape, F32)

    def tile(hh, r0, nrows, j, masked):
        rows = pl.ds(r0, nrows)
        q = q_ref[0, rows, hh * LANES:(hh + 1) * LANES]
        start = pl.multiple_of(j * tk, tk)
        kblk = k_ref[0, pl.ds(start, tk), hh * LANES:(hh + 1) * LANES]
        vblk = vaug[hh][pl.ds(start, tk), :]
        s2 = _dot_nt(q, kblk) * c2
        if masked:
            row = lax.broadcasted_iota(jnp.int32, (nrows, tk), 0)
            col = lax.broadcasted_iota(jnp.int32, (nrows, tk), 1)
            s2 = jnp.where(col <= row, s2, -jnp.inf)
        m = m_ref[hh, rows, :]
        m_new = jnp.maximum(m, jnp.max(s2, axis=-1, keepdims=True))
        alpha = jnp.exp2(m - m_new)
        p = jnp.exp2(s2 - jnp.tile(m_new, (1, tk // LANES)))
        acc_ref[hh, rows, :] = alpha * acc_ref[hh, rows, :] + _dot(p.astype(BF16), vblk)
        m_ref[hh, rows, :] = m_new

    def body(jj, carry):
        for t in range(unroll):
            for hh in range(2):
                tile(hh, 0, tq, jj * unroll + t, False)
        return carry

    lax.fori_loop(0, (qi * r) // unroll, body, 0)
    for u in range(r):
        for hh in range(2):
            tile(hh, u * tk, tq - u * tk, qi * r + u, True)

    lane = lax.broadcasted_iota(jnp.int32, (tq, LANES), 1)
    acc_a = acc_ref[0]
    acc_b = acc_ref[1]
    out_a = acc_a / pltpu.roll(acc_a, MLA_V_DIM, 1)
    out_b = acc_b / pltpu.roll(acc_b, MLA_V_DIM, 1)
    o_ref[0] = jnp.where(lane < MLA_V_DIM, out_a, out_b).astype(o_ref.dtype)


def _mla_attention(q3, k3, v3, *, tq, tk, unroll):
    b, s, _ = q3.shape
    assert (tq // tk) % unroll == 0
    n_pairs = MLA_HEADS // 2
    return pl.pallas_call(
        functools.partial(_mla_attn_kernel, tq=tq, tk=tk, unroll=unroll),
        out_shape=jax.ShapeDtypeStruct((b, s, MLA_HEADS * MLA_V_DIM), BF16),
        grid=(b, n_pairs, s // tq),
        in_specs=[pl.BlockSpec((1, tq, 2 * LANES), lambda bi, p, i: (bi, i, p)),
                  pl.BlockSpec((1, s, 2 * LANES), lambda bi, p, i: (bi, 0, p)),
                  pl.BlockSpec((1, s, LANES), lambda bi, p, i: (bi, 0, p))],
        out_specs=pl.BlockSpec((1, tq, LANES), lambda bi, p, i: (bi, i, p)),
        scratch_shapes=[pltpu.VMEM((s, LANES), BF16), pltpu.VMEM((s, LANES), BF16),
                        pltpu.VMEM((2, tq, LANES), F32), pltpu.VMEM((2, tq, LANES), F32)],
        compiler_params=_cparams(3),
        name="mla_attention",
    )(q3, k3, v3)


def _rotate_half_cols(w):
    half = MLA_ROPE_DIM // 2
    return jnp.concatenate([-w[..., half:], w[..., :half]], axis=-1)


def _mla_weight_layout(w_down, w_uq, w_ukv):
    d = w_down.shape[0]
    w_kr = w_down[:, MLA_Q_RANK + MLA_KV_RANK:]
    pad_lo = jnp.zeros((d, MLA_NOPE_DIM), F32)
    pad_hi = jnp.zeros((d, LANES - MLA_NOPE_DIM - MLA_ROPE_DIM), F32)
    wd = jnp.concatenate([w_down[:, :MLA_Q_RANK + MLA_KV_RANK],
                          pad_lo, w_kr, pad_hi,
                          pad_lo, _rotate_half_cols(w_kr), pad_hi], axis=1)

    wq = w_uq.reshape(MLA_Q_RANK, MLA_HEADS, MLA_NOPE_DIM + MLA_ROPE_DIM)
    wq_nope, wq_rope = wq[..., :MLA_NOPE_DIM], wq[..., MLA_NOPE_DIM:]
    zq = jnp.zeros((MLA_Q_RANK, MLA_HEADS, LANES - MLA_NOPE_DIM - MLA_ROPE_DIM), F32)
    wqa = jnp.concatenate([wq_nope, wq_rope, zq], axis=-1).reshape(MLA_Q_RANK, MLA_HEADS * LANES)
    wqb = jnp.concatenate([jnp.zeros_like(wq_nope), _rotate_half_cols(wq_rope), zq],
                          axis=-1).reshape(MLA_Q_RANK, MLA_HEADS * LANES)

    wkv = w_ukv.reshape(MLA_KV_RANK, MLA_HEADS, MLA_NOPE_DIM + MLA_V_DIM)
    wk_nope, wv = wkv[..., :MLA_NOPE_DIM], wkv[..., MLA_NOPE_DIM:]
    wk = jnp.concatenate([wk_nope, jnp.zeros((MLA_KV_RANK, MLA_HEADS, LANES - MLA_NOPE_DIM), F32)],
                         axis=-1).reshape(MLA_KV_RANK, MLA_HEADS * LANES)
    wv = wv.reshape(MLA_KV_RANK, MLA_HEADS * MLA_V_DIM)
    return wd.astype(BF16), wqa.astype(BF16), wqb.astype(BF16), wk.astype(BF16), wv.astype(BF16)


def _rope_lane_freqs():
    half = MLA_ROPE_DIM // 2
    freqs = ROPE_THETA ** (-jnp.arange(half, dtype=F32) / half)
    lane = np.arange(LANES)
    idx = np.clip(lane - MLA_NOPE_DIM, 0, MLA_ROPE_DIM - 1) % half
    return freqs[idx][None, :]


def kernel(x, positions, attn_norm, ffn_norm, even_w_in, even_sinks, even_w_out, rel_bias_table, mla_w_down, mla_q_norm, mla_w_uq, mla_kv_norm, mla_w_ukv, mla_w_o, ffn_w_gate, ffn_w_up, ffn_w_down, final_norm):
    b, s, d = x.shape
    t = b * s
    x2 = x.reshape(t, d)
    row = lambda v: v.reshape(1, -1)

    proj = _norm_proj(x2, row(attn_norm[0]), even_w_in[0].astype(BF16), tm=1024)
    proj3 = proj.reshape(b, s, EVEN_IN_WIDTH)
    o_a = _sb_attention(proj3, tq=1024, sub=256, unroll=4)
    o_b = _swa_attention(proj3, rel_bias_table, even_sinks[0].reshape(1, SW_HEADS))
    x2 = _out_proj(x2, [o_a.reshape(t, SB_WIDTH), o_b.reshape(t, SW_Q_WIDTH)],
                   even_w_out[0].astype(BF16), tm=1024)
    x2 = _ffn(x2, row(ffn_norm[0]), ffn_w_gate[0].astype(BF16), ffn_w_up[0].astype(BF16),
              ffn_w_down[0].astype(BF16), row(final_norm), tm=512, final_norm=False)

    wd, wqa, wqb, wk, wv = _mla_weight_layout(mla_w_down[0], mla_w_uq[0], mla_w_ukv[0])
    q, k, v = _mla_proj(x2, positions.reshape(t, 1), _rope_lane_freqs(), row(attn_norm[1]), wd,
                        row(mla_q_norm[0]), wqa, wqb, row(mla_kv_norm[0]), wk, wv, tm=512)
    o = _mla_attention(q.reshape(b, s, -1), k.reshape(b, s, -1), v.reshape(b, s, -1), tq=1024, tk=512, unroll=2)
    x2 = _out_proj(x2, [o.reshape(t, MLA_HEADS * MLA_V_DIM)], mla_w_o[0].astype(BF16), tm=1024)
    x2 = _ffn(x2, row(ffn_norm[1]), ffn_w_gate[1].astype(BF16), ffn_w_up[1].astype(BF16),
              ffn_w_down[1].astype(BF16), row(final_norm), tm=512, final_norm=True)
    return x2.reshape(b, s, d)
```

```python
import functools
import math

import numpy as np
import jax
import jax.numpy as jnp
from jax import lax
from jax.experimental import pallas as pl
from jax.experimental.pallas import tpu as pltpu

F32 = jnp.float32
BF16 = jnp.bfloat16

D_MODEL = 1024
EPS = 1e-6
HEAD_DIM = 64
LANES = 128
SB_HEADS = 8
SW_HEADS = 8
SW_KV_HEADS = 2
WINDOW = 128
SB_WIDTH = SB_HEADS * HEAD_DIM
SW_Q_WIDTH = SW_HEADS * HEAD_DIM
SW_KV_WIDTH = SW_KV_HEADS * HEAD_DIM
EVEN_IN_WIDTH = 3 * SB_WIDTH + SW_Q_WIDTH + 2 * SW_KV_WIDTH
REL_BUCKETS = 32
REL_MAX_DIST = 128
MLA_HEADS = 16
MLA_NOPE_DIM = 64
MLA_ROPE_DIM = 32
MLA_V_DIM = 64
MLA_Q_RANK = 384
MLA_KV_RANK = 256
ROPE_THETA = 10000.0
LOG2E = math.log2(math.e)
SB_SATURATED = 128.0
VMEM_LIMIT_BYTES = 56 * 1024 * 1024


def _cparams(n_axes):
    return pltpu.CompilerParams(
        dimension_semantics=("arbitrary",) * n_axes,
        vmem_limit_bytes=VMEM_LIMIT_BYTES)


def _rms(x, g):
    return x * lax.rsqrt(jnp.mean(x * x, axis=-1, keepdims=True) + EPS) * g


def _dot(a, b):
    return jnp.dot(a, b, preferred_element_type=F32)


def _dot_nt(a, b):
    return lax.dot_general(a, b, (((1,), (1,)), ((), ())), preferred_element_type=F32)


def _norm_proj_kernel(x_ref, g_ref, w_ref, o_ref):
    h = _rms(x_ref[...], g_ref[...]).astype(BF16)
    o_ref[...] = _dot(h, w_ref[...]).astype(o_ref.dtype)


def _norm_proj(x2, g, w, *, tm):
    t, d = x2.shape
    n = w.shape[1]
    return pl.pallas_call(
        _norm_proj_kernel,
        out_shape=jax.ShapeDtypeStruct((t, n), BF16),
        grid=(t // tm,),
        in_specs=[pl.BlockSpec((tm, d), lambda i: (i, 0)),
                  pl.BlockSpec((1, d), lambda i: (0, 0)),
                  pl.BlockSpec((d, n), lambda i: (0, 0))],
        out_specs=pl.BlockSpec((tm, n), lambda i: (i, 0)),
        compiler_params=_cparams(1),
        name="norm_proj",
    )(x2, g, w)


def _softplus(z):
    neg_abs = lax.bitcast_convert_type(
        lax.bitcast_convert_type(z, jnp.uint32) | jnp.uint32(0x80000000), F32)
    return jnp.maximum(z, 0.0) + jnp.log(1.0 + jnp.exp(neg_abs))


def _sb_kernel(q_ref, k_ref, v_ref, o_ref, r_ref, acc_ref, *, tq, sub, unroll):
    qi = pl.program_id(2)
    r = tq // sub
    q2 = q_ref[0]
    lane = lax.broadcasted_iota(jnp.int32, (tq, LANES), 1)
    low = lane < HEAD_DIM
    zero = jnp.zeros_like(q2)
    scale = HEAD_DIM ** -0.5
    qm = (jnp.where(low, q2, zero) * scale, jnp.where(low, zero, q2) * scale)

    tri_row = lax.broadcasted_iota(jnp.int32, (sub, sub), 0)
    tri_col = lax.broadcasted_iota(jnp.int32, (sub, sub), 1)
    tri = (tri_row >= tri_col).astype(BF16)

    r_ref[...] = jnp.zeros(r_ref.shape, F32)
    acc_ref[...] = jnp.zeros(acc_ref.shape, F32)

    def tile(hh, r0, nrows, j, masked):
        rows = pl.ds(r0, nrows)
        start = pl.multiple_of(j * sub, sub)
        kblk = k_ref[0, pl.ds(start, sub), :]
        vblk = v_ref[0, pl.ds(start, sub), :]
        z = _dot_nt(qm[hh][r0:r0 + nrows], kblk)
        sp = _softplus(z)
        if masked:
            row = lax.broadcasted_iota(jnp.int32, (nrows, sub), 0)
            col = lax.broadcasted_iota(jnp.int32, (nrows, sub), 1)
            causal = col < row
            sp = jnp.where(causal, sp, 0.0)
        later = r_ref[hh, rows, :]
        c = _dot(sp.astype(BF16), tri) + jnp.tile(later, (1, sub // LANES))
        w = jnp.exp(z - c)
        if masked:
            w = jnp.where(causal, w, 0.0)
        acc_ref[hh, rows, :] += _dot(w.astype(BF16), vblk)
        r_ref[hh, rows, :] = jnp.broadcast_to(c[:, 0:1], (nrows, LANES))

    for u in reversed(range(r)):
        for hh in range(2):
            tile(hh, u * sub, tq - u * sub, qi * r + u, True)

    def alive(hh):
        return jnp.min(r_ref[hh, :, :]) < SB_SATURATED

    n_iter = (qi * r) // unroll

    def cond(state):
        jj, a0, a1 = state
        return jnp.logical_and(jj < n_iter, jnp.logical_or(a0, a1))

    def body(state):
        jj, a0, a1 = state
        for hh, a in ((0, a0), (1, a1)):
            @pl.when(a)
            def _():
                for t in range(unroll):
                    tile(hh, 0, tq, qi * r - 1 - (jj * unroll + t), False)
        return jj + 1, alive(0), alive(1)

    lax.while_loop(cond, body, (jnp.int32(0), alive(0), alive(1)))
    o_ref[0] = jnp.where(low, acc_ref[0], acc_ref[1]).astype(o_ref.dtype)


def _sb_attention(proj3, *, tq, sub, unroll):
    b, s, _ = proj3.shape
    assert (tq // sub) % unroll == 0
    n_pairs = SB_WIDTH // LANES
    k_off = SB_WIDTH // LANES
    v_off = 2 * SB_WIDTH // LANES
    return pl.pallas_call(
        functools.partial(_sb_kernel, tq=tq, sub=sub, unroll=unroll),
        out_shape=jax.ShapeDtypeStruct((b, s, SB_WIDTH), BF16),
        grid=(b, n_pairs, s // tq),
        in_specs=[pl.BlockSpec((1, tq, LANES), lambda bi, p, i: (bi, i, p)),
                  pl.BlockSpec((1, s, LANES), lambda bi, p, i: (bi, 0, k_off + p)),
                  pl.BlockSpec((1, s, LANES), lambda bi, p, i: (bi, 0, v_off + p))],
        out_specs=pl.BlockSpec((1, tq, LANES), lambda bi, p, i: (bi, i, p)),
        scratch_shapes=[pltpu.VMEM((2, tq, LANES), F32), pltpu.VMEM((2, tq, LANES), F32)],
        compiler_params=_cparams(3),
        name="sb_attention",
    )(proj3, proj3, proj3)


def _t5_bucket_table():
    max_exact = REL_BUCKETS // 2
    rel = WINDOW + np.arange(WINDOW)[:, None] - np.arange(2 * WINDOW)[None, :]
    rel = np.maximum(rel, 0)
    relf = np.maximum(rel, 1).astype(np.float32)
    large = max_exact + (np.log(relf / np.float32(max_exact)) / np.float32(math.log(REL_MAX_DIST / max_exact))
                         * np.float32(REL_BUCKETS - max_exact)).astype(np.int32)
    large = np.minimum(large, REL_BUCKETS - 1)
    return np.where(rel < max_exact, rel, large).astype(np.int32)


def _swap_halves(x):
    return jnp.concatenate([x[:, HEAD_DIM:], x[:, :HEAD_DIM]], axis=1)


def _swa_kernel(table_ref, sinks_ref, bucket_ref, q_ref, kp_ref, kc_ref, vp_ref, vc_ref,
                o_ref, bias_ref):
    bi = pl.program_id(0)
    n = pl.program_id(1)
    blk = WINDOW

    @pl.when(jnp.logical_and(bi == 0, n == 0))
    def _():
        bucket = bucket_ref[...]
        for h in range(SW_HEADS):
            bias = jnp.zeros((blk, 2 * blk), F32)
            for bk in range(REL_BUCKETS):
                bias = jnp.where(bucket == bk, table_ref[bk, h], bias)
            bias_ref[h] = bias

    t = lax.broadcasted_iota(jnp.int32, (blk, 2 * blk), 0)
    s = lax.broadcasted_iota(jnp.int32, (blk, 2 * blk), 1)
    rel = blk + t - s
    valid = (rel >= 0) & (rel < WINDOW) & ((s >= blk) | (n > 0))

    kk = jnp.concatenate([kp_ref[0], kc_ref[0]], axis=0)
    vv = jnp.concatenate([vp_ref[0], vc_ref[0]], axis=0)
    kk_sw = _swap_halves(kk)
    vv_sw = _swap_halves(vv)
    lane = lax.broadcasted_iota(jnp.int32, (blk, LANES), 1)
    low = lane < HEAD_DIM
    scale = HEAD_DIM ** -0.5
    group = SW_HEADS // SW_KV_HEADS

    for p in range(SW_HEADS // 2):
        q2 = q_ref[0, :, p * LANES:(p + 1) * LANES]
        zero = jnp.zeros_like(q2)
        outs = []
        for half in range(2):
            h = 2 * p + half
            kv = h // group
            qh = (jnp.where(low, q2, zero) if half == 0 else jnp.where(low, zero, q2)) * scale
            k_use = kk if kv == half else kk_sw
            v_use = vv if kv == half else vv_sw
            logits = _dot_nt(qh, k_use) + bias_ref[h]
            sink = sinks_ref[0, h]
            m = jnp.maximum(jnp.max(jnp.where(valid, logits, -jnp.inf), axis=-1, keepdims=True), sink)
            pr = jnp.where(valid, jnp.exp(logits - m), 0.0)
            denom = jnp.sum(pr, axis=-1, keepdims=True) + jnp.exp(sink - m)
            outs.append(_dot(pr.astype(BF16), v_use) / denom)
        o_ref[0, :, p * LANES:(p + 1) * LANES] = jnp.where(low, outs[0], outs[1]).astype(o_ref.dtype)


def _swa_attention(proj3, table, sinks):
    b, s, _ = proj3.shape
    nb = s // WINDOW
    q_blk = (3 * SB_WIDTH) // SW_Q_WIDTH
    k_blk = (3 * SB_WIDTH + SW_Q_WIDTH) // LANES
    v_blk = k_blk + 1
    bucket = jnp.asarray(_t5_bucket_table())
    smem = pl.BlockSpec(memory_space=pltpu.SMEM)
    return pl.pallas_call(
        _swa_kernel,
        out_shape=jax.ShapeDtypeStruct((b, s, SW_Q_WIDTH), BF16),
        grid=(b, nb),
        in_specs=[smem, smem,
                  pl.BlockSpec((WINDOW, 2 * WINDOW), lambda bi, n: (0, 0)),
                  pl.BlockSpec((1, WINDOW, SW_Q_WIDTH), lambda bi, n: (bi, n, q_blk)),
                  pl.BlockSpec((1, WINDOW, LANES), lambda bi, n: (bi, jnp.maximum(n - 1, 0), k_blk)),
                  pl.BlockSpec((1, WINDOW, LANES), lambda bi, n: (bi, n, k_blk)),
                  pl.BlockSpec((1, WINDOW, LANES), lambda bi, n: (bi, jnp.maximum(n - 1, 0), v_blk)),
                  pl.BlockSpec((1, WINDOW, LANES), lambda bi, n: (bi, n, v_blk))],
        out_specs=pl.BlockSpec((1, WINDOW, SW_Q_WIDTH), lambda bi, n: (bi, n, 0)),
        scratch_shapes=[pltpu.VMEM((SW_HEADS, WINDOW, 2 * WINDOW), F32)],
        compiler_params=_cparams(2),
        name="swa_attention",
    )(table, sinks, bucket, proj3, proj3, proj3, proj3, proj3)


def _out_proj_kernel(*refs, n_parts):
    x_ref = refs[0]
    part_refs = refs[1:1 + n_parts]
    w_refs = refs[1 + n_parts:1 + 2 * n_parts]
    o_ref = refs[1 + 2 * n_parts]
    acc = x_ref[...]
    for a_ref, w_ref in zip(part_refs, w_refs):
        acc = acc + _dot(a_ref[...], w_ref[...])
    o_ref[...] = acc


def _out_proj(x2, parts, w, *, tm):
    t, d = x2.shape
    n_parts = len(parts)
    kp = w.shape[0] // n_parts
    in_specs = [pl.BlockSpec((tm, d), lambda i: (i, 0))]
    in_specs += [pl.BlockSpec((tm, kp), lambda i: (i, 0)) for _ in parts]
    in_specs += [pl.BlockSpec((kp, d), functools.partial(lambda i, c: (c, 0), c=c)) for c in range(n_parts)]
    return pl.pallas_call(
        functools.partial(_out_proj_kernel, n_parts=n_parts),
        out_shape=jax.ShapeDtypeStruct((t, d), F32),
        grid=(t // tm,),
        in_specs=in_specs,
        out_specs=pl.BlockSpec((tm, d), lambda i: (i, 0)),
        compiler_params=_cparams(1),
        name="out_proj",
    )(x2, *parts, *([w] * n_parts))


def _ffn_kernel(x_ref, g_ref, wg_ref, wu_ref, wd_ref, fg_ref, o_ref, *, final_norm):
    x = x_ref[...]
    h = _rms(x, g_ref[...]).astype(BF16)
    gate = _dot(h, wg_ref[...])
    up = _dot(h, wu_ref[...])
    act = (gate / (1.0 + jnp.exp(-gate)) * up).astype(BF16)
    y = x + _dot(act, wd_ref[...])
    if final_norm:
        y = _rms(y, fg_ref[...])
    o_ref[...] = y


def _ffn(x2, g, wg, wu, wd, fg, *, tm, final_norm):
    t, d = x2.shape
    hid = wg.shape[1]
    resident = dict(pipeline_mode=pl.Buffered(1))
    return pl.pallas_call(
        functools.partial(_ffn_kernel, final_norm=final_norm),
        out_shape=jax.ShapeDtypeStruct((t, d), F32),
        grid=(t // tm,),
        in_specs=[pl.BlockSpec((tm, d), lambda i: (i, 0)),
                  pl.BlockSpec((1, d), lambda i: (0, 0)),
                  pl.BlockSpec((d, hid), lambda i: (0, 0), **resident),
                  pl.BlockSpec((d, hid), lambda i: (0, 0), **resident),
                  pl.BlockSpec((hid, d), lambda i: (0, 0), **resident),
                  pl.BlockSpec((1, d), lambda i: (0, 0))],
        out_specs=pl.BlockSpec((tm, d), lambda i: (i, 0)),
        compiler_params=_cparams(1),
        name="ffn_final" if final_norm else "ffn",
    )(x2, g, wg, wu, wd, fg)


def _mla_proj_kernel(x_ref, pos_ref, freq_ref, g_ref, wd_ref, qg_ref, wqa_ref, wqb_ref,
                     kg_ref, wk_ref, wv_ref, q_ref, k_ref, v_ref):
    tm = x_ref.shape[0]
    h = _rms(x_ref[...], g_ref[...]).astype(BF16)
    down = _dot(h, wd_ref[...])
    c_q = down[:, :MLA_Q_RANK]
    c_kv = down[:, MLA_Q_RANK:MLA_Q_RANK + MLA_KV_RANK]
    kr = down[:, MLA_Q_RANK + MLA_KV_RANK:MLA_Q_RANK + MLA_KV_RANK + LANES]
    kr_rot = down[:, MLA_Q_RANK + MLA_KV_RANK + LANES:]

    lane = lax.broadcasted_iota(jnp.int32, (tm, LANES), 1)
    is_rope = (lane >= MLA_NOPE_DIM) & (lane < MLA_NOPE_DIM + MLA_ROPE_DIM)
    ang = pos_ref[...].astype(F32) * freq_ref[...]
    cos_p = jnp.where(is_rope, jnp.cos(ang), 1.0)
    sin_p = jnp.where(is_rope, jnp.sin(ang), 0.0)

    cqn = _rms(c_q, qg_ref[...]).astype(BF16)
    ckvn = _rms(c_kv, kg_ref[...]).astype(BF16)
    k_rope = kr * cos_p + kr_rot * sin_p
    qa = _dot(cqn, wqa_ref[...])
    qb = _dot(cqn, wqb_ref[...])
    kn = _dot(ckvn, wk_ref[...])
    for hd in range(MLA_HEADS):
        sl = slice(hd * LANES, (hd + 1) * LANES)
        q_ref[:, sl] = (qa[:, sl] * cos_p + qb[:, sl] * sin_p).astype(q_ref.dtype)
        k_ref[:, sl] = (kn[:, sl] + k_rope).astype(k_ref.dtype)
    v_ref[...] = _dot(ckvn, wv_ref[...]).astype(v_ref.dtype)


def _mla_proj(x2, pos2, freq, g, wd, qg, wqa, wqb, kg, wk, wv, *, tm):
    t, d = x2.shape
    full = lambda a: pl.BlockSpec(a.shape, lambda i: (0,) * a.ndim)
    qk_w = MLA_HEADS * LANES
    v_w = MLA_HEADS * MLA_V_DIM
    return pl.pallas_call(
        _mla_proj_kernel,
        out_shape=(jax.ShapeDtypeStruct((t, qk_w), BF16),
                   jax.ShapeDtypeStruct((t, qk_w), BF16),
                   jax.ShapeDtypeStruct((t, v_w), BF16)),
        grid=(t // tm,),
        in_specs=[pl.BlockSpec((tm, d), lambda i: (i, 0)),
                  pl.BlockSpec((tm, 1), lambda i: (i, 0)),
                  full(freq), full(g), full(wd), full(qg), full(wqa), full(wqb),
                  full(kg), full(wk), full(wv)],
        out_specs=(pl.BlockSpec((tm, qk_w), lambda i: (i, 0)),
                   pl.BlockSpec((tm, qk_w), lambda i: (i, 0)),
                   pl.BlockSpec((tm, v_w), lambda i: (i, 0))),
        compiler_params=_cparams(1),
        name="mla_proj",
    )(x2, pos2, freq, g, wd, qg, wqa, wqb, kg, wk, wv)


def _mla_attn_kernel(q_ref, k_ref, v_ref, o_ref, va_ref, vb_ref, m_ref, acc_ref, *, tq, tk, unroll):
    qi = pl.program_id(2)
    lane_s = lax.broadcasted_iota(jnp.int32, va_ref.shape, 1)

    @pl.when(qi == 0)
    def _():
        v_all = v_ref[0]
        one = jnp.ones_like(v_all)
        va_ref[...] = jnp.where(lane_s < MLA_V_DIM, v_all, one)
        vb_ref[...] = jnp.where(lane_s < MLA_V_DIM, one, v_all)

    r = tq // tk
    c2 = (MLA_NOPE_DIM + MLA_ROPE_DIM) ** -0.5 * LOG2E
    vaug = (va_ref, vb_ref)

    m_ref[...] = jnp.full(m_ref.shape, -jnp.inf, F32)
    acc_ref[...] = jnp.zeros(acc_ref.shape, F32)

    def tile(hh, r0, nrows, j, masked):
        rows = pl.ds(r0, nrows)
        q = q_ref[0, rows, hh * LANES:(hh + 1) * LANES]
        start = pl.multiple_of(j * tk, tk)
        kblk = k_ref[0, pl.ds(start, tk), hh * LANES:(hh + 1) * LANES]
        vblk = vaug[hh][pl.ds(start, tk), :]
        s2 = _dot_nt(q, kblk) * c2
        if masked:
            row = lax.broadcasted_iota(jnp.int32, (nrows, tk), 0)
            col = lax.broadcasted_iota(jnp.int32, (nrows, tk), 1)
            s2 = jnp.where(col <= row, s2, -jnp.inf)
        m = m_ref[hh, rows, :]
        m_new = jnp.maximum(m, jnp.max(s2, axis=-1, keepdims=True))
        alpha = jnp.exp2(m - m_new)
        p = jnp.exp2(s2 - jnp.tile(m_new, (1, tk // LANES)))
        acc_ref[hh, rows, :] = alpha * acc_ref[hh, rows, :] + _dot(p.astype(BF16), vblk)
        m_ref[hh, rows, :] = m_new

    def body(jj, carry):
        for t in range(unroll):
            for hh in range(2):
                tile(hh, 0, tq, jj * unroll + t, False)
        return carry

    lax.fori_loop(0, (qi * r) // unroll, body, 0)
    for u in range(r):
        for hh in range(2):
            tile(hh, u * tk, tq - u * tk, qi * r + u, True)

    lane = lax.broadcasted_iota(jnp.int32, (tq, LANES), 1)
    acc_a = acc_ref[0]
    acc_b = acc_ref[1]
    out_a = acc_a / pltpu.roll(acc_a, MLA_V_DIM, 1)
    out_b = acc_b / pltpu.roll(acc_b, MLA_V_DIM, 1)
    o_ref[0] = jnp.where(lane < MLA_V_DIM, out_a, out_b).astype(o_ref.dtype)


def _mla_attention(q3, k3, v3, *, tq, tk, unroll):
    b, s, _ = q3.shape
    assert (tq // tk) % unroll == 0
    n_pairs = MLA_HEADS // 2
    return pl.pallas_call(
        functools.partial(_mla_attn_kernel, tq=tq, tk=tk, unroll=unroll),
        out_shape=jax.ShapeDtypeStruct((b, s, MLA_HEADS * MLA_V_DIM), BF16),
        grid=(b, n_pairs, s // tq),
        in_specs=[pl.BlockSpec((1, tq, 2 * LANES), lambda bi, p, i: (bi, i, p)),
                  pl.BlockSpec((1, s, 2 * LANES), lambda bi, p, i: (bi, 0, p)),
                  pl.BlockSpec((1, s, LANES), lambda bi, p, i: (bi, 0, p))],
        out_specs=pl.BlockSpec((1, tq, LANES), lambda bi, p, i: (bi, i, p)),
        scratch_shapes=[pltpu.VMEM((s, LANES), BF16), pltpu.VMEM((s, LANES), BF16),
                        pltpu.VMEM((2, tq, LANES), F32), pltpu.VMEM((2, tq, LANES), F32)],
        compiler_params=_cparams(3),
        name="mla_attention",
    )(q3, k3, v3)


def _rotate_half_cols(w):
    half = MLA_ROPE_DIM // 2
    return jnp.concatenate([-w[..., half:], w[..., :half]], axis=-1)


def _mla_weight_layout(w_down, w_uq, w_ukv):
    d = w_down.shape[0]
    w_kr = w_down[:, MLA_Q_RANK + MLA_KV_RANK:]
    pad_lo = jnp.zeros((d, MLA_NOPE_DIM), F32)
    pad_hi = jnp.zeros((d, LANES - MLA_NOPE_DIM - MLA_ROPE_DIM), F32)
    wd = jnp.concatenate([w_down[:, :MLA_Q_RANK + MLA_KV_RANK],
                          pad_lo, w_kr, pad_hi,
                          pad_lo, _rotate_half_cols(w_kr), pad_hi], axis=1)

    wq = w_uq.reshape(MLA_Q_RANK, MLA_HEADS, MLA_NOPE_DIM + MLA_ROPE_DIM)
    wq_nope, wq_rope = wq[..., :MLA_NOPE_DIM], wq[..., MLA_NOPE_DIM:]
    zq = jnp.zeros((MLA_Q_RANK, MLA_HEADS, LANES - MLA_NOPE_DIM - MLA_ROPE_DIM), F32)
    wqa = jnp.concatenate([wq_nope, wq_rope, zq], axis=-1).reshape(MLA_Q_RANK, MLA_HEADS * LANES)
    wqb = jnp.concatenate([jnp.zeros_like(wq_nope), _rotate_half_cols(wq_rope), zq],
                          axis=-1).reshape(MLA_Q_RANK, MLA_HEADS * LANES)

    wkv = w_ukv.reshape(MLA_KV_RANK, MLA_HEADS, MLA_NOPE_DIM + MLA_V_DIM)
    wk_nope, wv = wkv[..., :MLA_NOPE_DIM], wkv[..., MLA_NOPE_DIM:]
    wk = jnp.concatenate([wk_nope, jnp.zeros((MLA_KV_RANK, MLA_HEADS, LANES - MLA_NOPE_DIM), F32)],
                         axis=-1).reshape(MLA_KV_RANK, MLA_HEADS * LANES)
    wv = wv.reshape(MLA_KV_RANK, MLA_HEADS * MLA_V_DIM)
    return wd.astype(BF16), wqa.astype(BF16), wqb.astype(BF16), wk.astype(BF16), wv.astype(BF16)


def _rope_lane_freqs():
    half = MLA_ROPE_DIM // 2
    freqs = ROPE_THETA ** (-jnp.arange(half, dtype=F32) / half)
    lane = np.arange(LANES)
    idx = np.clip(lane - MLA_NOPE_DIM, 0, MLA_ROPE_DIM - 1) % half
    return freqs[idx][None, :]


def kernel(x, positions, attn_norm, ffn_norm, even_w_in, even_sinks, even_w_out, rel_bias_table, mla_w_down, mla_q_norm, mla_w_uq, mla_kv_norm, mla_w_ukv, mla_w_o, ffn_w_gate, ffn_w_up, ffn_w_down, final_norm):
    b, s, d = x.shape
    t = b * s
    x2 = x.reshape(t, d)
    row = lambda v: v.reshape(1, -1)

    proj = _norm_proj(x2, row(attn_norm[0]), even_w_in[0].astype(BF16), tm=1024)
    proj3 = proj.reshape(b, s, EVEN_IN_WIDTH)
    o_a = _sb_attention(proj3, tq=1024, sub=256, unroll=1)
    o_b = _swa_attention(proj3, rel_bias_table, even_sinks[0].reshape(1, SW_HEADS))
    x2 = _out_proj(x2, [o_a.reshape(t, SB_WIDTH), o_b.reshape(t, SW_Q_WIDTH)],
                   even_w_out[0].astype(BF16), tm=1024)
    x2 = _ffn(x2, row(ffn_norm[0]), ffn_w_gate[0].astype(BF16), ffn_w_up[0].astype(BF16),
              ffn_w_down[0].astype(BF16), row(final_norm), tm=512, final_norm=False)

    wd, wqa, wqb, wk, wv = _mla_weight_layout(mla_w_down[0], mla_w_uq[0], mla_w_ukv[0])
    q, k, v = _mla_proj(x2, positions.reshape(t, 1), _rope_lane_freqs(), row(attn_norm[1]), wd,
                        row(mla_q_norm[0]), wqa, wqb, row(mla_kv_norm[0]), wk, wv, tm=512)
    o = _mla_attention(q.reshape(b, s, -1), k.reshape(b, s, -1), v.reshape(b, s, -1), tq=1024, tk=512, unroll=2)
    x2 = _out_proj(x2, [o.reshape(t, MLA_HEADS * MLA_V_DIM)], mla_w_o[0].astype(BF16), tm=1024)
    x2 = _ffn(x2, row(ffn_norm[1]), ffn_w_gate[1].astype(BF16), ffn_w_up[1].astype(BF16),
              ffn_w_down[1].astype(BF16), row(final_norm), tm=512, final_norm=True)
    return x2.reshape(b, s, d)
```

```python
import functools
import math

import numpy as np
import jax
import jax.numpy as jnp
from jax import lax
from jax.experimental import pallas as pl
from jax.experimental.pallas import tpu as pltpu

F32 = jnp.float32
BF16 = jnp.bfloat16

D_MODEL = 1024
EPS = 1e-6
HEAD_DIM = 64
LANES = 128
SB_HEADS = 8
SW_HEADS = 8
SW_KV_HEADS = 2
WINDOW = 128
SB_WIDTH = SB_HEADS * HEAD_DIM
SW_Q_WIDTH = SW_HEADS * HEAD_DIM
SW_KV_WIDTH = SW_KV_HEADS * HEAD_DIM
EVEN_IN_WIDTH = 3 * SB_WIDTH + SW_Q_WIDTH + 2 * SW_KV_WIDTH
REL_BUCKETS = 32
REL_MAX_DIST = 128
MLA_HEADS = 16
MLA_NOPE_DIM = 64
MLA_ROPE_DIM = 32
MLA_V_DIM = 64
MLA_Q_RANK = 384
MLA_KV_RANK = 256
ROPE_THETA = 10000.0
LOG2E = math.log2(math.e)
SB_SATURATED = 128.0
VMEM_LIMIT_BYTES = 56 * 1024 * 1024


def _cparams(n_axes):
    return pltpu.CompilerParams(
        dimension_semantics=("arbitrary",) * n_axes,
        vmem_limit_bytes=VMEM_LIMIT_BYTES)


def _rms(x, g):
    return x * lax.rsqrt(jnp.mean(x * x, axis=-1, keepdims=True) + EPS) * g


def _dot(a, b):
    return jnp.dot(a, b, preferred_element_type=F32)


def _dot_nt(a, b):
    return lax.dot_general(a, b, (((1,), (1,)), ((), ())), preferred_element_type=F32)


def _norm_proj_kernel(x_ref, g_ref, w_ref, o_ref):
    h = _rms(x_ref[...], g_ref[...]).astype(BF16)
    o_ref[...] = _dot(h, w_ref[...]).astype(o_ref.dtype)


def _norm_proj(x2, g, w, *, tm):
    t, d = x2.shape
    n = w.shape[1]
    return pl.pallas_call(
        _norm_proj_kernel,
        out_shape=jax.ShapeDtypeStruct((t, n), BF16),
        grid=(t // tm,),
        in_specs=[pl.BlockSpec((tm, d), lambda i: (i, 0)),
                  pl.BlockSpec((1, d), lambda i: (0, 0)),
                  pl.BlockSpec((d, n), lambda i: (0, 0))],
        out_specs=pl.BlockSpec((tm, n), lambda i: (i, 0)),
        compiler_params=_cparams(1),
        name="norm_proj",
    )(x2, g, w)


def _softplus(z):
    neg_abs = lax.bitcast_convert_type(
        lax.bitcast_convert_type(z, jnp.uint32) | jnp.uint32(0x80000000), F32)
    return jnp.maximum(z, 0.0) + jnp.log(1.0 + jnp.exp(neg_abs))


def _sb_kernel(q_ref, k_ref, v_ref, o_ref, r_ref, acc_ref, *, tq, sub, unroll):
    qi = pl.program_id(2)
    r = tq // sub
    q2 = q_ref[0]
    lane = lax.broadcasted_iota(jnp.int32, (tq, LANES), 1)
    low = lane < HEAD_DIM
    zero = jnp.zeros_like(q2)
    scale = HEAD_DIM ** -0.5
    qm = (jnp.where(low, q2, zero) * scale, jnp.where(low, zero, q2) * scale)

    tri_row = lax.broadcasted_iota(jnp.int32, (sub, sub), 0)
    tri_col = lax.broadcasted_iota(jnp.int32, (sub, sub), 1)
    tri = (tri_row >= tri_col).astype(BF16)

    r_ref[...] = jnp.zeros(r_ref.shape, F32)
    acc_ref[...] = jnp.zeros(acc_ref.shape, F32)

    def tile(hh, r0, nrows, j, masked):
        rows = pl.ds(r0, nrows)
        start = pl.multiple_of(j * sub, sub)
        kblk = k_ref[0, pl.ds(start, sub), :]
        vblk = v_ref[0, pl.ds(start, sub), :]
        z = _dot_nt(qm[hh][r0:r0 + nrows], kblk)
        sp = _softplus(z)
        if masked:
            row = lax.broadcasted_iota(jnp.int32, (nrows, sub), 0)
            col = lax.broadcasted_iota(jnp.int32, (nrows, sub), 1)
            causal = col < row
            sp = jnp.where(causal, sp, 0.0)
        later = r_ref[hh, rows, :]
        c = _dot(sp.astype(BF16), tri) + jnp.tile(later, (1, sub // LANES))
        w = jnp.exp(z - c)
        if masked:
            w = jnp.where(causal, w, 0.0)
        acc_ref[hh, rows, :] += _dot(w.astype(BF16), vblk)
        r_ref[hh, rows, :] = jnp.broadcast_to(c[:, 0:1], (nrows, LANES))

    for u in reversed(range(r)):
        for hh in range(2):
            tile(hh, u * sub, tq - u * sub, qi * r + u, True)

    def alive():
        return jnp.min(r_ref[...]) < SB_SATURATED

    n_iter = (qi * r) // unroll

    def cond(state):
        jj, a = state
        return jnp.logical_and(jj < n_iter, a)

    def body(state):
        jj, _ = state
        for t in range(unroll):
            for hh in range(2):
                tile(hh, 0, tq, qi * r - 1 - (jj * unroll + t), False)
        return jj + 1, alive()

    lax.while_loop(cond, body, (jnp.int32(0), alive()))
    o_ref[0] = jnp.where(low, acc_ref[0], acc_ref[1]).astype(o_ref.dtype)


def _sb_attention(proj3, *, tq, sub, unroll):
    b, s, _ = proj3.shape
    assert (tq // sub) % unroll == 0
    n_pairs = SB_WIDTH // LANES
    k_off = SB_WIDTH // LANES
    v_off = 2 * SB_WIDTH // LANES
    return pl.pallas_call(
        functools.partial(_sb_kernel, tq=tq, sub=sub, unroll=unroll),
        out_shape=jax.ShapeDtypeStruct((b, s, SB_WIDTH), BF16),
        grid=(b, n_pairs, s // tq),
        in_specs=[pl.BlockSpec((1, tq, LANES), lambda bi, p, i: (bi, i, p)),
                  pl.BlockSpec((1, s, LANES), lambda bi, p, i: (bi, 0, k_off + p)),
                  pl.BlockSpec((1, s, LANES), lambda bi, p, i: (bi, 0, v_off + p))],
        out_specs=pl.BlockSpec((1, tq, LANES), lambda bi, p, i: (bi, i, p)),
        scratch_shapes=[pltpu.VMEM((2, tq, LANES), F32), pltpu.VMEM((2, tq, LANES), F32)],
        compiler_params=_cparams(3),
        name="sb_attention",
    )(proj3, proj3, proj3)


def _t5_bucket_table():
    max_exact = REL_BUCKETS // 2
    rel = WINDOW + np.arange(WINDOW)[:, None] - np.arange(2 * WINDOW)[None, :]
    rel = np.maximum(rel, 0)
    relf = np.maximum(rel, 1).astype(np.float32)
    large = max_exact + (np.log(relf / np.float32(max_exact)) / np.float32(math.log(REL_MAX_DIST / max_exact))
                         * np.float32(REL_BUCKETS - max_exact)).astype(np.int32)
    large = np.minimum(large, REL_BUCKETS - 1)
    return np.where(rel < max_exact, rel, large).astype(np.int32)


def _swap_halves(x):
    return jnp.concatenate([x[:, HEAD_DIM:], x[:, :HEAD_DIM]], axis=1)


def _swa_kernel(table_ref, sinks_ref, bucket_ref, q_ref, kp_ref, kc_ref, vp_ref, vc_ref,
                o_ref, bias_ref, *, qb):
    bi = pl.program_id(0)
    n = pl.program_id(1)
    blk = WINDOW
    t = lax.broadcasted_iota(jnp.int32, (blk, 2 * blk), 0)
    s = lax.broadcasted_iota(jnp.int32, (blk, 2 * blk), 1)

    @pl.when(jnp.logical_and(bi == 0, n == 0))
    def _():
        bucket = bucket_ref[...]
        rel = blk + t - s
        in_window = (rel >= 0) & (rel < WINDOW)
        for h in range(SW_HEADS):
            bias = jnp.zeros((blk, 2 * blk), F32)
            for bk in range(REL_BUCKETS):
                bias = jnp.where(bucket == bk, table_ref[bk, h], bias)
            bias_ref[h] = jnp.where(in_window, bias, -jnp.inf)

    kk = jnp.concatenate([kp_ref[0], kc_ref[0]], axis=0)
    vv = jnp.concatenate([vp_ref[0], vc_ref[0]], axis=0)
    kk_sw = _swap_halves(kk)
    vv_sw = _swap_halves(vv)
    lane = lax.broadcasted_iota(jnp.int32, (blk, LANES), 1)
    low = lane < HEAD_DIM
    scale = HEAD_DIM ** -0.5
    group = SW_HEADS // SW_KV_HEADS
    has_prev = (s >= blk) | (n > 0)

    for j in range(qb):
        rows = slice(j * blk, (j + 1) * blk)
        keys = slice(j * blk, (j + 2) * blk)
        for p in range(SW_HEADS // 2):
            q2 = q_ref[0, rows, p * LANES:(p + 1) * LANES]
            zero = jnp.zeros_like(q2)
            outs = []
            for half in range(2):
                h = 2 * p + half
                kv = h // group
                qh = (jnp.where(low, q2, zero) if half == 0 else jnp.where(low, zero, q2)) * scale
                k_use = (kk if kv == half else kk_sw)[keys]
                v_use = (vv if kv == half else vv_sw)[keys]
                logits = _dot_nt(qh, k_use) + bias_ref[h]
                if j == 0:
                    logits = jnp.where(has_prev, logits, -jnp.inf)
                sink = sinks_ref[0, h]
                m = jnp.maximum(jnp.max(logits, axis=-1, keepdims=True), sink)
                pr = jnp.exp(logits - m)
                denom = jnp.sum(pr, axis=-1, keepdims=True) + jnp.exp(sink - m)
                outs.append(_dot(pr.astype(BF16), v_use) / denom)
            o_ref[0, rows, p * LANES:(p + 1) * LANES] = jnp.where(low, outs[0], outs[1]).astype(o_ref.dtype)


def _swa_attention(proj3, table, sinks, *, qb):
    b, s, _ = proj3.shape
    rows = qb * WINDOW
    q_blk = (3 * SB_WIDTH) // SW_Q_WIDTH
    k_blk = (3 * SB_WIDTH + SW_Q_WIDTH) // LANES
    v_blk = k_blk + 1
    bucket = jnp.asarray(_t5_bucket_table())
    smem = pl.BlockSpec(memory_space=pltpu.SMEM)
    prev = lambda c: (lambda bi, n: (bi, jnp.maximum(n * qb - 1, 0), c))
    cur = lambda c: (lambda bi, n: (bi, n, c))
    return pl.pallas_call(
        functools.partial(_swa_kernel, qb=qb),
        out_shape=jax.ShapeDtypeStruct((b, s, SW_Q_WIDTH), BF16),
        grid=(b, s // rows),
        in_specs=[smem, smem,
                  pl.BlockSpec((WINDOW, 2 * WINDOW), lambda bi, n: (0, 0)),
                  pl.BlockSpec((1, rows, SW_Q_WIDTH), cur(q_blk)),
                  pl.BlockSpec((1, WINDOW, LANES), prev(k_blk)),
                  pl.BlockSpec((1, rows, LANES), cur(k_blk)),
                  pl.BlockSpec((1, WINDOW, LANES), prev(v_blk)),
                  pl.BlockSpec((1, rows, LANES), cur(v_blk))],
        out_specs=pl.BlockSpec((1, rows, SW_Q_WIDTH), lambda bi, n: (bi, n, 0)),
        scratch_shapes=[pltpu.VMEM((SW_HEADS, WINDOW, 2 * WINDOW), F32)],
        compiler_params=_cparams(2),
        name="swa_attention",
    )(table, sinks, bucket, proj3, proj3, proj3, proj3, proj3)


def _out_proj_kernel(*refs, n_parts):
    x_ref = refs[0]
    part_refs = refs[1:1 + n_parts]
    w_refs = refs[1 + n_parts:1 + 2 * n_parts]
    o_ref = refs[1 + 2 * n_parts]
    acc = x_ref[...]
    for a_ref, w_ref in zip(part_refs, w_refs):
        acc = acc + _dot(a_ref[...], w_ref[...])
    o_ref[...] = acc


def _out_proj(x2, parts, w, *, tm):
    t, d = x2.shape
    n_parts = len(parts)
    kp = w.shape[0] // n_parts
    in_specs = [pl.BlockSpec((tm, d), lambda i: (i, 0))]
    in_specs += [pl.BlockSpec((tm, kp), lambda i: (i, 0)) for _ in parts]
    in_specs += [pl.BlockSpec((kp, d), functools.partial(lambda i, c: (c, 0), c=c)) for c in range(n_parts)]
    return pl.pallas_call(
        functools.partial(_out_proj_kernel, n_parts=n_parts),
        out_shape=jax.ShapeDtypeStruct((t, d), F32),
        grid=(t // tm,),
        in_specs=in_specs,
        out_specs=pl.BlockSpec((tm, d), lambda i: (i, 0)),
        compiler_params=_cparams(1),
        name="out_proj",
    )(x2, *parts, *([w] * n_parts))


def _ffn_kernel(x_ref, g_ref, wg_ref, wu_ref, wd_ref, fg_ref, o_ref, *, final_norm):
    x = x_ref[...]
    h = _rms(x, g_ref[...]).astype(BF16)
    gate = _dot(h, wg_ref[...])
    up = _dot(h, wu_ref[...])
    act = (gate / (1.0 + jnp.exp(-gate)) * up).astype(BF16)
    y = x + _dot(act, wd_ref[...])
    if final_norm:
        y = _rms(y, fg_ref[...])
    o_ref[...] = y


def _ffn(x2, g, wg, wu, wd, fg, *, tm, final_norm):
    t, d = x2.shape
    hid = wg.shape[1]
    resident = dict(pipeline_mode=pl.Buffered(1))
    return pl.pallas_call(
        functools.partial(_ffn_kernel, final_norm=final_norm),
        out_shape=jax.ShapeDtypeStruct((t, d), F32),
        grid=(t // tm,),
        in_specs=[pl.BlockSpec((tm, d), lambda i: (i, 0)),
                  pl.BlockSpec((1, d), lambda i: (0, 0)),
                  pl.BlockSpec((d, hid), lambda i: (0, 0), **resident),
                  pl.BlockSpec((d, hid), lambda i: (0, 0), **resident),
                  pl.BlockSpec((hid, d), lambda i: (0, 0), **resident),
                  pl.BlockSpec((1, d), lambda i: (0, 0))],
        out_specs=pl.BlockSpec((tm, d), lambda i: (i, 0)),
        compiler_params=_cparams(1),
        name="ffn_final" if final_norm else "ffn",
    )(x2, g, wg, wu, wd, fg)


def _mla_proj_kernel(x_ref, pos_ref, freq_ref, g_ref, wd_ref, qg_ref, wqa_ref, wqb_ref,
                     kg_ref, wk_ref, wv_ref, q_ref, k_ref, v_ref):
    tm = x_ref.shape[0]
    h = _rms(x_ref[...], g_ref[...]).astype(BF16)
    down = _dot(h, wd_ref[...])
    c_q = down[:, :MLA_Q_RANK]
    c_kv = down[:, MLA_Q_RANK:MLA_Q_RANK + MLA_KV_RANK]
    kr = down[:, MLA_Q_RANK + MLA_KV_RANK:MLA_Q_RANK + MLA_KV_RANK + LANES]
    kr_rot = down[:, MLA_Q_RANK + MLA_KV_RANK + LANES:]

    half = MLA_ROPE_DIM // 2
    ang = pos_ref[...].astype(F32) * freq_ref[...]
    lane = lax.broadcasted_iota(jnp.int32, (tm, LANES), 1)
    first = (lane >= MLA_NOPE_DIM) & (lane < MLA_NOPE_DIM + half)
    second = (lane >= MLA_NOPE_DIM + half) & (lane < MLA_NOPE_DIM + MLA_ROPE_DIM)

    def spread(dense, fill):
        rows8 = jnp.broadcast_to(dense[:, None, :], (tm // 8, 8, LANES)).reshape(tm, LANES)
        y = pltpu.roll(rows8, 0, 1, stride=half, stride_axis=0)
        return jnp.where(first, y, jnp.where(second, pltpu.roll(y, half, 1), fill))

    cos_p = spread(jnp.cos(ang), 1.0)
    sin_p = spread(jnp.sin(ang), 0.0)

    cqn = _rms(c_q, qg_ref[...]).astype(BF16)
    ckvn = _rms(c_kv, kg_ref[...]).astype(BF16)
    k_rope = kr * cos_p + kr_rot * sin_p
    qa = _dot(cqn, wqa_ref[...])
    qb = _dot(cqn, wqb_ref[...])
    kn = _dot(ckvn, wk_ref[...])
    for hd in range(MLA_HEADS):
        sl = slice(hd * LANES, (hd + 1) * LANES)
        q_ref[:, sl] = (qa[:, sl] * cos_p + qb[:, sl] * sin_p).astype(q_ref.dtype)
        k_ref[:, sl] = (kn[:, sl] + k_rope).astype(k_ref.dtype)
    v_ref[...] = _dot(ckvn, wv_ref[...]).astype(v_ref.dtype)


def _mla_proj(x2, pos2, freq, g, wd, qg, wqa, wqb, kg, wk, wv, *, tm):
    t, d = x2.shape
    full = lambda a: pl.BlockSpec(a.shape, lambda i: (0,) * a.ndim)
    qk_w = MLA_HEADS * LANES
    v_w = MLA_HEADS * MLA_V_DIM
    return pl.pallas_call(
        _mla_proj_kernel,
        out_shape=(jax.ShapeDtypeStruct((t, qk_w), BF16),
                   jax.ShapeDtypeStruct((t, qk_w), BF16),
                   jax.ShapeDtypeStruct((t, v_w), BF16)),
        grid=(t // tm,),
        in_specs=[pl.BlockSpec((tm, d), lambda i: (i, 0)),
                  pl.BlockSpec((tm // 8, LANES), lambda i: (i, 0)),
                  full(freq), full(g), full(wd), full(qg), full(wqa), full(wqb),
                  full(kg), full(wk), full(wv)],
        out_specs=(pl.BlockSpec((tm, qk_w), lambda i: (i, 0)),
                   pl.BlockSpec((tm, qk_w), lambda i: (i, 0)),
                   pl.BlockSpec((tm, v_w), lambda i: (i, 0))),
        compiler_params=_cparams(1),
        name="mla_proj",
    )(x2, pos2, freq, g, wd, qg, wqa, wqb, kg, wk, wv)


def _mla_attn_kernel(q_ref, k_ref, v_ref, o_ref, va_ref, vb_ref, m_ref, acc_ref, *, tq, tk, unroll):
    qi = pl.program_id(2)
    lane_s = lax.broadcasted_iota(jnp.int32, va_ref.shape, 1)

    @pl.when(qi == 0)
    def _():
        v_all = v_ref[0]
        one = jnp.ones_like(v_all)
        va_ref[...] = jnp.where(lane_s < MLA_V_DIM, v_all, one)
        vb_ref[...] = jnp.where(lane_s < MLA_V_DIM, one, v_all)

    r = tq // tk
    c2 = (MLA_NOPE_DIM + MLA_ROPE_DIM) ** -0.5 * LOG2E
    vaug = (va_ref, vb_ref)

    m_ref[...] = jnp.full(m_ref.shape, -jnp.inf, F32)
    acc_ref[...] = jnp.zeros(acc_ref.shape, F32)

    def tile(hh, r0, nrows, j, masked):
        rows = pl.ds(r0, nrows)
        q = q_ref[0, rows, hh * LANES:(hh + 1) * LANES]
        start = pl.multiple_of(j * tk, tk)
        kblk = k_ref[0, pl.ds(start, tk), hh * LANES:(hh + 1) * LANES]
        vblk = vaug[hh][pl.ds(start, tk), :]
        s2 = _dot_nt(q, kblk) * c2
        if masked:
            row = lax.broadcasted_iota(jnp.int32, (nrows, tk), 0)
            col = lax.broadcasted_iota(jnp.int32, (nrows, tk), 1)
            s2 = jnp.where(col <= row, s2, -jnp.inf)
        m = m_ref[hh, rows, :]
        m_new = jnp.maximum(m, jnp.max(s2, axis=-1, keepdims=True))
        alpha = jnp.exp2(m - m_new)
        p = jnp.exp2(s2 - jnp.tile(m_new, (1, tk // LANES)))
        acc_ref[hh, rows, :] = alpha * acc_ref[hh, rows, :] + _dot(p.astype(BF16), vblk)
        m_ref[hh, rows, :] = m_new

    def body(jj, carry):
        for t in range(unroll):
            for hh in range(2):
                tile(hh, 0, tq, jj * unroll + t, False)
        return carry

    lax.fori_loop(0, (qi * r) // unroll, body, 0)
    for u in range(r):
        for hh in range(2):
            tile(hh, u * tk, tq - u * tk, qi * r + u, True)

    lane = lax.broadcasted_iota(jnp.int32, (tq, LANES), 1)
    acc_a = acc_ref[0]
    acc_b = acc_ref[1]
    out_a = acc_a / pltpu.roll(acc_a, MLA_V_DIM, 1)
    out_b = acc_b / pltpu.roll(acc_b, MLA_V_DIM, 1)
    o_ref[0] = jnp.where(lane < MLA_V_DIM, out_a, out_b).astype(o_ref.dtype)


def _mla_attention(q3, k3, v3, *, tq, tk, unroll):
    b, s, _ = q3.shape
    assert (tq // tk) % unroll == 0
    n_pairs = MLA_HEADS // 2
    return pl.pallas_call(
        functools.partial(_mla_attn_kernel, tq=tq, tk=tk, unroll=unroll),
        out_shape=jax.ShapeDtypeStruct((b, s, MLA_HEADS * MLA_V_DIM), BF16),
        grid=(b, n_pairs, s // tq),
        in_specs=[pl.BlockSpec((1, tq, 2 * LANES), lambda bi, p, i: (bi, i, p)),
                  pl.BlockSpec((1, s, 2 * LANES), lambda bi, p, i: (bi, 0, p)),
                  pl.BlockSpec((1, s, LANES), lambda bi, p, i: (bi, 0, p))],
        out_specs=pl.BlockSpec((1, tq, LANES), lambda bi, p, i: (bi, i, p)),
        scratch_shapes=[pltpu.VMEM((s, LANES), BF16), pltpu.VMEM((s, LANES), BF16),
                        pltpu.VMEM((2, tq, LANES), F32), pltpu.VMEM((2, tq, LANES), F32)],
        compiler_params=_cparams(3),
        name="mla_attention",
    )(q3, k3, v3)


def _rotate_half_cols(w):
    half = MLA_ROPE_DIM // 2
    return jnp.concatenate([-w[..., half:], w[..., :half]], axis=-1)


def _mla_weight_layout(w_down, w_uq, w_ukv):
    d = w_down.shape[0]
    w_kr = w_down[:, MLA_Q_RANK + MLA_KV_RANK:]
    pad_lo = jnp.zeros((d, MLA_NOPE_DIM), F32)
    pad_hi = jnp.zeros((d, LANES - MLA_NOPE_DIM - MLA_ROPE_DIM), F32)
    wd = jnp.concatenate([w_down[:, :MLA_Q_RANK + MLA_KV_RANK],
                          pad_lo, w_kr, pad_hi,
                          pad_lo, _rotate_half_cols(w_kr), pad_hi], axis=1)

    wq = w_uq.reshape(MLA_Q_RANK, MLA_HEADS, MLA_NOPE_DIM + MLA_ROPE_DIM)
    wq_nope, wq_rope = wq[..., :MLA_NOPE_DIM], wq[..., MLA_NOPE_DIM:]
    zq = jnp.zeros((MLA_Q_RANK, MLA_HEADS, LANES - MLA_NOPE_DIM - MLA_ROPE_DIM), F32)
    wqa = jnp.concatenate([wq_nope, wq_rope, zq], axis=-1).reshape(MLA_Q_RANK, MLA_HEADS * LANES)
    wqb = jnp.concatenate([jnp.zeros_like(wq_nope), _rotate_half_cols(wq_rope), zq],
                          axis=-1).reshape(MLA_Q_RANK, MLA_HEADS * LANES)

    wkv = w_ukv.reshape(MLA_KV_RANK, MLA_HEADS, MLA_NOPE_DIM + MLA_V_DIM)
    wk_nope, wv = wkv[..., :MLA_NOPE_DIM], wkv[..., MLA_NOPE_DIM:]
    wk = jnp.concatenate([wk_nope, jnp.zeros((MLA_KV_RANK, MLA_HEADS, LANES - MLA_NOPE_DIM), F32)],
                         axis=-1).reshape(MLA_KV_RANK, MLA_HEADS * LANES)
    wv = wv.reshape(MLA_KV_RANK, MLA_HEADS * MLA_V_DIM)
    return wd.astype(BF16), wqa.astype(BF16), wqb.astype(BF16), wk.astype(BF16), wv.astype(BF16)


def _rope_dense_layout(positions):
    half = MLA_ROPE_DIM // 2
    n_blk = LANES // half
    freqs = ROPE_THETA ** (-jnp.arange(half, dtype=F32) / half)
    order = np.array([(MLA_NOPE_DIM // half - b) % n_blk for b in range(n_blk)])
    pos_dense = jnp.repeat(positions.reshape(-1, n_blk)[:, order], half, axis=1)
    return pos_dense, jnp.tile(freqs, n_blk)[None, :]


def kernel(x, positions, attn_norm, ffn_norm, even_w_in, even_sinks, even_w_out, rel_bias_table, mla_w_down, mla_q_norm, mla_w_uq, mla_kv_norm, mla_w_ukv, mla_w_o, ffn_w_gate, ffn_w_up, ffn_w_down, final_norm):
    b, s, d = x.shape
    t = b * s
    x2 = x.reshape(t, d)
    row = lambda v: v.reshape(1, -1)

    proj = _norm_proj(x2, row(attn_norm[0]), even_w_in[0].astype(BF16), tm=1024)
    proj3 = proj.reshape(b, s, EVEN_IN_WIDTH)
    o_a = _sb_attention(proj3, tq=1024, sub=256, unroll=1)
    o_b = _swa_attention(proj3, rel_bias_table, even_sinks[0].reshape(1, SW_HEADS), qb=4)
    x2 = _out_proj(x2, [o_a.reshape(t, SB_WIDTH), o_b.reshape(t, SW_Q_WIDTH)],
                   even_w_out[0].astype(BF16), tm=1024)
    x2 = _ffn(x2, row(ffn_norm[0]), ffn_w_gate[0].astype(BF16), ffn_w_up[0].astype(BF16),
              ffn_w_down[0].astype(BF16), row(final_norm), tm=512, final_norm=False)

    wd, wqa, wqb, wk, wv = _mla_weight_layout(mla_w_down[0], mla_w_uq[0], mla_w_ukv[0])
    pos_dense, freq_dense = _rope_dense_layout(positions)
    q, k, v = _mla_proj(x2, pos_dense, freq_dense, row(attn_norm[1]), wd,
                        row(mla_q_norm[0]), wqa, wqb, row(mla_kv_norm[0]), wk, wv, tm=512)
    o = _mla_attention(q.reshape(b, s, -1), k.reshape(b, s, -1), v.reshape(b, s, -1), tq=1024, tk=512, unroll=2)
    x2 = _out_proj(x2, [o.reshape(t, MLA_HEADS * MLA_V_DIM)], mla_w_o[0].astype(BF16), tm=1024)
    x2 = _ffn(x2, row(ffn_norm[1]), ffn_w_gate[1].astype(BF16), ffn_w_up[1].astype(BF16),
              ffn_w_down[1].astype(BF16), row(final_norm), tm=512, final_norm=True)
    return x2.reshape(b, s, d)
```

```python
import functools
import math

import numpy as np
import jax
import jax.numpy as jnp
from jax import lax
from jax.experimental import pallas as pl
from jax.experimental.pallas import tpu as pltpu

F32 = jnp.float32
BF16 = jnp.bfloat16

D_MODEL = 1024
EPS = 1e-6
HEAD_DIM = 64
LANES = 128
SB_HEADS = 8
SW_HEADS = 8
SW_KV_HEADS = 2
WINDOW = 128
SB_WIDTH = SB_HEADS * HEAD_DIM
SW_Q_WIDTH = SW_HEADS * HEAD_DIM
SW_KV_WIDTH = SW_KV_HEADS * HEAD_DIM
EVEN_IN_WIDTH = 3 * SB_WIDTH + SW_Q_WIDTH + 2 * SW_KV_WIDTH
REL_BUCKETS = 32
REL_MAX_DIST = 128
MLA_HEADS = 16
MLA_NOPE_DIM = 64
MLA_ROPE_DIM = 32
MLA_V_DIM = 64
MLA_Q_RANK = 384
MLA_KV_RANK = 256
ROPE_THETA = 10000.0
LOG2E = math.log2(math.e)
SB_SATURATED = 128.0
VMEM_LIMIT_BYTES = 56 * 1024 * 1024


def _cparams(n_axes):
    return pltpu.CompilerParams(
        dimension_semantics=("arbitrary",) * n_axes,
        vmem_limit_bytes=VMEM_LIMIT_BYTES)


def _rms(x, g):
    return x * lax.rsqrt(jnp.mean(x * x, axis=-1, keepdims=True) + EPS) * g


def _dot(a, b):
    return jnp.dot(a, b, preferred_element_type=F32)


def _dot_nt(a, b):
    return lax.dot_general(a, b, (((1,), (1,)), ((), ())), preferred_element_type=F32)


def _norm_proj_kernel(x_ref, g_ref, w_ref, o_ref):
    h = _rms(x_ref[...], g_ref[...]).astype(BF16)
    o_ref[...] = _dot(h, w_ref[...]).astype(o_ref.dtype)


def _norm_proj(x2, g, w, *, tm):
    t, d = x2.shape
    n = w.shape[1]
    return pl.pallas_call(
        _norm_proj_kernel,
        out_shape=jax.ShapeDtypeStruct((t, n), BF16),
        grid=(t // tm,),
        in_specs=[pl.BlockSpec((tm, d), lambda i: (i, 0)),
                  pl.BlockSpec((1, d), lambda i: (0, 0)),
                  pl.BlockSpec((d, n), lambda i: (0, 0))],
        out_specs=pl.BlockSpec((tm, n), lambda i: (i, 0)),
        compiler_params=_cparams(1),
        name="norm_proj",
    )(x2, g, w)


def _softplus(z):
    neg_abs = lax.bitcast_convert_type(
        lax.bitcast_convert_type(z, jnp.uint32) | jnp.uint32(0x80000000), F32)
    return jnp.maximum(z, 0.0) + jnp.log(1.0 + jnp.exp(neg_abs))


def _sb_kernel(q_ref, k_ref, v_ref, o_ref, r_ref, acc_ref, *, tq, sub, unroll):
    qi = pl.program_id(2)
    r = tq // sub
    q2 = q_ref[0]
    lane = lax.broadcasted_iota(jnp.int32, (tq, LANES), 1)
    low = lane < HEAD_DIM
    zero = jnp.zeros_like(q2)
    scale = HEAD_DIM ** -0.5
    qm = (jnp.where(low, q2, zero) * scale, jnp.where(low, zero, q2) * scale)

    tri_row = lax.broadcasted_iota(jnp.int32, (sub, sub), 0)
    tri_col = lax.broadcasted_iota(jnp.int32, (sub, sub), 1)
    tri = (tri_row >= tri_col).astype(BF16)

    r_ref[...] = jnp.zeros(r_ref.shape, F32)
    acc_ref[...] = jnp.zeros(acc_ref.shape, F32)

    def tile(hh, r0, nrows, j, masked):
        rows = pl.ds(r0, nrows)
        start = pl.multiple_of(j * sub, sub)
        kblk = k_ref[0, pl.ds(start, sub), :]
        vblk = v_ref[0, pl.ds(start, sub), :]
        z = _dot_nt(qm[hh][r0:r0 + nrows], kblk)
        sp = _softplus(z)
        if masked:
            row = lax.broadcasted_iota(jnp.int32, (nrows, sub), 0)
            col = lax.broadcasted_iota(jnp.int32, (nrows, sub), 1)
            causal = col < row
            sp = jnp.where(causal, sp, 0.0)
        later = r_ref[hh, rows, :]
        c = _dot(sp.astype(BF16), tri) + jnp.tile(later, (1, sub // LANES))
        w = jnp.exp(z - c)
        if masked:
            w = jnp.where(causal, w, 0.0)
        acc_ref[hh, rows, :] += _dot(w.astype(BF16), vblk)
        r_ref[hh, rows, :] = jnp.broadcast_to(c[:, 0:1], (nrows, LANES))

    for u in reversed(range(r)):
        for hh in range(2):
            tile(hh, u * sub, tq - u * sub, qi * r + u, True)

    def alive():
        return jnp.min(r_ref[...]) < SB_SATURATED

    n_iter = (qi * r) // unroll

    def cond(state):
        jj, a = state
        return jnp.logical_and(jj < n_iter, a)

    def body(state):
        jj, _ = state
        for t in range(unroll):
            for hh in range(2):
                tile(hh, 0, tq, qi * r - 1 - (jj * unroll + t), False)
        return jj + 1, alive()

    lax.while_loop(cond, body, (jnp.int32(0), alive()))
    o_ref[0] = jnp.where(low, acc_ref[0], acc_ref[1]).astype(o_ref.dtype)


def _sb_attention(proj3, *, tq, sub, unroll):
    b, s, _ = proj3.shape
    assert (tq // sub) % unroll == 0
    n_pairs = SB_WIDTH // LANES
    k_off = SB_WIDTH // LANES
    v_off = 2 * SB_WIDTH // LANES
    return pl.pallas_call(
        functools.partial(_sb_kernel, tq=tq, sub=sub, unroll=unroll),
        out_shape=jax.ShapeDtypeStruct((b, s, SB_WIDTH), BF16),
        grid=(b, n_pairs, s // tq),
        in_specs=[pl.BlockSpec((1, tq, LANES), lambda bi, p, i: (bi, i, p)),
                  pl.BlockSpec((1, s, LANES), lambda bi, p, i: (bi, 0, k_off + p)),
                  pl.BlockSpec((1, s, LANES), lambda bi, p, i: (bi, 0, v_off + p))],
        out_specs=pl.BlockSpec((1, tq, LANES), lambda bi, p, i: (bi, i, p)),
        scratch_shapes=[pltpu.VMEM((2, tq, LANES), F32), pltpu.VMEM((2, tq, LANES), F32)],
        compiler_params=_cparams(3),
        name="sb_attention",
    )(proj3, proj3, proj3)


def _t5_bucket_table():
    max_exact = REL_BUCKETS // 2
    rel = WINDOW + np.arange(WINDOW)[:, None] - np.arange(2 * WINDOW)[None, :]
    rel = np.maximum(rel, 0)
    relf = np.maximum(rel, 1).astype(np.float32)
    large = max_exact + (np.log(relf / np.float32(max_exact)) / np.float32(math.log(REL_MAX_DIST / max_exact))
                         * np.float32(REL_BUCKETS - max_exact)).astype(np.int32)
    large = np.minimum(large, REL_BUCKETS - 1)
    return np.where(rel < max_exact, rel, large).astype(np.int32)


def _swap_halves(x):
    return jnp.concatenate([x[:, HEAD_DIM:], x[:, :HEAD_DIM]], axis=1)


def _swa_kernel(table_ref, sinks_ref, bucket_ref, q_ref, kp_ref, kc_ref, vp_ref, vc_ref,
                o_ref, bias_ref, *, qb):
    bi = pl.program_id(0)
    n = pl.program_id(1)
    blk = WINDOW
    t = lax.broadcasted_iota(jnp.int32, (blk, 2 * blk), 0)
    s = lax.broadcasted_iota(jnp.int32, (blk, 2 * blk), 1)

    @pl.when(jnp.logical_and(bi == 0, n == 0))
    def _():
        bucket = bucket_ref[...]
        rel = blk + t - s
        in_window = (rel >= 0) & (rel < WINDOW)
        for h in range(SW_HEADS):
            bias = jnp.zeros((blk, 2 * blk), F32)
            for bk in range(REL_BUCKETS):
                bias = jnp.where(bucket == bk, table_ref[bk, h], bias)
            bias_ref[h] = jnp.where(in_window, bias, -jnp.inf)

    kk = jnp.concatenate([kp_ref[0], kc_ref[0]], axis=0)
    vv = jnp.concatenate([vp_ref[0], vc_ref[0]], axis=0)
    kk_sw = _swap_halves(kk)
    vv_sw = _swap_halves(vv)
    lane = lax.broadcasted_iota(jnp.int32, (blk, LANES), 1)
    low = lane < HEAD_DIM
    scale = HEAD_DIM ** -0.5
    group = SW_HEADS // SW_KV_HEADS
    has_prev = (s >= blk) | (n > 0)

    for j in range(qb):
        rows = slice(j * blk, (j + 1) * blk)
        keys = slice(j * blk, (j + 2) * blk)
        for p in range(SW_HEADS // 2):
            q2 = q_ref[0, rows, p * LANES:(p + 1) * LANES]
            zero = jnp.zeros_like(q2)
            outs = []
            for half in range(2):
                h = 2 * p + half
                kv = h // group
                qh = (jnp.where(low, q2, zero) if half == 0 else jnp.where(low, zero, q2)) * scale
                k_use = (kk if kv == half else kk_sw)[keys]
                v_use = (vv if kv == half else vv_sw)[keys]
                logits = _dot_nt(qh, k_use) + bias_ref[h]
                if j == 0:
                    logits = jnp.where(has_prev, logits, -jnp.inf)
                sink = sinks_ref[0, h]
                m = jnp.maximum(jnp.max(logits, axis=-1, keepdims=True), sink)
                pr = jnp.exp(logits - m)
                denom = jnp.sum(pr, axis=-1, keepdims=True) + jnp.exp(sink - m)
                outs.append(_dot(pr.astype(BF16), v_use) / denom)
            o_ref[0, rows, p * LANES:(p + 1) * LANES] = jnp.where(low, outs[0], outs[1]).astype(o_ref.dtype)


def _swa_attention(proj3, table, sinks, *, qb):
    b, s, _ = proj3.shape
    rows = qb * WINDOW
    q_blk = (3 * SB_WIDTH) // SW_Q_WIDTH
    k_blk = (3 * SB_WIDTH + SW_Q_WIDTH) // LANES
    v_blk = k_blk + 1
    bucket = jnp.asarray(_t5_bucket_table())
    smem = pl.BlockSpec(memory_space=pltpu.SMEM)
    prev = lambda c: (lambda bi, n: (bi, jnp.maximum(n * qb - 1, 0), c))
    cur = lambda c: (lambda bi, n: (bi, n, c))
    return pl.pallas_call(
        functools.partial(_swa_kernel, qb=qb),
        out_shape=jax.ShapeDtypeStruct((b, s, SW_Q_WIDTH), BF16),
        grid=(b, s // rows),
        in_specs=[smem, smem,
                  pl.BlockSpec((WINDOW, 2 * WINDOW), lambda bi, n: (0, 0)),
                  pl.BlockSpec((1, rows, SW_Q_WIDTH), cur(q_blk)),
                  pl.BlockSpec((1, WINDOW, LANES), prev(k_blk)),
                  pl.BlockSpec((1, rows, LANES), cur(k_blk)),
                  pl.BlockSpec((1, WINDOW, LANES), prev(v_blk)),
                  pl.BlockSpec((1, rows, LANES), cur(v_blk))],
        out_specs=pl.BlockSpec((1, rows, SW_Q_WIDTH), lambda bi, n: (bi, n, 0)),
        scratch_shapes=[pltpu.VMEM((SW_HEADS, WINDOW, 2 * WINDOW), F32)],
        compiler_params=_cparams(2),
        name="swa_attention",
    )(table, sinks, bucket, proj3, proj3, proj3, proj3, proj3)


def _to_bf16_kernel(w_ref, o_ref):
    o_ref[...] = w_ref[...].astype(BF16)


def _to_bf16(w, *, tr):
    n, r, c = w.shape
    spec = pl.BlockSpec((1, tr, c), lambda l, i: (l, i, 0))
    return pl.pallas_call(
        _to_bf16_kernel,
        out_shape=jax.ShapeDtypeStruct(w.shape, BF16),
        grid=(n, r // tr),
        in_specs=[spec],
        out_specs=spec,
        compiler_params=_cparams(2),
        name="to_bf16",
    )(w)


def _out_ffn_kernel(*refs, n_parts, final_norm):
    x_ref = refs[0]
    part_refs = refs[1:1 + n_parts]
    wo_refs = refs[1 + n_parts:1 + 2 * n_parts]
    g_ref, wg_ref, wu_ref, wd_ref, fg_ref, o_ref = refs[1 + 2 * n_parts:]
    x = x_ref[...]
    for a_ref, w_ref in zip(part_refs, wo_refs):
        x = x + _dot(a_ref[...], w_ref[...])
    h = _rms(x, g_ref[...]).astype(BF16)
    gate = _dot(h, wg_ref[...])
    up = _dot(h, wu_ref[...])
    act = (gate / (1.0 + jnp.exp(-gate)) * up).astype(BF16)
    y = x + _dot(act, wd_ref[...])
    if final_norm:
        y = _rms(y, fg_ref[...])
    o_ref[...] = y


def _out_ffn(x2, parts, w_out, g, wg, wu, wd, fg, *, layer, tm, final_norm):
    t, d = x2.shape
    hid = wg.shape[2]
    n_parts = len(parts)
    kp = w_out.shape[0] // n_parts
    resident = dict(pipeline_mode=pl.Buffered(1))
    row_tile = lambda width: pl.BlockSpec((tm, width), lambda i: (i, 0))
    const = lambda r, c: pl.BlockSpec((None, r, c), lambda i: (layer, 0, 0), **resident)
    in_specs = [row_tile(d)] + [row_tile(kp) for _ in parts]
    in_specs += [pl.BlockSpec((kp, d), functools.partial(lambda i, c: (c, 0), c=c), **resident)
                 for c in range(n_parts)]
    in_specs += [pl.BlockSpec((1, d), lambda i: (0, 0)), const(d, hid), const(d, hid),
                 const(hid, d), pl.BlockSpec((1, d), lambda i: (0, 0))]
    return pl.pallas_call(
        functools.partial(_out_ffn_kernel, n_parts=n_parts, final_norm=final_norm),
        out_shape=jax.ShapeDtypeStruct((t, d), F32),
        grid=(t // tm,),
        in_specs=in_specs,
        out_specs=row_tile(d),
        compiler_params=_cparams(1),
        name="out_ffn_final" if final_norm else "out_ffn",
    )(x2, *parts, *([w_out] * n_parts), g, wg, wu, wd, fg)


def _mla_proj_kernel(x_ref, pos_ref, freq_ref, g_ref, wd_ref, qg_ref, wq_ref,
                     kg_ref, wk_ref, wv_ref, q_ref, k_ref, v_ref):
    tm = x_ref.shape[0]
    h = _rms(x_ref[...], g_ref[...]).astype(BF16)
    down = _dot(h, wd_ref[...])
    c_q = down[:, :MLA_Q_RANK]
    c_kv = down[:, MLA_Q_RANK:MLA_Q_RANK + MLA_KV_RANK]
    kr = down[:, MLA_Q_RANK + MLA_KV_RANK:]

    half = MLA_ROPE_DIM // 2
    ang = pos_ref[...].astype(F32) * freq_ref[...]
    lane = lax.broadcasted_iota(jnp.int32, (tm, LANES), 1)
    first = lane < half
    second = (lane >= LANES // 2) & (lane < LANES // 2 + half)

    def spread(dense):
        rows8 = jnp.broadcast_to(dense[:, None, :], (tm // 8, 8, LANES)).reshape(tm, LANES)
        y = pltpu.roll(rows8, 0, 1, stride=half, stride_axis=0)
        return y, pltpu.roll(y, LANES // 2, 1)

    cos_1, cos_2 = spread(jnp.cos(ang))
    sin_1, sin_2 = spread(jnp.sin(ang))
    cos_p = jnp.where(first, cos_1, jnp.where(second, cos_2, 1.0))
    sin_p = jnp.where(first, -sin_1, jnp.where(second, sin_2, 0.0))

    def rope(blk):
        return blk * cos_p + pltpu.roll(blk, LANES // 2, 1) * sin_p

    cqn = _rms(c_q, qg_ref[...]).astype(BF16)
    ckvn = _rms(c_kv, kg_ref[...]).astype(BF16)
    k_rope = rope(kr)
    q = _dot(cqn, wq_ref[...])
    kn = _dot(ckvn, wk_ref[...])
    for hd in range(MLA_HEADS):
        sl = slice(hd * LANES, (hd + 1) * LANES)
        q_ref[:, sl] = rope(q[:, sl]).astype(q_ref.dtype)
        k_ref[:, sl] = (kn[:, sl] + k_rope).astype(k_ref.dtype)
    v_ref[...] = _dot(ckvn, wv_ref[...]).astype(v_ref.dtype)


def _mla_proj(x2, pos2, freq, g, wd, qg, wq, kg, wk, wv, *, tm):
    t, d = x2.shape
    full = lambda a: pl.BlockSpec(a.shape, lambda i: (0,) * a.ndim)
    qk_w = MLA_HEADS * LANES
    v_w = MLA_HEADS * MLA_V_DIM
    return pl.pallas_call(
        _mla_proj_kernel,
        out_shape=(jax.ShapeDtypeStruct((t, qk_w), BF16),
                   jax.ShapeDtypeStruct((t, qk_w), BF16),
                   jax.ShapeDtypeStruct((t, v_w), BF16)),
        grid=(t // tm,),
        in_specs=[pl.BlockSpec((tm, d), lambda i: (i, 0)),
                  pl.BlockSpec((tm // 8, LANES), lambda i: (i, 0)),
                  full(freq), full(g), full(wd), full(qg), full(wq),
                  full(kg), full(wk), full(wv)],
        out_specs=(pl.BlockSpec((tm, qk_w), lambda i: (i, 0)),
                   pl.BlockSpec((tm, qk_w), lambda i: (i, 0)),
                   pl.BlockSpec((tm, v_w), lambda i: (i, 0))),
        compiler_params=_cparams(1),
        name="mla_proj",
    )(x2, pos2, freq, g, wd, qg, wq, kg, wk, wv)


def _mla_attn_kernel(q_ref, k_ref, v_ref, o_ref, va_ref, vb_ref, m_ref, acc_ref, *, tq, tk, unroll):
    qi = pl.program_id(2)
    lane_s = lax.broadcasted_iota(jnp.int32, va_ref.shape, 1)

    @pl.when(qi == 0)
    def _():
        v_all = v_ref[0]
        one = jnp.ones_like(v_all)
        va_ref[...] = jnp.where(lane_s < MLA_V_DIM, v_all, one)
        vb_ref[...] = jnp.where(lane_s < MLA_V_DIM, one, v_all)

    r = tq // tk
    c2 = (MLA_NOPE_DIM + MLA_ROPE_DIM) ** -0.5 * LOG2E
    vaug = (va_ref, vb_ref)

    m_ref[...] = jnp.full(m_ref.shape, -jnp.inf, F32)
    acc_ref[...] = jnp.zeros(acc_ref.shape, F32)

    def tile(hh, r0, nrows, j, masked):
        rows = pl.ds(r0, nrows)
        q = q_ref[0, rows, hh * LANES:(hh + 1) * LANES]
        start = pl.multiple_of(j * tk, tk)
        kblk = k_ref[0, pl.ds(start, tk), hh * LANES:(hh + 1) * LANES]
        vblk = vaug[hh][pl.ds(start, tk), :]
        s2 = _dot_nt(q, kblk) * c2
        if masked:
            row = lax.broadcasted_iota(jnp.int32, (nrows, tk), 0)
            col = lax.broadcasted_iota(jnp.int32, (nrows, tk), 1)
            s2 = jnp.where(col <= row, s2, -jnp.inf)
        m = m_ref[hh, rows, :]
        m_new = jnp.maximum(m, jnp.max(s2, axis=-1, keepdims=True))
        alpha = jnp.exp2(m - m_new)
        p = jnp.exp2(s2 - jnp.tile(m_new, (1, tk // LANES)))
        acc_ref[hh, rows, :] = alpha * acc_ref[hh, rows, :] + _dot(p.astype(BF16), vblk)
        m_ref[hh, rows, :] = m_new

    def body(jj, carry):
        for t in range(unroll):
            for hh in range(2):
                tile(hh, 0, tq, jj * unroll + t, False)
        return carry

    lax.fori_loop(0, (qi * r) // unroll, body, 0)
    for u in range(r):
        for hh in range(2):
            tile(hh, u * tk, tq - u * tk, qi * r + u, True)

    lane = lax.broadcasted_iota(jnp.int32, (tq, LANES), 1)
    acc_a = acc_ref[0]
    acc_b = acc_ref[1]
    out_a = acc_a / pltpu.roll(acc_a, MLA_V_DIM, 1)
    out_b = acc_b / pltpu.roll(acc_b, MLA_V_DIM, 1)
    o_ref[0] = jnp.where(lane < MLA_V_DIM, out_a, out_b).astype(o_ref.dtype)


def _mla_attention(q3, k3, v3, *, tq, tk, unroll):
    b, s, _ = q3.shape
    assert (tq // tk) % unroll == 0
    n_pairs = MLA_HEADS // 2
    return pl.pallas_call(
        functools.partial(_mla_attn_kernel, tq=tq, tk=tk, unroll=unroll),
        out_shape=jax.ShapeDtypeStruct((b, s, MLA_HEADS * MLA_V_DIM), BF16),
        grid=(b, n_pairs, s // tq),
        in_specs=[pl.BlockSpec((1, tq, 2 * LANES), lambda bi, p, i: (bi, i, p)),
                  pl.BlockSpec((1, s, 2 * LANES), lambda bi, p, i: (bi, 0, p)),
                  pl.BlockSpec((1, s, LANES), lambda bi, p, i: (bi, 0, p))],
        out_specs=pl.BlockSpec((1, tq, LANES), lambda bi, p, i: (bi, i, p)),
        scratch_shapes=[pltpu.VMEM((s, LANES), BF16), pltpu.VMEM((s, LANES), BF16),
                        pltpu.VMEM((2, tq, LANES), F32), pltpu.VMEM((2, tq, LANES), F32)],
        compiler_params=_cparams(3),
        name="mla_attention",
    )(q3, k3, v3)


def _head_lanes(nope, rope):
    half = MLA_ROPE_DIM // 2
    split = LANES // 2 - half
    zero = jnp.zeros(nope.shape[:-1] + (LANES - MLA_NOPE_DIM - MLA_ROPE_DIM,), nope.dtype)
    return jnp.concatenate([rope[..., :half], nope[..., :split], rope[..., half:], nope[..., split:], zero],
                           axis=-1)


def _mla_weight_layout(w_down, w_uq, w_ukv):
    d = w_down.shape[0]
    w_kr = w_down[:, MLA_Q_RANK + MLA_KV_RANK:]
    wd = jnp.concatenate([w_down[:, :MLA_Q_RANK + MLA_KV_RANK],
                          _head_lanes(jnp.zeros((d, MLA_NOPE_DIM), F32), w_kr)], axis=1)

    wq = w_uq.reshape(MLA_Q_RANK, MLA_HEADS, MLA_NOPE_DIM + MLA_ROPE_DIM)
    wq = _head_lanes(wq[..., :MLA_NOPE_DIM], wq[..., MLA_NOPE_DIM:]).reshape(MLA_Q_RANK, MLA_HEADS * LANES)

    wkv = w_ukv.reshape(MLA_KV_RANK, MLA_HEADS, MLA_NOPE_DIM + MLA_V_DIM)
    wk_nope, wv = wkv[..., :MLA_NOPE_DIM], wkv[..., MLA_NOPE_DIM:]
    wk = _head_lanes(wk_nope, jnp.zeros((MLA_KV_RANK, MLA_HEADS, MLA_ROPE_DIM), F32))
    wk = wk.reshape(MLA_KV_RANK, MLA_HEADS * LANES)
    wv = wv.reshape(MLA_KV_RANK, MLA_HEADS * MLA_V_DIM)
    return wd.astype(BF16), wq.astype(BF16), wk.astype(BF16), wv.astype(BF16)


def _rope_dense_layout(positions):
    half = MLA_ROPE_DIM // 2
    n_blk = LANES // half
    freqs = ROPE_THETA ** (-jnp.arange(half, dtype=F32) / half)
    order = np.array([(-b) % n_blk for b in range(n_blk)])
    pos_dense = jnp.repeat(positions.reshape(-1, n_blk)[:, order], half, axis=1)
    return pos_dense, jnp.tile(freqs, n_blk)[None, :]


def kernel(x, positions, attn_norm, ffn_norm, even_w_in, even_sinks, even_w_out, rel_bias_table, mla_w_down, mla_q_norm, mla_w_uq, mla_kv_norm, mla_w_ukv, mla_w_o, ffn_w_gate, ffn_w_up, ffn_w_down, final_norm):
    b, s, d = x.shape
    t = b * s
    x2 = x.reshape(t, d)
    row = lambda v: v.reshape(1, -1)

    proj = _norm_proj(x2, row(attn_norm[0]), even_w_in[0].astype(BF16), tm=1024)
    proj3 = proj.reshape(b, s, EVEN_IN_WIDTH)
    o_a = _sb_attention(proj3, tq=1024, sub=256, unroll=1)
    o_b = _swa_attention(proj3, rel_bias_table, even_sinks[0].reshape(1, SW_HEADS), qb=4)
    wg, wu, wd_ffn = _to_bf16(ffn_w_gate, tr=256), _to_bf16(ffn_w_up, tr=256), _to_bf16(ffn_w_down, tr=704)
    x2 = _out_ffn(x2, [o_a.reshape(t, SB_WIDTH), o_b.reshape(t, SW_Q_WIDTH)], even_w_out[0].astype(BF16),
                  row(ffn_norm[0]), wg, wu, wd_ffn, row(final_norm), layer=0, tm=512, final_norm=False)

    wd, wq, wk, wv = _mla_weight_layout(mla_w_down[0], mla_w_uq[0], mla_w_ukv[0])
    pos_dense, freq_dense = _rope_dense_layout(positions)
    q, k, v = _mla_proj(x2, pos_dense, freq_dense, row(attn_norm[1]), wd,
                        row(mla_q_norm[0]), wq, row(mla_kv_norm[0]), wk, wv, tm=512)
    o = _mla_attention(q.reshape(b, s, -1), k.reshape(b, s, -1), v.reshape(b, s, -1), tq=1024, tk=512, unroll=2)
    x2 = _out_ffn(x2, [o.reshape(t, MLA_HEADS * MLA_V_DIM)], mla_w_o[0].astype(BF16),
                  row(ffn_norm[1]), wg, wu, wd_ffn, row(final_norm), layer=1, tm=512, final_norm=True)
    return x2.reshape(b, s, d)
```

```python
import functools
import math

import numpy as np
import jax
import jax.numpy as jnp
from jax import lax
from jax.experimental import pallas as pl
from jax.experimental.pallas import tpu as pltpu

F32 = jnp.float32
BF16 = jnp.bfloat16

D_MODEL = 1024
EPS = 1e-6
HEAD_DIM = 64
LANES = 128
SB_HEADS = 8
SW_HEADS = 8
SW_KV_HEADS = 2
WINDOW = 128
SB_WIDTH = SB_HEADS * HEAD_DIM
SW_Q_WIDTH = SW_HEADS * HEAD_DIM
SW_KV_WIDTH = SW_KV_HEADS * HEAD_DIM
EVEN_IN_WIDTH = 3 * SB_WIDTH + SW_Q_WIDTH + 2 * SW_KV_WIDTH
REL_BUCKETS = 32
REL_MAX_DIST = 128
MLA_HEADS = 16
MLA_NOPE_DIM = 64
MLA_ROPE_DIM = 32
MLA_V_DIM = 64
MLA_Q_RANK = 384
MLA_KV_RANK = 256
ROPE_THETA = 10000.0
LOG2E = math.log2(math.e)
SB_SATURATED = 128.0
VMEM_LIMIT_BYTES = 56 * 1024 * 1024


def _cparams(n_axes):
    return pltpu.CompilerParams(
        dimension_semantics=("arbitrary",) * n_axes,
        vmem_limit_bytes=VMEM_LIMIT_BYTES)


def _rms(x, g):
    return x * lax.rsqrt(jnp.mean(x * x, axis=-1, keepdims=True) + EPS) * g


def _dot(a, b):
    return jnp.dot(a, b, preferred_element_type=F32)


def _dot_nt(a, b):
    return lax.dot_general(a, b, (((1,), (1,)), ((), ())), preferred_element_type=F32)


def _norm_proj_kernel(x_ref, g_ref, w_ref, o_ref):
    h = _rms(x_ref[...], g_ref[...]).astype(BF16)
    o_ref[...] = _dot(h, w_ref[...]).astype(o_ref.dtype)


def _norm_proj(x2, g, w, *, tm):
    t, d = x2.shape
    n = w.shape[1]
    return pl.pallas_call(
        _norm_proj_kernel,
        out_shape=jax.ShapeDtypeStruct((t, n), BF16),
        grid=(t // tm,),
        in_specs=[pl.BlockSpec((tm, d), lambda i: (i, 0)),
                  pl.BlockSpec((1, d), lambda i: (0, 0)),
                  pl.BlockSpec((d, n), lambda i: (0, 0))],
        out_specs=pl.BlockSpec((tm, n), lambda i: (i, 0)),
        compiler_params=_cparams(1),
        name="norm_proj",
    )(x2, g, w)


def _softplus(z):
    neg_abs = lax.bitcast_convert_type(
        lax.bitcast_convert_type(z, jnp.uint32) | jnp.uint32(0x80000000), F32)
    return jnp.maximum(z, 0.0) + jnp.log(1.0 + jnp.exp(neg_abs))


def _sb_kernel(q_ref, k_ref, v_ref, o_ref, r_ref, acc_ref, *, tq, sub, unroll):
    qi = pl.program_id(2)
    r = tq // sub
    q2 = q_ref[0]
    lane = lax.broadcasted_iota(jnp.int32, (tq, LANES), 1)
    low = lane < HEAD_DIM
    zero = jnp.zeros_like(q2)
    scale = HEAD_DIM ** -0.5
    qm = (jnp.where(low, q2, zero) * scale, jnp.where(low, zero, q2) * scale)

    tri_row = lax.broadcasted_iota(jnp.int32, (sub, sub), 0)
    tri_col = lax.broadcasted_iota(jnp.int32, (sub, sub), 1)
    tri = (tri_row >= tri_col).astype(BF16)

    r_ref[...] = jnp.zeros(r_ref.shape, F32)
    acc_ref[...] = jnp.zeros(acc_ref.shape, F32)

    def tile(hh, r0, nrows, j, masked):
        rows = pl.ds(r0, nrows)
        start = pl.multiple_of(j * sub, sub)
        kblk = k_ref[0, pl.ds(start, sub), :]
        vblk = v_ref[0, pl.ds(start, sub), :]
        z = _dot_nt(qm[hh][r0:r0 + nrows], kblk)
        sp = _softplus(z)
        if masked:
            row = lax.broadcasted_iota(jnp.int32, (nrows, sub), 0)
            col = lax.broadcasted_iota(jnp.int32, (nrows, sub), 1)
            causal = col < row
            sp = jnp.where(causal, sp, 0.0)
        later = r_ref[hh, rows, :]
        c = _dot(sp.astype(BF16), tri) + jnp.tile(later, (1, sub // LANES))
        w = jnp.exp(z - c)
        if masked:
            w = jnp.where(causal, w, 0.0)
        acc_ref[hh, rows, :] += _dot(w.astype(BF16), vblk)
        r_ref[hh, rows, :] = jnp.broadcast_to(c[:, 0:1], (nrows, LANES))

    for u in reversed(range(r)):
        for hh in range(2):
            tile(hh, u * sub, tq - u * sub, qi * r + u, True)

    def alive():
        return jnp.min(r_ref[...]) < SB_SATURATED

    n_iter = (qi * r) // unroll

    def cond(state):
        jj, a = state
        return jnp.logical_and(jj < n_iter, a)

    def body(state):
        jj, _ = state
        for t in range(unroll):
            for hh in range(2):
                tile(hh, 0, tq, qi * r - 1 - (jj * unroll + t), False)
        return jj + 1, alive()

    lax.while_loop(cond, body, (jnp.int32(0), alive()))
    o_ref[0] = jnp.where(low, acc_ref[0], acc_ref[1]).astype(o_ref.dtype)


def _sb_attention(proj3, *, tq, sub, unroll):
    b, s, _ = proj3.shape
    assert (tq // sub) % unroll == 0
    n_pairs = SB_WIDTH // LANES
    k_off = SB_WIDTH // LANES
    v_off = 2 * SB_WIDTH // LANES
    return pl.pallas_call(
        functools.partial(_sb_kernel, tq=tq, sub=sub, unroll=unroll),
        out_shape=jax.ShapeDtypeStruct((b, s, SB_WIDTH), BF16),
        grid=(b, n_pairs, s // tq),
        in_specs=[pl.BlockSpec((1, tq, LANES), lambda bi, p, i: (bi, i, p)),
                  pl.BlockSpec((1, s, LANES), lambda bi, p, i: (bi, 0, k_off + p)),
                  pl.BlockSpec((1, s, LANES), lambda bi, p, i: (bi, 0, v_off + p))],
        out_specs=pl.BlockSpec((1, tq, LANES), lambda bi, p, i: (bi, i, p)),
        scratch_shapes=[pltpu.VMEM((2, tq, LANES), F32), pltpu.VMEM((2, tq, LANES), F32)],
        compiler_params=_cparams(3),
        name="sb_attention",
    )(proj3, proj3, proj3)


def _t5_bucket_table():
    max_exact = REL_BUCKETS // 2
    rel = WINDOW + np.arange(WINDOW)[:, None] - np.arange(2 * WINDOW)[None, :]
    rel = np.maximum(rel, 0)
    relf = np.maximum(rel, 1).astype(np.float32)
    large = max_exact + (np.log(relf / np.float32(max_exact)) / np.float32(math.log(REL_MAX_DIST / max_exact))
                         * np.float32(REL_BUCKETS - max_exact)).astype(np.int32)
    large = np.minimum(large, REL_BUCKETS - 1)
    return np.where(rel < max_exact, rel, large).astype(np.int32)


def _swap_halves(x):
    return jnp.concatenate([x[:, HEAD_DIM:], x[:, :HEAD_DIM]], axis=1)


def _swa_kernel(table_ref, sinks_ref, bucket_ref, q_ref, kp_ref, kc_ref, vp_ref, vc_ref,
                o_ref, bias_ref, *, qb):
    bi = pl.program_id(0)
    n = pl.program_id(1)
    blk = WINDOW
    t = lax.broadcasted_iota(jnp.int32, (blk, 2 * blk), 0)
    s = lax.broadcasted_iota(jnp.int32, (blk, 2 * blk), 1)

    @pl.when(jnp.logical_and(bi == 0, n == 0))
    def _():
        bucket = bucket_ref[...]
        rel = blk + t - s
        in_window = (rel >= 0) & (rel < WINDOW)
        for h in range(SW_HEADS):
            bias = jnp.zeros((blk, 2 * blk), F32)
            for bk in range(REL_BUCKETS):
                bias = jnp.where(bucket == bk, table_ref[bk, h], bias)
            bias_ref[h] = jnp.where(in_window, bias, -jnp.inf)

    kk = jnp.concatenate([kp_ref[0], kc_ref[0]], axis=0)
    vv = jnp.concatenate([vp_ref[0], vc_ref[0]], axis=0)
    kk_sw = _swap_halves(kk)
    vv_sw = _swap_halves(vv)
    lane = lax.broadcasted_iota(jnp.int32, (blk, LANES), 1)
    low = lane < HEAD_DIM
    scale = HEAD_DIM ** -0.5
    group = SW_HEADS // SW_KV_HEADS
    has_prev = (s >= blk) | (n > 0)

    for j in range(qb):
        rows = slice(j * blk, (j + 1) * blk)
        keys = slice(j * blk, (j + 2) * blk)
        for p in range(SW_HEADS // 2):
            q2 = q_ref[0, rows, p * LANES:(p + 1) * LANES]
            zero = jnp.zeros_like(q2)
            outs = []
            for half in range(2):
                h = 2 * p + half
                kv = h // group
                qh = (jnp.where(low, q2, zero) if half == 0 else jnp.where(low, zero, q2)) * scale
                k_use = (kk if kv == half else kk_sw)[keys]
                v_use = (vv if kv == half else vv_sw)[keys]
                logits = _dot_nt(qh, k_use) + bias_ref[h]
                if j == 0:
                    logits = jnp.where(has_prev, logits, -jnp.inf)
                sink = sinks_ref[0, h]
                m = jnp.maximum(jnp.max(logits, axis=-1, keepdims=True), sink)
                pr = jnp.exp(logits - m)
                denom = jnp.sum(pr, axis=-1, keepdims=True) + jnp.exp(sink - m)
                outs.append(_dot(pr.astype(BF16), v_use) / denom)
            o_ref[0, rows, p * LANES:(p + 1) * LANES] = jnp.where(low, outs[0], outs[1]).astype(o_ref.dtype)


def _swa_attention(proj3, table, sinks, *, qb):
    b, s, _ = proj3.shape
    rows = qb * WINDOW
    q_blk = (3 * SB_WIDTH) // SW_Q_WIDTH
    k_blk = (3 * SB_WIDTH + SW_Q_WIDTH) // LANES
    v_blk = k_blk + 1
    bucket = jnp.asarray(_t5_bucket_table())
    smem = pl.BlockSpec(memory_space=pltpu.SMEM)
    prev = lambda c: (lambda bi, n: (bi, jnp.maximum(n * qb - 1, 0), c))
    cur = lambda c: (lambda bi, n: (bi, n, c))
    return pl.pallas_call(
        functools.partial(_swa_kernel, qb=qb),
        out_shape=jax.ShapeDtypeStruct((b, s, SW_Q_WIDTH), BF16),
        grid=(b, s // rows),
        in_specs=[smem, smem,
                  pl.BlockSpec((WINDOW, 2 * WINDOW), lambda bi, n: (0, 0)),
                  pl.BlockSpec((1, rows, SW_Q_WIDTH), cur(q_blk)),
                  pl.BlockSpec((1, WINDOW, LANES), prev(k_blk)),
                  pl.BlockSpec((1, rows, LANES), cur(k_blk)),
                  pl.BlockSpec((1, WINDOW, LANES), prev(v_blk)),
                  pl.BlockSpec((1, rows, LANES), cur(v_blk))],
        out_specs=pl.BlockSpec((1, rows, SW_Q_WIDTH), lambda bi, n: (bi, n, 0)),
        scratch_shapes=[pltpu.VMEM((SW_HEADS, WINDOW, 2 * WINDOW), F32)],
        compiler_params=_cparams(2),
        name="swa_attention",
    )(table, sinks, bucket, proj3, proj3, proj3, proj3, proj3)


def _to_bf16_kernel(w_ref, o_ref):
    o_ref[...] = w_ref[...].astype(BF16)


def _to_bf16(w, *, tr):
    n, r, c = w.shape
    spec = pl.BlockSpec((1, tr, c), lambda l, i: (l, i, 0))
    return pl.pallas_call(
        _to_bf16_kernel,
        out_shape=jax.ShapeDtypeStruct(w.shape, BF16),
        grid=(n, r // tr),
        in_specs=[spec],
        out_specs=spec,
        compiler_params=_cparams(2),
        name="to_bf16",
    )(w)


def _out_ffn_kernel(*refs, n_parts, final_norm):
    x_ref = refs[0]
    part_refs = refs[1:1 + n_parts]
    wo_refs = refs[1 + n_parts:1 + 2 * n_parts]
    g_ref, wg_ref, wu_ref, wd_ref, fg_ref, o_ref = refs[1 + 2 * n_parts:]
    x = x_ref[...]
    for a_ref, w_ref in zip(part_refs, wo_refs):
        x = x + _dot(a_ref[...], w_ref[...])
    h = _rms(x, g_ref[...]).astype(BF16)
    gate = _dot(h, wg_ref[...])
    up = _dot(h, wu_ref[...])
    act = (gate / (1.0 + jnp.exp(-gate)) * up).astype(BF16)
    y = x + _dot(act, wd_ref[...])
    if final_norm:
        y = _rms(y, fg_ref[...])
    o_ref[...] = y


def _out_ffn(x2, parts, w_out, g, wg, wu, wd, fg, *, layer, tm, final_norm):
    t, d = x2.shape
    hid = wg.shape[2]
    n_parts = len(parts)
    kp = w_out.shape[0] // n_parts
    resident = dict(pipeline_mode=pl.Buffered(1))
    row_tile = lambda width: pl.BlockSpec((tm, width), lambda i: (i, 0))
    const = lambda r, c: pl.BlockSpec((None, r, c), lambda i: (layer, 0, 0), **resident)
    in_specs = [row_tile(d)] + [row_tile(kp) for _ in parts]
    in_specs += [pl.BlockSpec((kp, d), functools.partial(lambda i, c: (c, 0), c=c), **resident)
                 for c in range(n_parts)]
    in_specs += [pl.BlockSpec((1, d), lambda i: (0, 0)), const(d, hid), const(d, hid),
                 const(hid, d), pl.BlockSpec((1, d), lambda i: (0, 0))]
    return pl.pallas_call(
        functools.partial(_out_ffn_kernel, n_parts=n_parts, final_norm=final_norm),
        out_shape=jax.ShapeDtypeStruct((t, d), F32),
        grid=(t // tm,),
        in_specs=in_specs,
        out_specs=row_tile(d),
        compiler_params=_cparams(1),
        name="out_ffn_final" if final_norm else "out_ffn",
    )(x2, *parts, *([w_out] * n_parts), g, wg, wu, wd, fg)


def _mla_proj_kernel(x_ref, pos_ref, freq_ref, g_ref, wd_ref, qg_ref, wq_ref,
                     kg_ref, wk_ref, wv_ref, q_ref, k_ref, v_ref):
    tm = x_ref.shape[0]
    h = _rms(x_ref[...], g_ref[...]).astype(BF16)
    down = _dot(h, wd_ref[...])
    c_q = down[:, :MLA_Q_RANK]
    c_kv = down[:, MLA_Q_RANK:MLA_Q_RANK + MLA_KV_RANK]
    kr = down[:, MLA_Q_RANK + MLA_KV_RANK:]

    half = MLA_ROPE_DIM // 2
    ang = pos_ref[...].astype(F32) * freq_ref[...]
    lane = lax.broadcasted_iota(jnp.int32, (tm, LANES), 1)
    first = lane < half
    second = (lane >= LANES // 2) & (lane < LANES // 2 + half)

    def spread(dense):
        rows8 = jnp.broadcast_to(dense[:, None, :], (tm // 8, 8, LANES)).reshape(tm, LANES)
        y = pltpu.roll(rows8, 0, 1, stride=half, stride_axis=0)
        return y, pltpu.roll(y, LANES // 2, 1)

    cos_1, cos_2 = spread(jnp.cos(ang))
    sin_1, sin_2 = spread(jnp.sin(ang))
    cos_p = jnp.where(first, cos_1, jnp.where(second, cos_2, 1.0))
    sin_p = jnp.where(first, -sin_1, jnp.where(second, sin_2, 0.0))

    def rope(blk):
        return blk * cos_p + pltpu.roll(blk, LANES // 2, 1) * sin_p

    cqn = _rms(c_q, qg_ref[...]).astype(BF16)
    ckvn = _rms(c_kv, kg_ref[...]).astype(BF16)
    k_rope = rope(kr)
    q = _dot(cqn, wq_ref[...])
    kn = _dot(ckvn, wk_ref[...])
    for hd in range(MLA_HEADS):
        sl = slice(hd * LANES, (hd + 1) * LANES)
        q_ref[:, sl] = rope(q[:, sl]).astype(q_ref.dtype)
        k_ref[:, sl] = (kn[:, sl] + k_rope).astype(k_ref.dtype)
    v_ref[...] = _dot(ckvn, wv_ref[...]).astype(v_ref.dtype)


def _mla_proj(x2, pos2, freq, g, wd, qg, wq, kg, wk, wv, *, tm):
    t, d = x2.shape
    full = lambda a: pl.BlockSpec(a.shape, lambda i: (0,) * a.ndim)
    qk_w = MLA_HEADS * LANES
    v_w = MLA_HEADS * MLA_V_DIM
    return pl.pallas_call(
        _mla_proj_kernel,
        out_shape=(jax.ShapeDtypeStruct((t, qk_w), BF16),
                   jax.ShapeDtypeStruct((t, qk_w), BF16),
                   jax.ShapeDtypeStruct((t, v_w), BF16)),
        grid=(t // tm,),
        in_specs=[pl.BlockSpec((tm, d), lambda i: (i, 0)),
                  pl.BlockSpec((tm // 8, LANES), lambda i: (i, 0)),
                  full(freq), full(g), full(wd), full(qg), full(wq),
                  full(kg), full(wk), full(wv)],
        out_specs=(pl.BlockSpec((tm, qk_w), lambda i: (i, 0)),
                   pl.BlockSpec((tm, qk_w), lambda i: (i, 0)),
                   pl.BlockSpec((tm, v_w), lambda i: (i, 0))),
        compiler_params=_cparams(1),
        name="mla_proj",
    )(x2, pos2, freq, g, wd, qg, wq, kg, wk, wv)


def _mla_attn_kernel(q_ref, k_ref, v_ref, o_ref, va_ref, vb_ref, m_ref, acc_ref, *, tq, tk, unroll):
    qi = pl.program_id(2)
    lane_s = lax.broadcasted_iota(jnp.int32, va_ref.shape, 1)

    @pl.when(qi == 0)
    def _():
        v_all = v_ref[0]
        one = jnp.ones_like(v_all)
        va_ref[...] = jnp.where(lane_s < MLA_V_DIM, v_all, one)
        vb_ref[...] = jnp.where(lane_s < MLA_V_DIM, one, v_all)

    r = tq // tk
    c2 = (MLA_NOPE_DIM + MLA_ROPE_DIM) ** -0.5 * LOG2E
    vaug = (va_ref, vb_ref)

    m_ref[...] = jnp.full(m_ref.shape, -jnp.inf, F32)
    acc_ref[...] = jnp.zeros(acc_ref.shape, F32)

    def tile(hh, r0, nrows, j, masked):
        rows = pl.ds(r0, nrows)
        q = q_ref[0, rows, hh * LANES:(hh + 1) * LANES]
        start = pl.multiple_of(j * tk, tk)
        kblk = k_ref[0, pl.ds(start, tk), hh * LANES:(hh + 1) * LANES]
        vblk = vaug[hh][pl.ds(start, tk), :]
        s2 = _dot_nt(q, kblk) * c2
        if masked:
            row = lax.broadcasted_iota(jnp.int32, (nrows, tk), 0)
            col = lax.broadcasted_iota(jnp.int32, (nrows, tk), 1)
            s2 = jnp.where(col <= row, s2, -jnp.inf)
        m = m_ref[hh, rows, :]
        m_new = jnp.maximum(m, jnp.max(s2, axis=-1, keepdims=True))
        alpha = jnp.exp2(m - m_new)
        p = jnp.exp2(s2 - jnp.tile(m_new, (1, tk // LANES)))
        acc_ref[hh, rows, :] = alpha * acc_ref[hh, rows, :] + _dot(p.astype(BF16), vblk)
        m_ref[hh, rows, :] = m_new

    def body(jj, carry):
        for t in range(unroll):
            for hh in range(2):
                tile(hh, 0, tq, jj * unroll + t, False)
        return carry

    lax.fori_loop(0, (qi * r) // unroll, body, 0)
    for u in range(r):
        for hh in range(2):
            tile(hh, u * tk, tq - u * tk, qi * r + u, True)

    lane = lax.broadcasted_iota(jnp.int32, (tq, LANES), 1)
    acc_a = acc_ref[0]
    acc_b = acc_ref[1]
    out_a = acc_a / pltpu.roll(acc_a, MLA_V_DIM, 1)
    out_b = acc_b / pltpu.roll(acc_b, MLA_V_DIM, 1)
    o_ref[0] = jnp.where(lane < MLA_V_DIM, out_a, out_b).astype(o_ref.dtype)


def _mla_attention(q3, k3, v3, *, tq, tk, unroll):
    b, s, _ = q3.shape
    assert (tq // tk) % unroll == 0
    n_pairs = MLA_HEADS // 2
    return pl.pallas_call(
        functools.partial(_mla_attn_kernel, tq=tq, tk=tk, unroll=unroll),
        out_shape=jax.ShapeDtypeStruct((b, s, MLA_HEADS * MLA_V_DIM), BF16),
        grid=(b, n_pairs, s // tq),
        in_specs=[pl.BlockSpec((1, tq, 2 * LANES), lambda bi, p, i: (bi, i, p)),
                  pl.BlockSpec((1, s, 2 * LANES), lambda bi, p, i: (bi, 0, p)),
                  pl.BlockSpec((1, s, LANES), lambda bi, p, i: (bi, 0, p))],
        out_specs=pl.BlockSpec((1, tq, LANES), lambda bi, p, i: (bi, i, p)),
        scratch_shapes=[pltpu.VMEM((s, LANES), BF16), pltpu.VMEM((s, LANES), BF16),
                        pltpu.VMEM((2, tq, LANES), F32), pltpu.VMEM((2, tq, LANES), F32)],
        compiler_params=_cparams(3),
        name="mla_attention",
    )(q3, k3, v3)


def _head_lanes(nope, rope):
    half = MLA_ROPE_DIM // 2
    split = LANES // 2 - half
    zero = jnp.zeros(nope.shape[:-1] + (LANES - MLA_NOPE_DIM - MLA_ROPE_DIM,), nope.dtype)
    return jnp.concatenate([rope[..., :half], nope[..., :split], rope[..., half:], nope[..., split:], zero],
                           axis=-1)


def _mla_weight_layout(w_down, w_uq, w_ukv):
    d = w_down.shape[0]
    w_kr = w_down[:, MLA_Q_RANK + MLA_KV_RANK:]
    wd = jnp.concatenate([w_down[:, :MLA_Q_RANK + MLA_KV_RANK],
                          _head_lanes(jnp.zeros((d, MLA_NOPE_DIM), F32), w_kr)], axis=1)

    wq = w_uq.reshape(MLA_Q_RANK, MLA_HEADS, MLA_NOPE_DIM + MLA_ROPE_DIM)
    wq = _head_lanes(wq[..., :MLA_NOPE_DIM], wq[..., MLA_NOPE_DIM:]).reshape(MLA_Q_RANK, MLA_HEADS * LANES)

    wkv = w_ukv.reshape(MLA_KV_RANK, MLA_HEADS, MLA_NOPE_DIM + MLA_V_DIM)
    wk_nope, wv = wkv[..., :MLA_NOPE_DIM], wkv[..., MLA_NOPE_DIM:]
    wk = _head_lanes(wk_nope, jnp.zeros((MLA_KV_RANK, MLA_HEADS, MLA_ROPE_DIM), F32))
    wk = wk.reshape(MLA_KV_RANK, MLA_HEADS * LANES)
    wv = wv.reshape(MLA_KV_RANK, MLA_HEADS * MLA_V_DIM)
    return wd.astype(BF16), wq.astype(BF16), wk.astype(BF16), wv.astype(BF16)


def _rope_dense_layout(positions):
    half = MLA_ROPE_DIM // 2
    n_blk = LANES // half
    freqs = ROPE_THETA ** (-jnp.arange(half, dtype=F32) / half)
    order = np.array([(-b) % n_blk for b in range(n_blk)])
    pos_dense = jnp.repeat(positions.reshape(-1, n_blk)[:, order], half, axis=1)
    return pos_dense, jnp.tile(freqs, n_blk)[None, :]


def kernel(x, positions, attn_norm, ffn_norm, even_w_in, even_sinks, even_w_out, rel_bias_table, mla_w_down, mla_q_norm, mla_w_uq, mla_kv_norm, mla_w_ukv, mla_w_o, ffn_w_gate, ffn_w_up, ffn_w_down, final_norm):
    b, s, d = x.shape
    t = b * s
    x2 = x.reshape(t, d)
    row = lambda v: v.reshape(1, -1)

    proj = _norm_proj(x2, row(attn_norm[0]), even_w_in[0].astype(BF16), tm=1024)
    proj3 = proj.reshape(b, s, EVEN_IN_WIDTH)
    o_a = _sb_attention(proj3, tq=1024, sub=256, unroll=1)
    o_b = _swa_attention(proj3, rel_bias_table, even_sinks[0].reshape(1, SW_HEADS), qb=4)
    wg, wu, wd_ffn = _to_bf16(ffn_w_gate, tr=256), _to_bf16(ffn_w_up, tr=256), _to_bf16(ffn_w_down, tr=704)
    x2 = _out_ffn(x2, [o_a.reshape(t, SB_WIDTH), o_b.reshape(t, SW_Q_WIDTH)], even_w_out[0].astype(BF16),
                  row(ffn_norm[0]), wg, wu, wd_ffn, row(final_norm), layer=0, tm=512, final_norm=False)

    wd, wq, wk, wv = _mla_weight_layout(mla_w_down[0], mla_w_uq[0], mla_w_ukv[0])
    pos_dense, freq_dense = _rope_dense_layout(positions)
    q, k, v = _mla_proj(x2, pos_dense, freq_dense, row(attn_norm[1]), wd,
                        row(mla_q_norm[0]), wq, row(mla_kv_norm[0]), wk, wv, tm=512)
    o = _mla_attention(q.reshape(b, s, -1), k.reshape(b, s, -1), v.reshape(b, s, -1), tq=2048, tk=512, unroll=2)
    x2 = _out_ffn(x2, [o.reshape(t, MLA_HEADS * MLA_V_DIM)], mla_w_o[0].astype(BF16),
                  row(ffn_norm[1]), wg, wu, wd_ffn, row(final_norm), layer=1, tm=512, final_norm=True)
    return x2.reshape(b, s, d)
```

```python
import functools
import math

import numpy as np
import jax
import jax.numpy as jnp
from jax import lax
from jax.experimental import pallas as pl
from jax.experimental.pallas import tpu as pltpu

F32 = jnp.float32
BF16 = jnp.bfloat16

D_MODEL = 1024
EPS = 1e-6
HEAD_DIM = 64
LANES = 128
SB_HEADS = 8
SW_HEADS = 8
SW_KV_HEADS = 2
WINDOW = 128
SB_WIDTH = SB_HEADS * HEAD_DIM
SW_Q_WIDTH = SW_HEADS * HEAD_DIM
SW_KV_WIDTH = SW_KV_HEADS * HEAD_DIM
EVEN_IN_WIDTH = 3 * SB_WIDTH + SW_Q_WIDTH + 2 * SW_KV_WIDTH
REL_BUCKETS = 32
REL_MAX_DIST = 128
MLA_HEADS = 16
MLA_NOPE_DIM = 64
MLA_ROPE_DIM = 32
MLA_V_DIM = 64
MLA_Q_RANK = 384
MLA_KV_RANK = 256
ROPE_THETA = 10000.0
LOG2E = math.log2(math.e)
SB_SATURATED = 128.0
VMEM_LIMIT_BYTES = 56 * 1024 * 1024


def _cparams(n_axes):
    return pltpu.CompilerParams(
        dimension_semantics=("arbitrary",) * n_axes,
        vmem_limit_bytes=VMEM_LIMIT_BYTES)


def _rms(x, g):
    return x * lax.rsqrt(jnp.mean(x * x, axis=-1, keepdims=True) + EPS) * g


def _dot(a, b):
    return jnp.dot(a, b, preferred_element_type=F32)


def _dot_nt(a, b):
    return lax.dot_general(a, b, (((1,), (1,)), ((), ())), preferred_element_type=F32)


def _norm_proj_kernel(x_ref, g_ref, w_ref, o_ref):
    h = _rms(x_ref[...], g_ref[...]).astype(BF16)
    o_ref[...] = _dot(h, w_ref[...]).astype(o_ref.dtype)


def _norm_proj(x2, g, w, *, tm):
    t, d = x2.shape
    n = w.shape[1]
    return pl.pallas_call(
        _norm_proj_kernel,
        out_shape=jax.ShapeDtypeStruct((t, n), BF16),
        grid=(t // tm,),
        in_specs=[pl.BlockSpec((tm, d), lambda i: (i, 0)),
                  pl.BlockSpec((1, d), lambda i: (0, 0)),
                  pl.BlockSpec((d, n), lambda i: (0, 0))],
        out_specs=pl.BlockSpec((tm, n), lambda i: (i, 0)),
        compiler_params=_cparams(1),
        name="norm_proj",
    )(x2, g, w)


def _softplus(z):
    neg_abs = lax.bitcast_convert_type(
        lax.bitcast_convert_type(z, jnp.uint32) | jnp.uint32(0x80000000), F32)
    return jnp.maximum(z, 0.0) + jnp.log(1.0 + jnp.exp(neg_abs))


def _sb_kernel(q_ref, k_ref, v_ref, o_ref, r_ref, acc_ref, *, tq, sub):
    qi = pl.program_id(2)
    r = tq // sub
    q2 = q_ref[0]
    lane = lax.broadcasted_iota(jnp.int32, (tq, LANES), 1)
    low = lane < HEAD_DIM
    zero = jnp.zeros_like(q2)
    scale = HEAD_DIM ** -0.5
    qm = (jnp.where(low, q2, zero) * scale, jnp.where(low, zero, q2) * scale)

    tri_row = lax.broadcasted_iota(jnp.int32, (sub, sub), 0)
    tri_col = lax.broadcasted_iota(jnp.int32, (sub, sub), 1)
    tri = (tri_row >= tri_col).astype(BF16)
    strictly_lower = tri_col < tri_row

    r_ref[...] = jnp.zeros(r_ref.shape, F32)
    acc_ref[...] = jnp.zeros(acc_ref.shape, F32)

    def tile(hh, r0, nrows, j, masked):
        rows = pl.ds(r0, nrows)
        start = pl.multiple_of(j * sub, sub)
        kblk = k_ref[0, pl.ds(start, sub), :]
        vblk = v_ref[0, pl.ds(start, sub), :]
        z = _dot_nt(qm[hh][r0:r0 + nrows], kblk)
        if masked:
            top = jnp.where(strictly_lower, z[:sub], -jnp.inf)
            z = top if nrows == sub else jnp.concatenate([top, z[sub:]], axis=0)
        sp = _softplus(z)
        later = r_ref[hh, rows, :]
        c = _dot(sp.astype(BF16), tri) + jnp.tile(later, (1, sub // LANES))
        w = jnp.exp(z - c)
        acc_ref[hh, rows, :] += _dot(w.astype(BF16), vblk)
        r_ref[hh, rows, :] = jnp.broadcast_to(c[:, 0:1], (nrows, LANES))

    def saturated(first_row):
        return jnp.min(r_ref[:, first_row:, :]) >= SB_SATURATED

    for u in reversed(range(r)):
        j = qi * r + u
        r0 = u * sub
        if tq - r0 > 2 * sub:
            tail_done = saturated(r0 + 2 * sub)

            @pl.when(tail_done)
            def _():
                for hh in range(2):
                    tile(hh, r0, 2 * sub, j, True)

            @pl.when(jnp.logical_not(tail_done))
            def _():
                for hh in range(2):
                    tile(hh, r0, tq - r0, j, True)
        else:
            for hh in range(2):
                tile(hh, r0, tq - r0, j, True)

    def alive():
        top = jnp.min(r_ref[:, :sub, :]) < SB_SATURATED
        rest = jnp.min(r_ref[:, sub:, :]) < SB_SATURATED
        return top, rest

    n_blocks = qi * r

    def cond(state):
        jj, top, rest = state
        return jnp.logical_and(jj < n_blocks, jnp.logical_or(top, rest))

    def body(state):
        jj, _, rest = state
        j = n_blocks - 1 - jj

        @pl.when(rest)
        def _():
            for hh in range(2):
                tile(hh, 0, tq, j, False)

        @pl.when(jnp.logical_not(rest))
        def _():
            for hh in range(2):
                tile(hh, 0, sub, j, False)

        return (jj + 1,) + alive()

    lax.while_loop(cond, body, (jnp.int32(0),) + alive())
    o_ref[0] = jnp.where(low, acc_ref[0], acc_ref[1]).astype(o_ref.dtype)


def _sb_attention(proj3, *, tq, sub):
    b, s, _ = proj3.shape
    n_pairs = SB_WIDTH // LANES
    k_off = SB_WIDTH // LANES
    v_off = 2 * SB_WIDTH // LANES
    return pl.pallas_call(
        functools.partial(_sb_kernel, tq=tq, sub=sub),
        out_shape=jax.ShapeDtypeStruct((b, s, SB_WIDTH), BF16),
        grid=(b, n_pairs, s // tq),
        in_specs=[pl.BlockSpec((1, tq, LANES), lambda bi, p, i: (bi, i, p)),
                  pl.BlockSpec((1, s, LANES), lambda bi, p, i: (bi, 0, k_off + p)),
                  pl.BlockSpec((1, s, LANES), lambda bi, p, i: (bi, 0, v_off + p))],
        out_specs=pl.BlockSpec((1, tq, LANES), lambda bi, p, i: (bi, i, p)),
        scratch_shapes=[pltpu.VMEM((2, tq, LANES), F32), pltpu.VMEM((2, tq, LANES), F32)],
        compiler_params=_cparams(3),
        name="sb_attention",
    )(proj3, proj3, proj3)


def _t5_bucket_table():
    max_exact = REL_BUCKETS // 2
    rel = WINDOW + np.arange(WINDOW)[:, None] - np.arange(2 * WINDOW)[None, :]
    rel = np.maximum(rel, 0)
    relf = np.maximum(rel, 1).astype(np.float32)
    large = max_exact + (np.log(relf / np.float32(max_exact)) / np.float32(math.log(REL_MAX_DIST / max_exact))
                         * np.float32(REL_BUCKETS - max_exact)).astype(np.int32)
    large = np.minimum(large, REL_BUCKETS - 1)
    return np.where(rel < max_exact, rel, large).astype(np.int32)


def _swap_halves(x):
    return jnp.concatenate([x[:, HEAD_DIM:], x[:, :HEAD_DIM]], axis=1)


def _swa_kernel(table_ref, sinks_ref, bucket_ref, q_ref, kp_ref, kc_ref, vp_ref, vc_ref,
                o_ref, bias_ref, *, qb):
    bi = pl.program_id(0)
    n = pl.program_id(1)
    blk = WINDOW
    t = lax.broadcasted_iota(jnp.int32, (blk, 2 * blk), 0)
    s = lax.broadcasted_iota(jnp.int32, (blk, 2 * blk), 1)

    @pl.when(jnp.logical_and(bi == 0, n == 0))
    def _():
        bucket = bucket_ref[...]
        rel = blk + t - s
        in_window = (rel >= 0) & (rel < WINDOW)
        for h in range(SW_HEADS):
            bias = jnp.zeros((blk, 2 * blk), F32)
            for bk in range(REL_BUCKETS):
                bias = jnp.where(bucket == bk, table_ref[bk, h], bias)
            bias_ref[h] = jnp.where(in_window, bias, -jnp.inf)

    kk = jnp.concatenate([kp_ref[0], kc_ref[0]], axis=0)
    vv = jnp.concatenate([vp_ref[0], vc_ref[0]], axis=0)
    kk_sw = _swap_halves(kk)
    vv_sw = _swap_halves(vv)
    lane = lax.broadcasted_iota(jnp.int32, (blk, LANES), 1)
    low = lane < HEAD_DIM
    scale = HEAD_DIM ** -0.5
    group = SW_HEADS // SW_KV_HEADS
    has_prev = (s >= blk) | (n > 0)

    for j in range(qb):
        rows = slice(j * blk, (j + 1) * blk)
        keys = slice(j * blk, (j + 2) * blk)
        for p in range(SW_HEADS // 2):
            q2 = q_ref[0, rows, p * LANES:(p + 1) * LANES]
            zero = jnp.zeros_like(q2)
            outs = []
            for half in range(2):
                h = 2 * p + half
                kv = h // group
                qh = (jnp.where(low, q2, zero) if half == 0 else jnp.where(low, zero, q2)) * scale
                k_use = (kk if kv == half else kk_sw)[keys]
                v_use = (vv if kv == half else vv_sw)[keys]
                logits = _dot_nt(qh, k_use) + bias_ref[h]
                if j == 0:
                    logits = jnp.where(has_prev, logits, -jnp.inf)
                sink = sinks_ref[0, h]
                m = jnp.maximum(jnp.max(logits, axis=-1, keepdims=True), sink)
                pr = jnp.exp(logits - m)
                denom = jnp.sum(pr, axis=-1, keepdims=True) + jnp.exp(sink - m)
                outs.append(_dot(pr.astype(BF16), v_use) / denom)
            o_ref[0, rows, p * LANES:(p + 1) * LANES] = jnp.where(low, outs[0], outs[1]).astype(o_ref.dtype)


def _swa_attention(proj3, table, sinks, *, qb):
    b, s, _ = proj3.shape
    rows = qb * WINDOW
    q_blk = (3 * SB_WIDTH) // SW_Q_WIDTH
    k_blk = (3 * SB_WIDTH + SW_Q_WIDTH) // LANES
    v_blk = k_blk + 1
    bucket = jnp.asarray(_t5_bucket_table())
    smem = pl.BlockSpec(memory_space=pltpu.SMEM)
    prev = lambda c: (lambda bi, n: (bi, jnp.maximum(n * qb - 1, 0), c))
    cur = lambda c: (lambda bi, n: (bi, n, c))
    return pl.pallas_call(
        functools.partial(_swa_kernel, qb=qb),
        out_shape=jax.ShapeDtypeStruct((b, s, SW_Q_WIDTH), BF16),
        grid=(b, s // rows),
        in_specs=[smem, smem,
                  pl.BlockSpec((WINDOW, 2 * WINDOW), lambda bi, n: (0, 0)),
                  pl.BlockSpec((1, rows, SW_Q_WIDTH), cur(q_blk)),
                  pl.BlockSpec((1, WINDOW, LANES), prev(k_blk)),
                  pl.BlockSpec((1, rows, LANES), cur(k_blk)),
                  pl.BlockSpec((1, WINDOW, LANES), prev(v_blk)),
                  pl.BlockSpec((1, rows, LANES), cur(v_blk))],
        out_specs=pl.BlockSpec((1, rows, SW_Q_WIDTH), lambda bi, n: (bi, n, 0)),
        scratch_shapes=[pltpu.VMEM((SW_HEADS, WINDOW, 2 * WINDOW), F32)],
        compiler_params=_cparams(2),
        name="swa_attention",
    )(table, sinks, bucket, proj3, proj3, proj3, proj3, proj3)


def _to_bf16_kernel(w_ref, o_ref):
    o_ref[...] = w_ref[...].astype(BF16)


def _to_bf16(w, *, tr):
    n, r, c = w.shape
    spec = pl.BlockSpec((1, tr, c), lambda l, i: (l, i, 0))
    return pl.pallas_call(
        _to_bf16_kernel,
        out_shape=jax.ShapeDtypeStruct(w.shape, BF16),
        grid=(n, r // tr),
        in_specs=[spec],
        out_specs=spec,
        compiler_params=_cparams(2),
        name="to_bf16",
    )(w)


def _out_ffn_kernel(*refs, n_parts, final_norm):
    x_ref = refs[0]
    part_refs = refs[1:1 + n_parts]
    wo_refs = refs[1 + n_parts:1 + 2 * n_parts]
    g_ref, wg_ref, wu_ref, wd_ref, fg_ref, o_ref = refs[1 + 2 * n_parts:]
    x = x_ref[...]
    for a_ref, w_ref in zip(part_refs, wo_refs):
        x = x + _dot(a_ref[...], w_ref[...])
    h = _rms(x, g_ref[...]).astype(BF16)
    gate = _dot(h, wg_ref[...])
    up = _dot(h, wu_ref[...])
    act = (gate / (1.0 + jnp.exp(-gate)) * up).astype(BF16)
    y = x + _dot(act, wd_ref[...])
    if final_norm:
        y = _rms(y, fg_ref[...])
    o_ref[...] = y


def _out_ffn(x2, parts, w_out, g, wg, wu, wd, fg, *, layer, tm, final_norm):
    t, d = x2.shape
    hid = wg.shape[2]
    n_parts = len(parts)
    kp = w_out.shape[0] // n_parts
    resident = dict(pipeline_mode=pl.Buffered(1))
    row_tile = lambda width: pl.BlockSpec((tm, width), lambda i: (i, 0))
    const = lambda r, c: pl.BlockSpec((None, r, c), lambda i: (layer, 0, 0), **resident)
    in_specs = [row_tile(d)] + [row_tile(kp) for _ in parts]
    in_specs += [pl.BlockSpec((kp, d), functools.partial(lambda i, c: (c, 0), c=c), **resident)
                 for c in range(n_parts)]
    in_specs += [pl.BlockSpec((1, d), lambda i: (0, 0)), const(d, hid), const(d, hid),
                 const(hid, d), pl.BlockSpec((1, d), lambda i: (0, 0))]
    return pl.pallas_call(
        functools.partial(_out_ffn_kernel, n_parts=n_parts, final_norm=final_norm),
        out_shape=jax.ShapeDtypeStruct((t, d), F32),
        grid=(t // tm,),
        in_specs=in_specs,
        out_specs=row_tile(d),
        compiler_params=_cparams(1),
        name="out_ffn_final" if final_norm else "out_ffn",
    )(x2, *parts, *([w_out] * n_parts), g, wg, wu, wd, fg)


def _mla_proj_kernel(x_ref, pos_ref, freq_ref, g_ref, wd_ref, qg_ref, wq_ref,
                     kg_ref, wk_ref, wv_ref, q_ref, k_ref, v_ref):
    tm = x_ref.shape[0]
    h = _rms(x_ref[...], g_ref[...]).astype(BF16)
    down = _dot(h, wd_ref[...])
    c_q = down[:, :MLA_Q_RANK]
    c_kv = down[:, MLA_Q_RANK:MLA_Q_RANK + MLA_KV_RANK]
    kr = down[:, MLA_Q_RANK + MLA_KV_RANK:]

    half = MLA_ROPE_DIM // 2
    ang = pos_ref[...].astype(F32) * freq_ref[...]
    lane = lax.broadcasted_iota(jnp.int32, (tm, LANES), 1)
    first = lane < half
    second = (lane >= LANES // 2) & (lane < LANES // 2 + half)

    def spread(dense):
        rows8 = jnp.broadcast_to(dense[:, None, :], (tm // 8, 8, LANES)).reshape(tm, LANES)
        y = pltpu.roll(rows8, 0, 1, stride=half, stride_axis=0)
        return y, pltpu.roll(y, LANES // 2, 1)

    cos_1, cos_2 = spread(jnp.cos(ang))
    sin_1, sin_2 = spread(jnp.sin(ang))
    cos_p = jnp.where(first, cos_1, jnp.where(second, cos_2, 1.0))
    sin_p = jnp.where(first, -sin_1, jnp.where(second, sin_2, 0.0))

    def rope(blk):
        return blk * cos_p + pltpu.roll(blk, LANES // 2, 1) * sin_p

    cqn = _rms(c_q, qg_ref[...]).astype(BF16)
    ckvn = _rms(c_kv, kg_ref[...]).astype(BF16)
    k_rope = rope(kr)
    q = _dot(cqn, wq_ref[...])
    kn = _dot(ckvn, wk_ref[...])
    for hd in range(MLA_HEADS):
        sl = slice(hd * LANES, (hd + 1) * LANES)
        q_ref[:, sl] = rope(q[:, sl]).astype(q_ref.dtype)
        k_ref[:, sl] = (kn[:, sl] + k_rope).astype(k_ref.dtype)
    v_ref[...] = _dot(ckvn, wv_ref[...]).astype(v_ref.dtype)


def _mla_proj(x2, pos2, freq, g, wd, qg, wq, kg, wk, wv, *, tm):
    t, d = x2.shape
    full = lambda a: pl.BlockSpec(a.shape, lambda i: (0,) * a.ndim)
    qk_w = MLA_HEADS * LANES
    v_w = MLA_HEADS * MLA_V_DIM
    return pl.pallas_call(
        _mla_proj_kernel,
        out_shape=(jax.ShapeDtypeStruct((t, qk_w), BF16),
                   jax.ShapeDtypeStruct((t, qk_w), BF16),
                   jax.ShapeDtypeStruct((t, v_w), BF16)),
        grid=(t // tm,),
        in_specs=[pl.BlockSpec((tm, d), lambda i: (i, 0)),
                  pl.BlockSpec((tm // 8, LANES), lambda i: (i, 0)),
                  full(freq), full(g), full(wd), full(qg), full(wq),
                  full(kg), full(wk), full(wv)],
        out_specs=(pl.BlockSpec((tm, qk_w), lambda i: (i, 0)),
                   pl.BlockSpec((tm, qk_w), lambda i: (i, 0)),
                   pl.BlockSpec((tm, v_w), lambda i: (i, 0))),
        compiler_params=_cparams(1),
        name="mla_proj",
    )(x2, pos2, freq, g, wd, qg, wq, kg, wk, wv)


def _mla_attn_kernel(q_ref, k_ref, v_ref, o_ref, va_ref, vb_ref, m_ref, acc_ref, *, tq, tk, unroll):
    qi = pl.program_id(2)
    lane_s = lax.broadcasted_iota(jnp.int32, va_ref.shape, 1)

    @pl.when(qi == 0)
    def _():
        v_all = v_ref[0]
        one = jnp.ones_like(v_all)
        va_ref[...] = jnp.where(lane_s < MLA_V_DIM, v_all, one)
        vb_ref[...] = jnp.where(lane_s < MLA_V_DIM, one, v_all)

    r = tq // tk
    lower = (lax.broadcasted_iota(jnp.int32, (tk, tk), 1)
             <= lax.broadcasted_iota(jnp.int32, (tk, tk), 0))
    c2 = (MLA_NOPE_DIM + MLA_ROPE_DIM) ** -0.5 * LOG2E
    vaug = (va_ref, vb_ref)

    m_ref[...] = jnp.full(m_ref.shape, -jnp.inf, F32)
    acc_ref[...] = jnp.zeros(acc_ref.shape, F32)

    def tile(hh, r0, nrows, j, masked):
        rows = pl.ds(r0, nrows)
        q = q_ref[0, rows, hh * LANES:(hh + 1) * LANES]
        start = pl.multiple_of(j * tk, tk)
        kblk = k_ref[0, pl.ds(start, tk), hh * LANES:(hh + 1) * LANES]
        vblk = vaug[hh][pl.ds(start, tk), :]
        s2 = _dot_nt(q, kblk) * c2
        if masked:
            top = jnp.where(lower, s2[:tk], -jnp.inf)
            s2 = top if nrows == tk else jnp.concatenate([top, s2[tk:]], axis=0)
        m = m_ref[hh, rows, :]
        m_new = jnp.maximum(m, jnp.max(s2, axis=-1, keepdims=True))
        alpha = jnp.exp2(m - m_new)
        p = jnp.exp2(s2 - jnp.tile(m_new, (1, tk // LANES)))
        acc_ref[hh, rows, :] = alpha * acc_ref[hh, rows, :] + _dot(p.astype(BF16), vblk)
        m_ref[hh, rows, :] = m_new

    def body(jj, carry):
        for t in range(unroll):
            for hh in range(2):
                tile(hh, 0, tq, jj * unroll + t, False)
        return carry

    lax.fori_loop(0, (qi * r) // unroll, body, 0)
    for u in range(r):
        for hh in range(2):
            tile(hh, u * tk, tq - u * tk, qi * r + u, True)

    lane = lax.broadcasted_iota(jnp.int32, (tq, LANES), 1)
    acc_a = acc_ref[0]
    acc_b = acc_ref[1]
    out_a = acc_a / pltpu.roll(acc_a, MLA_V_DIM, 1)
    out_b = acc_b / pltpu.roll(acc_b, MLA_V_DIM, 1)
    o_ref[0] = jnp.where(lane < MLA_V_DIM, out_a, out_b).astype(o_ref.dtype)


def _mla_attention(q3, k3, v3, *, tq, tk, unroll):
    b, s, _ = q3.shape
    assert (tq // tk) % unroll == 0
    n_pairs = MLA_HEADS // 2
    return pl.pallas_call(
        functools.partial(_mla_attn_kernel, tq=tq, tk=tk, unroll=unroll),
        out_shape=jax.ShapeDtypeStruct((b, s, MLA_HEADS * MLA_V_DIM), BF16),
        grid=(b, n_pairs, s // tq),
        in_specs=[pl.BlockSpec((1, tq, 2 * LANES), lambda bi, p, i: (bi, i, p)),
                  pl.BlockSpec((1, s, 2 * LANES), lambda bi, p, i: (bi, 0, p)),
                  pl.BlockSpec((1, s, LANES), lambda bi, p, i: (bi, 0, p))],
        out_specs=pl.BlockSpec((1, tq, LANES), lambda bi, p, i: (bi, i, p)),
        scratch_shapes=[pltpu.VMEM((s, LANES), BF16), pltpu.VMEM((s, LANES), BF16),
                        pltpu.VMEM((2, tq, LANES), F32), pltpu.VMEM((2, tq, LANES), F32)],
        compiler_params=_cparams(3),
        name="mla_attention",
    )(q3, k3, v3)


def _head_lanes(nope, rope):
    half = MLA_ROPE_DIM // 2
    split = LANES // 2 - half
    zero = jnp.zeros(nope.shape[:-1] + (LANES - MLA_NOPE_DIM - MLA_ROPE_DIM,), nope.dtype)
    return jnp.concatenate([rope[..., :half], nope[..., :split], rope[..., half:], nope[..., split:], zero],
                           axis=-1)


def _mla_weight_layout(w_down, w_uq, w_ukv):
    d = w_down.shape[0]
    w_kr = w_down[:, MLA_Q_RANK + MLA_KV_RANK:]
    wd = jnp.concatenate([w_down[:, :MLA_Q_RANK + MLA_KV_RANK],
                          _head_lanes(jnp.zeros((d, MLA_NOPE_DIM), F32), w_kr)], axis=1)

    wq = w_uq.reshape(MLA_Q_RANK, MLA_HEADS, MLA_NOPE_DIM + MLA_ROPE_DIM)
    wq = _head_lanes(wq[..., :MLA_NOPE_DIM], wq[..., MLA_NOPE_DIM:]).reshape(MLA_Q_RANK, MLA_HEADS * LANES)

    wkv = w_ukv.reshape(MLA_KV_RANK, MLA_HEADS, MLA_NOPE_DIM + MLA_V_DIM)
    wk_nope, wv = wkv[..., :MLA_NOPE_DIM], wkv[..., MLA_NOPE_DIM:]
    wk = _head_lanes(wk_nope, jnp.zeros((MLA_KV_RANK, MLA_HEADS, MLA_ROPE_DIM), F32))
    wk = wk.reshape(MLA_KV_RANK, MLA_HEADS * LANES)
    wv = wv.reshape(MLA_KV_RANK, MLA_HEADS * MLA_V_DIM)
    return wd.astype(BF16), wq.astype(BF16), wk.astype(BF16), wv.astype(BF16)


def _rope_dense_layout(positions):
    half = MLA_ROPE_DIM // 2
    n_blk = LANES // half
    freqs = ROPE_THETA ** (-jnp.arange(half, dtype=F32) / half)
    order = np.array([(-b) % n_blk for b in range(n_blk)])
    pos_dense = jnp.repeat(positions.reshape(-1, n_blk)[:, order], half, axis=1)
    return pos_dense, jnp.tile(freqs, n_blk)[None, :]


def kernel(x, positions, attn_norm, ffn_norm, even_w_in, even_sinks, even_w_out, rel_bias_table, mla_w_down, mla_q_norm, mla_w_uq, mla_kv_norm, mla_w_ukv, mla_w_o, ffn_w_gate, ffn_w_up, ffn_w_down, final_norm):
    b, s, d = x.shape
    t = b * s
    x2 = x.reshape(t, d)
    row = lambda v: v.reshape(1, -1)

    proj = _norm_proj(x2, row(attn_norm[0]), even_w_in[0].astype(BF16), tm=1024)
    proj3 = proj.reshape(b, s, EVEN_IN_WIDTH)
    o_a = _sb_attention(proj3, tq=1024, sub=256)
    o_b = _swa_attention(proj3, rel_bias_table, even_sinks[0].reshape(1, SW_HEADS), qb=4)
    wg, wu, wd_ffn = _to_bf16(ffn_w_gate, tr=256), _to_bf16(ffn_w_up, tr=256), _to_bf16(ffn_w_down, tr=704)
    x2 = _out_ffn(x2, [o_a.reshape(t, SB_WIDTH), o_b.reshape(t, SW_Q_WIDTH)], even_w_out[0].astype(BF16),
                  row(ffn_norm[0]), wg, wu, wd_ffn, row(final_norm), layer=0, tm=512, final_norm=False)

    wd, wq, wk, wv = _mla_weight_layout(mla_w_down[0], mla_w_uq[0], mla_w_ukv[0])
    pos_dense, freq_dense = _rope_dense_layout(positions)
    q, k, v = _mla_proj(x2, pos_dense, freq_dense, row(attn_norm[1]), wd,
                        row(mla_q_norm[0]), wq, row(mla_kv_norm[0]), wk, wv, tm=512)
    o = _mla_attention(q.reshape(b, s, -1), k.reshape(b, s, -1), v.reshape(b, s, -1), tq=2048, tk=512, unroll=2)
    x2 = _out_ffn(x2, [o.reshape(t, MLA_HEADS * MLA_V_DIM)], mla_w_o[0].astype(BF16),
                  row(ffn_norm[1]), wg, wu, wd_ffn, row(final_norm), layer=1, tm=512, final_norm=True)
    return x2.reshape(b, s, d)
```

```python
import functools
import math

import numpy as np
import jax
import jax.numpy as jnp
from jax import lax
from jax.experimental import pallas as pl
from jax.experimental.pallas import tpu as pltpu

F32 = jnp.float32
BF16 = jnp.bfloat16

D_MODEL = 1024
EPS = 1e-6
HEAD_DIM = 64
LANES = 128
SB_HEADS = 8
SW_HEADS = 8
SW_KV_HEADS = 2
WINDOW = 128
SB_WIDTH = SB_HEADS * HEAD_DIM
SW_Q_WIDTH = SW_HEADS * HEAD_DIM
SW_KV_WIDTH = SW_KV_HEADS * HEAD_DIM
EVEN_IN_WIDTH = 3 * SB_WIDTH + SW_Q_WIDTH + 2 * SW_KV_WIDTH
REL_BUCKETS = 32
REL_MAX_DIST = 128
MLA_HEADS = 16
MLA_NOPE_DIM = 64
MLA_ROPE_DIM = 32
MLA_V_DIM = 64
MLA_Q_RANK = 384
MLA_KV_RANK = 256
ROPE_THETA = 10000.0
LOG2E = math.log2(math.e)
SB_SATURATED = 128.0
VMEM_LIMIT_BYTES = 56 * 1024 * 1024


def _cparams(n_axes):
    return pltpu.CompilerParams(
        dimension_semantics=("arbitrary",) * n_axes,
        vmem_limit_bytes=VMEM_LIMIT_BYTES)


def _rms(x, g):
    return x * lax.rsqrt(jnp.mean(x * x, axis=-1, keepdims=True) + EPS) * g


def _dot(a, b):
    return jnp.dot(a, b, preferred_element_type=F32)


def _dot_nt(a, b):
    return lax.dot_general(a, b, (((1,), (1,)), ((), ())), preferred_element_type=F32)


def _norm_proj_kernel(x_ref, g_ref, w_ref, o_ref):
    h = _rms(x_ref[...], g_ref[...]).astype(BF16)
    o_ref[...] = _dot(h, w_ref[...]).astype(o_ref.dtype)


def _norm_proj(x2, g, w, *, tm):
    t, d = x2.shape
    n = w.shape[1]
    return pl.pallas_call(
        _norm_proj_kernel,
        out_shape=jax.ShapeDtypeStruct((t, n), BF16),
        grid=(t // tm,),
        in_specs=[pl.BlockSpec((tm, d), lambda i: (i, 0)),
                  pl.BlockSpec((1, d), lambda i: (0, 0)),
                  pl.BlockSpec((d, n), lambda i: (0, 0))],
        out_specs=pl.BlockSpec((tm, n), lambda i: (i, 0)),
        compiler_params=_cparams(1),
        name="norm_proj",
    )(x2, g, w)


def _softplus(z):
    neg_abs = lax.bitcast_convert_type(
        lax.bitcast_convert_type(z, jnp.uint32) | jnp.uint32(0x80000000), F32)
    return jnp.maximum(z, 0.0) + jnp.log(1.0 + jnp.exp(neg_abs))


def _sb_kernel(q_ref, k_ref, v_ref, o_ref, r_ref, acc_ref, *, tq, sub):
    qi = pl.program_id(2)
    r = tq // sub
    q2 = q_ref[0]
    lane = lax.broadcasted_iota(jnp.int32, (tq, LANES), 1)
    low = lane < HEAD_DIM
    zero = jnp.zeros_like(q2)
    scale = HEAD_DIM ** -0.5
    qm = (jnp.where(low, q2, zero) * scale, jnp.where(low, zero, q2) * scale)

    tri_row = lax.broadcasted_iota(jnp.int32, (sub, sub), 0)
    tri_col = lax.broadcasted_iota(jnp.int32, (sub, sub), 1)
    tri = (tri_row >= tri_col).astype(BF16)
    strictly_lower = tri_col < tri_row

    r_ref[...] = jnp.zeros(r_ref.shape, F32)
    acc_ref[...] = jnp.zeros(acc_ref.shape, F32)

    def tile(hh, r0, nrows, j, masked):
        rows = pl.ds(r0, nrows)
        start = pl.multiple_of(j * sub, sub)
        kblk = k_ref[0, pl.ds(start, sub), :]
        vblk = v_ref[0, pl.ds(start, sub), :]
        z = _dot_nt(qm[hh][r0:r0 + nrows], kblk)
        if masked:
            top = jnp.where(strictly_lower, z[:sub], -jnp.inf)
            z = top if nrows == sub else jnp.concatenate([top, z[sub:]], axis=0)
        sp = _softplus(z)
        later = r_ref[hh, rows, :]
        c = _dot(sp.astype(BF16), tri) + jnp.tile(later, (1, sub // LANES))
        w = jnp.exp(z - c)
        acc_ref[hh, rows, :] += _dot(w.astype(BF16), vblk)
        r_ref[hh, rows, :] = jnp.broadcast_to(c[:, 0:1], (nrows, LANES))

    def unsaturated(first_row):
        return jnp.min(r_ref[:, first_row:, :]) < SB_SATURATED

    band = 2 * sub
    for u in reversed(range(r)):
        for hh in range(2):
            tile(hh, u * sub, min(band, tq - u * sub), qi * r + u, True)

    for u in reversed(range(r)):
        r0 = u * sub + band
        if r0 < tq:
            @pl.when(unsaturated(r0))
            def _():
                for hh in range(2):
                    tile(hh, r0, tq - r0, qi * r + u, False)

    def alive():
        top = jnp.min(r_ref[:, :sub, :]) < SB_SATURATED
        rest = jnp.min(r_ref[:, sub:, :]) < SB_SATURATED
        return top, rest

    n_blocks = qi * r

    def cond(state):
        jj, top, rest = state
        return jnp.logical_and(jj < n_blocks, jnp.logical_or(top, rest))

    def body(state):
        jj, _, rest = state
        j = n_blocks - 1 - jj

        @pl.when(rest)
        def _():
            for hh in range(2):
                tile(hh, 0, tq, j, False)

        @pl.when(jnp.logical_not(rest))
        def _():
            for hh in range(2):
                tile(hh, 0, sub, j, False)

        return (jj + 1,) + alive()

    lax.while_loop(cond, body, (jnp.int32(0),) + alive())
    o_ref[0] = jnp.where(low, acc_ref[0], acc_ref[1]).astype(o_ref.dtype)


def _sb_attention(proj3, *, tq, sub):
    b, s, _ = proj3.shape
    n_pairs = SB_WIDTH // LANES
    k_off = SB_WIDTH // LANES
    v_off = 2 * SB_WIDTH // LANES
    return pl.pallas_call(
        functools.partial(_sb_kernel, tq=tq, sub=sub),
        out_shape=jax.ShapeDtypeStruct((b, s, SB_WIDTH), BF16),
        grid=(b, n_pairs, s // tq),
        in_specs=[pl.BlockSpec((1, tq, LANES), lambda bi, p, i: (bi, i, p)),
                  pl.BlockSpec((1, s, LANES), lambda bi, p, i: (bi, 0, k_off + p)),
                  pl.BlockSpec((1, s, LANES), lambda bi, p, i: (bi, 0, v_off + p))],
        out_specs=pl.BlockSpec((1, tq, LANES), lambda bi, p, i: (bi, i, p)),
        scratch_shapes=[pltpu.VMEM((2, tq, LANES), F32), pltpu.VMEM((2, tq, LANES), F32)],
        compiler_params=_cparams(3),
        name="sb_attention",
    )(proj3, proj3, proj3)


def _t5_bucket_table():
    max_exact = REL_BUCKETS // 2
    rel = WINDOW + np.arange(WINDOW)[:, None] - np.arange(2 * WINDOW)[None, :]
    rel = np.maximum(rel, 0)
    relf = np.maximum(rel, 1).astype(np.float32)
    large = max_exact + (np.log(relf / np.float32(max_exact)) / np.float32(math.log(REL_MAX_DIST / max_exact))
                         * np.float32(REL_BUCKETS - max_exact)).astype(np.int32)
    large = np.minimum(large, REL_BUCKETS - 1)
    return np.where(rel < max_exact, rel, large).astype(np.int32)


def _swap_halves(x):
    return jnp.concatenate([x[:, HEAD_DIM:], x[:, :HEAD_DIM]], axis=1)


def _swa_kernel(table_ref, sinks_ref, bucket_ref, q_ref, kp_ref, kc_ref, vp_ref, vc_ref,
                o_ref, bias_ref, *, qb):
    bi = pl.program_id(0)
    n = pl.program_id(1)
    blk = WINDOW
    t = lax.broadcasted_iota(jnp.int32, (blk, 2 * blk), 0)
    s = lax.broadcasted_iota(jnp.int32, (blk, 2 * blk), 1)

    @pl.when(jnp.logical_and(bi == 0, n == 0))
    def _():
        bucket = bucket_ref[...]
        rel = blk + t - s
        in_window = (rel >= 0) & (rel < WINDOW)
        for h in range(SW_HEADS):
            bias = jnp.zeros((blk, 2 * blk), F32)
            for bk in range(REL_BUCKETS):
                bias = jnp.where(bucket == bk, table_ref[bk, h], bias)
            bias_ref[h] = jnp.where(in_window, bias, -jnp.inf)

    kk = jnp.concatenate([kp_ref[0], kc_ref[0]], axis=0)
    vv = jnp.concatenate([vp_ref[0], vc_ref[0]], axis=0)
    kk_sw = _swap_halves(kk)
    vv_sw = _swap_halves(vv)
    lane = lax.broadcasted_iota(jnp.int32, (blk, LANES), 1)
    low = lane < HEAD_DIM
    scale = HEAD_DIM ** -0.5
    group = SW_HEADS // SW_KV_HEADS
    has_prev = (s >= blk) | (n > 0)

    for j in range(qb):
        rows = slice(j * blk, (j + 1) * blk)
        keys = slice(j * blk, (j + 2) * blk)
        for p in range(SW_HEADS // 2):
            q2 = q_ref[0, rows, p * LANES:(p + 1) * LANES]
            zero = jnp.zeros_like(q2)
            outs = []
            for half in range(2):
                h = 2 * p + half
                kv = h // group
                qh = (jnp.where(low, q2, zero) if half == 0 else jnp.where(low, zero, q2)) * scale
                k_use = (kk if kv == half else kk_sw)[keys]
                v_use = (vv if kv == half else vv_sw)[keys]
                logits = _dot_nt(qh, k_use) + bias_ref[h]
                if j == 0:
                    logits = jnp.where(has_prev, logits, -jnp.inf)
                sink = sinks_ref[0, h]
                m = jnp.maximum(jnp.max(logits, axis=-1, keepdims=True), sink)
                pr = jnp.exp(logits - m)
                denom = jnp.sum(pr, axis=-1, keepdims=True) + jnp.exp(sink - m)
                outs.append(_dot(pr.astype(BF16), v_use) / denom)
            o_ref[0, rows, p * LANES:(p + 1) * LANES] = jnp.where(low, outs[0], outs[1]).astype(o_ref.dtype)


def _swa_attention(proj3, table, sinks, *, qb):
    b, s, _ = proj3.shape
    rows = qb * WINDOW
    q_blk = (3 * SB_WIDTH) // SW_Q_WIDTH
    k_blk = (3 * SB_WIDTH + SW_Q_WIDTH) // LANES
    v_blk = k_blk + 1
    bucket = jnp.asarray(_t5_bucket_table())
    smem = pl.BlockSpec(memory_space=pltpu.SMEM)
    prev = lambda c: (lambda bi, n: (bi, jnp.maximum(n * qb - 1, 0), c))
    cur = lambda c: (lambda bi, n: (bi, n, c))
    return pl.pallas_call(
        functools.partial(_swa_kernel, qb=qb),
        out_shape=jax.ShapeDtypeStruct((b, s, SW_Q_WIDTH), BF16),
        grid=(b, s // rows),
        in_specs=[smem, smem,
                  pl.BlockSpec((WINDOW, 2 * WINDOW), lambda bi, n: (0, 0)),
                  pl.BlockSpec((1, rows, SW_Q_WIDTH), cur(q_blk)),
                  pl.BlockSpec((1, WINDOW, LANES), prev(k_blk)),
                  pl.BlockSpec((1, rows, LANES), cur(k_blk)),
                  pl.BlockSpec((1, WINDOW, LANES), prev(v_blk)),
                  pl.BlockSpec((1, rows, LANES), cur(v_blk))],
        out_specs=pl.BlockSpec((1, rows, SW_Q_WIDTH), lambda bi, n: (bi, n, 0)),
        scratch_shapes=[pltpu.VMEM((SW_HEADS, WINDOW, 2 * WINDOW), F32)],
        compiler_params=_cparams(2),
        name="swa_attention",
    )(table, sinks, bucket, proj3, proj3, proj3, proj3, proj3)


def _to_bf16_kernel(w_ref, o_ref):
    o_ref[...] = w_ref[...].astype(BF16)


def _to_bf16(w, *, tr):
    n, r, c = w.shape
    spec = pl.BlockSpec((1, tr, c), lambda l, i: (l, i, 0))
    return pl.pallas_call(
        _to_bf16_kernel,
        out_shape=jax.ShapeDtypeStruct(w.shape, BF16),
        grid=(n, r // tr),
        in_specs=[spec],
        out_specs=spec,
        compiler_params=_cparams(2),
        name="to_bf16",
    )(w)


def _out_ffn_kernel(*refs, n_parts, final_norm):
    x_ref = refs[0]
    part_refs = refs[1:1 + n_parts]
    wo_refs = refs[1 + n_parts:1 + 2 * n_parts]
    g_ref, wg_ref, wu_ref, wd_ref, fg_ref, o_ref = refs[1 + 2 * n_parts:]
    x = x_ref[...]
    for a_ref, w_ref in zip(part_refs, wo_refs):
        x = x + _dot(a_ref[...], w_ref[...])
    h = _rms(x, g_ref[...]).astype(BF16)
    gate = _dot(h, wg_ref[...])
    up = _dot(h, wu_ref[...])
    act = (gate / (1.0 + jnp.exp(-gate)) * up).astype(BF16)
    y = x + _dot(act, wd_ref[...])
    if final_norm:
        y = _rms(y, fg_ref[...])
    o_ref[...] = y


def _out_ffn(x2, parts, w_out, g, wg, wu, wd, fg, *, layer, tm, final_norm):
    t, d = x2.shape
    hid = wg.shape[2]
    n_parts = len(parts)
    kp = w_out.shape[0] // n_parts
    resident = dict(pipeline_mode=pl.Buffered(1))
    row_tile = lambda width: pl.BlockSpec((tm, width), lambda i: (i, 0))
    const = lambda r, c: pl.BlockSpec((None, r, c), lambda i: (layer, 0, 0), **resident)
    in_specs = [row_tile(d)] + [row_tile(kp) for _ in parts]
    in_specs += [pl.BlockSpec((kp, d), functools.partial(lambda i, c: (c, 0), c=c), **resident)
                 for c in range(n_parts)]
    in_specs += [pl.BlockSpec((1, d), lambda i: (0, 0)), const(d, hid), const(d, hid),
                 const(hid, d), pl.BlockSpec((1, d), lambda i: (0, 0))]
    return pl.pallas_call(
        functools.partial(_out_ffn_kernel, n_parts=n_parts, final_norm=final_norm),
        out_shape=jax.ShapeDtypeStruct((t, d), F32),
        grid=(t // tm,),
        in_specs=in_specs,
        out_specs=row_tile(d),
        compiler_params=_cparams(1),
        name="out_ffn_final" if final_norm else "out_ffn",
    )(x2, *parts, *([w_out] * n_parts), g, wg, wu, wd, fg)


def _mla_proj_kernel(x_ref, pos_ref, freq_ref, g_ref, wd_ref, qg_ref, wq_ref,
                     kg_ref, wk_ref, wv_ref, q_ref, k_ref, v_ref):
    tm = x_ref.shape[0]
    h = _rms(x_ref[...], g_ref[...]).astype(BF16)
    down = _dot(h, wd_ref[...])
    c_q = down[:, :MLA_Q_RANK]
    c_kv = down[:, MLA_Q_RANK:MLA_Q_RANK + MLA_KV_RANK]
    kr = down[:, MLA_Q_RANK + MLA_KV_RANK:]

    half = MLA_ROPE_DIM // 2
    ang = pos_ref[...].astype(F32) * freq_ref[...]
    lane = lax.broadcasted_iota(jnp.int32, (tm, LANES), 1)
    first = lane < half
    second = (lane >= LANES // 2) & (lane < LANES // 2 + half)

    def spread(dense):
        rows8 = jnp.broadcast_to(dense[:, None, :], (tm // 8, 8, LANES)).reshape(tm, LANES)
        y = pltpu.roll(rows8, 0, 1, stride=half, stride_axis=0)
        return y, pltpu.roll(y, LANES // 2, 1)

    cos_1, cos_2 = spread(jnp.cos(ang))
    sin_1, sin_2 = spread(jnp.sin(ang))
    cos_p = jnp.where(first, cos_1, jnp.where(second, cos_2, 1.0))
    sin_p = jnp.where(first, -sin_1, jnp.where(second, sin_2, 0.0))

    def rope(blk):
        return blk * cos_p + pltpu.roll(blk, LANES // 2, 1) * sin_p

    cqn = _rms(c_q, qg_ref[...]).astype(BF16)
    ckvn = _rms(c_kv, kg_ref[...]).astype(BF16)
    k_rope = rope(kr)
    q = _dot(cqn, wq_ref[...])
    kn = _dot(ckvn, wk_ref[...])
    for hd in range(MLA_HEADS):
        sl = slice(hd * LANES, (hd + 1) * LANES)
        q_ref[:, sl] = rope(q[:, sl]).astype(q_ref.dtype)
        k_ref[:, sl] = (kn[:, sl] + k_rope).astype(k_ref.dtype)
    v_ref[...] = _dot(ckvn, wv_ref[...]).astype(v_ref.dtype)


def _mla_proj(x2, pos2, freq, g, wd, qg, wq, kg, wk, wv, *, tm):
    t, d = x2.shape
    full = lambda a: pl.BlockSpec(a.shape, lambda i: (0,) * a.ndim)
    qk_w = MLA_HEADS * LANES
    v_w = MLA_HEADS * MLA_V_DIM
    return pl.pallas_call(
        _mla_proj_kernel,
        out_shape=(jax.ShapeDtypeStruct((t, qk_w), BF16),
                   jax.ShapeDtypeStruct((t, qk_w), BF16),
                   jax.ShapeDtypeStruct((t, v_w), BF16)),
        grid=(t // tm,),
        in_specs=[pl.BlockSpec((tm, d), lambda i: (i, 0)),
                  pl.BlockSpec((tm // 8, LANES), lambda i: (i, 0)),
                  full(freq), full(g), full(wd), full(qg), full(wq),
                  full(kg), full(wk), full(wv)],
        out_specs=(pl.BlockSpec((tm, qk_w), lambda i: (i, 0)),
                   pl.BlockSpec((tm, qk_w), lambda i: (i, 0)),
                   pl.BlockSpec((tm, v_w), lambda i: (i, 0))),
        compiler_params=_cparams(1),
        name="mla_proj",
    )(x2, pos2, freq, g, wd, qg, wq, kg, wk, wv)


def _mla_attn_kernel(q_ref, k_ref, v_ref, o_ref, va_ref, vb_ref, m_ref, acc_ref, *, tq, tk, unroll):
    qi = pl.program_id(2)
    lane_s = lax.broadcasted_iota(jnp.int32, va_ref.shape, 1)

    @pl.when(qi == 0)
    def _():
        v_all = v_ref[0]
        one = jnp.ones_like(v_all)
        va_ref[...] = jnp.where(lane_s < MLA_V_DIM, v_all, one)
        vb_ref[...] = jnp.where(lane_s < MLA_V_DIM, one, v_all)

    r = tq // tk
    lower = (lax.broadcasted_iota(jnp.int32, (tk, tk), 1)
             <= lax.broadcasted_iota(jnp.int32, (tk, tk), 0))
    c2 = (MLA_NOPE_DIM + MLA_ROPE_DIM) ** -0.5 * LOG2E
    vaug = (va_ref, vb_ref)

    m_ref[...] = jnp.full(m_ref.shape, -jnp.inf, F32)
    acc_ref[...] = jnp.zeros(acc_ref.shape, F32)

    def tile(hh, r0, nrows, j, masked):
        rows = pl.ds(r0, nrows)
        q = q_ref[0, rows, hh * LANES:(hh + 1) * LANES]
        start = pl.multiple_of(j * tk, tk)
        kblk = k_ref[0, pl.ds(start, tk), hh * LANES:(hh + 1) * LANES]
        vblk = vaug[hh][pl.ds(start, tk), :]
        s2 = _dot_nt(q, kblk) * c2
        if masked:
            top = jnp.where(lower, s2[:tk], -jnp.inf)
            s2 = top if nrows == tk else jnp.concatenate([top, s2[tk:]], axis=0)
        m = m_ref[hh, rows, :]
        m_new = jnp.maximum(m, jnp.max(s2, axis=-1, keepdims=True))
        alpha = jnp.exp2(m - m_new)
        p = jnp.exp2(s2 - jnp.tile(m_new, (1, tk // LANES)))
        acc_ref[hh, rows, :] = alpha * acc_ref[hh, rows, :] + _dot(p.astype(BF16), vblk)
        m_ref[hh, rows, :] = m_new

    def body(jj, carry):
        for t in range(unroll):
            for hh in range(2):
                tile(hh, 0, tq, jj * unroll + t, False)
        return carry

    lax.fori_loop(0, (qi * r) // unroll, body, 0)
    for u in range(r):
        for hh in range(2):
            tile(hh, u * tk, tq - u * tk, qi * r + u, True)

    lane = lax.broadcasted_iota(jnp.int32, (tq, LANES), 1)
    acc_a = acc_ref[0]
    acc_b = acc_ref[1]
    out_a = acc_a / pltpu.roll(acc_a, MLA_V_DIM, 1)
    out_b = acc_b / pltpu.roll(acc_b, MLA_V_DIM, 1)
    o_ref[0] = jnp.where(lane < MLA_V_DIM, out_a, out_b).astype(o_ref.dtype)


def _mla_attention(q3, k3, v3, *, tq, tk, unroll):
    b, s, _ = q3.shape
    assert (tq // tk) % unroll == 0
    n_pairs = MLA_HEADS // 2
    return pl.pallas_call(
        functools.partial(_mla_attn_kernel, tq=tq, tk=tk, unroll=unroll),
        out_shape=jax.ShapeDtypeStruct((b, s, MLA_HEADS * MLA_V_DIM), BF16),
        grid=(b, n_pairs, s // tq),
        in_specs=[pl.BlockSpec((1, tq, 2 * LANES), lambda bi, p, i: (bi, i, p)),
                  pl.BlockSpec((1, s, 2 * LANES), lambda bi, p, i: (bi, 0, p)),
                  pl.BlockSpec((1, s, LANES), lambda bi, p, i: (bi, 0, p))],
        out_specs=pl.BlockSpec((1, tq, LANES), lambda bi, p, i: (bi, i, p)),
        scratch_shapes=[pltpu.VMEM((s, LANES), BF16), pltpu.VMEM((s, LANES), BF16),
                        pltpu.VMEM((2, tq, LANES), F32), pltpu.VMEM((2, tq, LANES), F32)],
        compiler_params=_cparams(3),
        name="mla_attention",
    )(q3, k3, v3)


def _head_lanes(nope, rope):
    half = MLA_ROPE_DIM // 2
    split = LANES // 2 - half
    zero = jnp.zeros(nope.shape[:-1] + (LANES - MLA_NOPE_DIM - MLA_ROPE_DIM,), nope.dtype)
    return jnp.concatenate([rope[..., :half], nope[..., :split], rope[..., half:], nope[..., split:], zero],
                           axis=-1)


def _mla_weight_layout(w_down, w_uq, w_ukv):
    d = w_down.shape[0]
    w_kr = w_down[:, MLA_Q_RANK + MLA_KV_RANK:]
    wd = jnp.concatenate([w_down[:, :MLA_Q_RANK + MLA_KV_RANK],
                          _head_lanes(jnp.zeros((d, MLA_NOPE_DIM), F32), w_kr)], axis=1)

    wq = w_uq.reshape(MLA_Q_RANK, MLA_HEADS, MLA_NOPE_DIM + MLA_ROPE_DIM)
    wq = _head_lanes(wq[..., :MLA_NOPE_DIM], wq[..., MLA_NOPE_DIM:]).reshape(MLA_Q_RANK, MLA_HEADS * LANES)

    wkv = w_ukv.reshape(MLA_KV_RANK, MLA_HEADS, MLA_NOPE_DIM + MLA_V_DIM)
    wk_nope, wv = wkv[..., :MLA_NOPE_DIM], wkv[..., MLA_NOPE_DIM:]
    wk = _head_lanes(wk_nope, jnp.zeros((MLA_KV_RANK, MLA_HEADS, MLA_ROPE_DIM), F32))
    wk = wk.reshape(MLA_KV_RANK, MLA_HEADS * LANES)
    wv = wv.reshape(MLA_KV_RANK, MLA_HEADS * MLA_V_DIM)
    return wd.astype(BF16), wq.astype(BF16), wk.astype(BF16), wv.astype(BF16)


def _rope_dense_layout(positions):
    half = MLA_ROPE_DIM // 2
    n_blk = LANES // half
    freqs = ROPE_THETA ** (-jnp.arange(half, dtype=F32) / half)
    order = np.array([(-b) % n_blk for b in range(n_blk)])
    pos_dense = jnp.repeat(positions.reshape(-1, n_blk)[:, order], half, axis=1)
    return pos_dense, jnp.tile(freqs, n_blk)[None, :]


def kernel(x, positions, attn_norm, ffn_norm, even_w_in, even_sinks, even_w_out, rel_bias_table, mla_w_down, mla_q_norm, mla_w_uq, mla_kv_norm, mla_w_ukv, mla_w_o, ffn_w_gate, ffn_w_up, ffn_w_down, final_norm):
    b, s, d = x.shape
    t = b * s
    x2 = x.reshape(t, d)
    row = lambda v: v.reshape(1, -1)

    proj = _norm_proj(x2, row(attn_norm[0]), even_w_in[0].astype(BF16), tm=1024)
    proj3 = proj.reshape(b, s, EVEN_IN_WIDTH)
    o_a = _sb_attention(proj3, tq=1024, sub=256)
    o_b = _swa_attention(proj3, rel_bias_table, even_sinks[0].reshape(1, SW_HEADS), qb=4)
    wg, wu, wd_ffn = _to_bf16(ffn_w_gate, tr=256), _to_bf16(ffn_w_up, tr=256), _to_bf16(ffn_w_down, tr=704)
    x2 = _out_ffn(x2, [o_a.reshape(t, SB_WIDTH), o_b.reshape(t, SW_Q_WIDTH)], even_w_out[0].astype(BF16),
                  row(ffn_norm[0]), wg, wu, wd_ffn, row(final_norm), layer=0, tm=512, final_norm=False)

    wd, wq, wk, wv = _mla_weight_layout(mla_w_down[0], mla_w_uq[0], mla_w_ukv[0])
    pos_dense, freq_dense = _rope_dense_layout(positions)
    q, k, v = _mla_proj(x2, pos_dense, freq_dense, row(attn_norm[1]), wd,
                        row(mla_q_norm[0]), wq, row(mla_kv_norm[0]), wk, wv, tm=512)
    o = _mla_attention(q.reshape(b, s, -1), k.reshape(b, s, -1), v.reshape(b, s, -1), tq=2048, tk=512, unroll=2)
    x2 = _out_ffn(x2, [o.reshape(t, MLA_HEADS * MLA_V_DIM)], mla_w_o[0].astype(BF16),
                  row(ffn_norm[1]), wg, wu, wd_ffn, row(final_norm), layer=1, tm=512, final_norm=True)
    return x2.reshape(b, s, d)
```

```python
import functools
import math

import numpy as np
import jax
import jax.numpy as jnp
from jax import lax
from jax.experimental import pallas as pl
from jax.experimental.pallas import tpu as pltpu

F32 = jnp.float32
BF16 = jnp.bfloat16

D_MODEL = 1024
EPS = 1e-6
HEAD_DIM = 64
LANES = 128
SB_HEADS = 8
SW_HEADS = 8
SW_KV_HEADS = 2
WINDOW = 128
SB_WIDTH = SB_HEADS * HEAD_DIM
SW_Q_WIDTH = SW_HEADS * HEAD_DIM
SW_KV_WIDTH = SW_KV_HEADS * HEAD_DIM
EVEN_IN_WIDTH = 3 * SB_WIDTH + SW_Q_WIDTH + 2 * SW_KV_WIDTH
REL_BUCKETS = 32
REL_MAX_DIST = 128
MLA_HEADS = 16
MLA_NOPE_DIM = 64
MLA_ROPE_DIM = 32
MLA_V_DIM = 64
MLA_Q_RANK = 384
MLA_KV_RANK = 256
ROPE_THETA = 10000.0
LOG2E = math.log2(math.e)
SB_SATURATED = 128.0
VMEM_LIMIT_BYTES = 56 * 1024 * 1024


def _cparams(n_axes):
    return pltpu.CompilerParams(
        dimension_semantics=("arbitrary",) * n_axes,
        vmem_limit_bytes=VMEM_LIMIT_BYTES)


def _rms(x, g):
    return x * lax.rsqrt(jnp.mean(x * x, axis=-1, keepdims=True) + EPS) * g


def _dot(a, b):
    return jnp.dot(a, b, preferred_element_type=F32)


def _dot_nt(a, b):
    return lax.dot_general(a, b, (((1,), (1,)), ((), ())), preferred_element_type=F32)


def _norm_proj_kernel(x_ref, g_ref, w_ref, o_ref):
    h = _rms(x_ref[...], g_ref[...]).astype(BF16)
    o_ref[...] = _dot(h, w_ref[...]).astype(o_ref.dtype)


def _norm_proj(x2, g, w, *, tm):
    t, d = x2.shape
    n = w.shape[1]
    return pl.pallas_call(
        _norm_proj_kernel,
        out_shape=jax.ShapeDtypeStruct((t, n), BF16),
        grid=(t // tm,),
        in_specs=[pl.BlockSpec((tm, d), lambda i: (i, 0)),
                  pl.BlockSpec((1, d), lambda i: (0, 0)),
                  pl.BlockSpec((d, n), lambda i: (0, 0))],
        out_specs=pl.BlockSpec((tm, n), lambda i: (i, 0)),
        compiler_params=_cparams(1),
        name="norm_proj",
    )(x2, g, w)


def _softplus(z):
    return jnp.maximum(z, 0.0) + jnp.log(1.0 + jnp.exp2(jnp.abs(z) * (-LOG2E)))


def _sb_kernel(q_ref, k_ref, v_ref, o_ref, r_ref, acc_ref, *, tq, sub):
    qi = pl.program_id(2)
    r = tq // sub
    q2 = q_ref[0]
    lane = lax.broadcasted_iota(jnp.int32, (tq, LANES), 1)
    low = lane < HEAD_DIM
    zero = jnp.zeros_like(q2)
    scale = HEAD_DIM ** -0.5
    qm = (jnp.where(low, q2, zero) * scale, jnp.where(low, zero, q2) * scale)

    tri_row = lax.broadcasted_iota(jnp.int32, (sub, sub), 0)
    tri_col = lax.broadcasted_iota(jnp.int32, (sub, sub), 1)
    strictly_lower = tri_col < tri_row
    tri = strictly_lower.astype(BF16)

    r_ref[...] = jnp.zeros(r_ref.shape, F32)
    acc_ref[...] = jnp.zeros(acc_ref.shape, F32)

    def tile(hh, r0, nrows, j, masked, enabled=None):
        rows = pl.ds(r0, nrows)
        start = pl.multiple_of(j * sub, sub)
        kblk = k_ref[0, pl.ds(start, sub), :]
        vblk = v_ref[0, pl.ds(start, sub), :]
        z = _dot_nt(qm[hh][r0:r0 + nrows], kblk)
        if enabled is not None:
            z = jnp.where(enabled, z, -jnp.inf)
        if masked:
            top = jnp.where(strictly_lower, z[:sub], -jnp.inf)
            z = top if nrows == sub else jnp.concatenate([top, z[sub:]], axis=0)
        sp = _softplus(z)
        log_beta = z - sp
        later = r_ref[hh, rows, :]
        e = _dot(sp.astype(BF16), tri) + jnp.tile(later, (1, sub // LANES))
        w = jnp.exp(log_beta - e)
        acc_ref[hh, rows, :] += _dot(w.astype(BF16), vblk)
        r_ref[hh, rows, :] = jnp.broadcast_to(e[:, 0:1] + sp[:, 0:1], (nrows, LANES))

    def unsaturated(first_row):
        return jnp.min(r_ref[:, first_row:, :]) < SB_SATURATED

    band = 2 * sub
    n_blocks = qi * r
    for u in reversed(range(r)):
        for hh in range(2):
            tile(hh, u * sub, min(band, tq - u * sub), n_blocks + u, True)
    for hh in range(2):
        tile(hh, 0, sub, jnp.maximum(n_blocks - 1, 0), False, enabled=qi > 0)

    @pl.when(unsaturated(0))
    def _():
        for u in reversed(range(r)):
            r0 = u * sub + band
            if r0 < tq:
                @pl.when(unsaturated(r0))
                def _():
                    for hh in range(2):
                        tile(hh, r0, tq - r0, n_blocks + u, False)

        @pl.when(jnp.logical_and(qi > 0, unsaturated(sub)))
        def _():
            for hh in range(2):
                tile(hh, sub, tq - sub, n_blocks - 1, False)

        def alive():
            top = jnp.min(r_ref[:, :sub, :]) < SB_SATURATED
            rest = jnp.min(r_ref[:, sub:, :]) < SB_SATURATED
            return top, rest

        def cond(state):
            jj, top, rest = state
            return jnp.logical_and(jj < n_blocks, jnp.logical_or(top, rest))

        def body(state):
            jj, _, rest = state
            j = n_blocks - 1 - jj

            @pl.when(rest)
            def _():
                for hh in range(2):
                    tile(hh, 0, tq, j, False)

            @pl.when(jnp.logical_not(rest))
            def _():
                for hh in range(2):
                    tile(hh, 0, sub, j, False)

            return (jj + 1,) + alive()

        lax.while_loop(cond, body, (jnp.int32(1),) + alive())

    o_ref[0] = jnp.where(low, acc_ref[0], acc_ref[1]).astype(o_ref.dtype)


def _sb_attention(proj3, *, tq, sub):
    b, s, _ = proj3.shape
    n_pairs = SB_WIDTH // LANES
    k_off = SB_WIDTH // LANES
    v_off = 2 * SB_WIDTH // LANES
    return pl.pallas_call(
        functools.partial(_sb_kernel, tq=tq, sub=sub),
        out_shape=jax.ShapeDtypeStruct((b, s, SB_WIDTH), BF16),
        grid=(b, n_pairs, s // tq),
        in_specs=[pl.BlockSpec((1, tq, LANES), lambda bi, p, i: (bi, i, p)),
                  pl.BlockSpec((1, s, LANES), lambda bi, p, i: (bi, 0, k_off + p)),
                  pl.BlockSpec((1, s, LANES), lambda bi, p, i: (bi, 0, v_off + p))],
        out_specs=pl.BlockSpec((1, tq, LANES), lambda bi, p, i: (bi, i, p)),
        scratch_shapes=[pltpu.VMEM((2, tq, LANES), F32), pltpu.VMEM((2, tq, LANES), F32)],
        compiler_params=_cparams(3),
        name="sb_attention",
    )(proj3, proj3, proj3)


def _t5_bucket_table():
    max_exact = REL_BUCKETS // 2
    rel = WINDOW + np.arange(WINDOW)[:, None] - np.arange(2 * WINDOW)[None, :]
    rel = np.maximum(rel, 0)
    relf = np.maximum(rel, 1).astype(np.float32)
    large = max_exact + (np.log(relf / np.float32(max_exact)) / np.float32(math.log(REL_MAX_DIST / max_exact))
                         * np.float32(REL_BUCKETS - max_exact)).astype(np.int32)
    large = np.minimum(large, REL_BUCKETS - 1)
    return np.where(rel < max_exact, rel, large).astype(np.int32)


def _swap_halves(x):
    return jnp.concatenate([x[:, HEAD_DIM:], x[:, :HEAD_DIM]], axis=1)


def _swa_kernel(table_ref, sinks_ref, bucket_ref, q_ref, kp_ref, kc_ref, vp_ref, vc_ref,
                o_ref, bias_ref, *, qb):
    bi = pl.program_id(0)
    n = pl.program_id(1)
    blk = WINDOW
    t = lax.broadcasted_iota(jnp.int32, (blk, 2 * blk), 0)
    s = lax.broadcasted_iota(jnp.int32, (blk, 2 * blk), 1)

    @pl.when(jnp.logical_and(bi == 0, n == 0))
    def _():
        bucket = bucket_ref[...]
        rel = blk + t - s
        in_window = (rel >= 0) & (rel < WINDOW)
        for h in range(SW_HEADS):
            bias = jnp.zeros((blk, 2 * blk), F32)
            for bk in range(REL_BUCKETS):
                bias = jnp.where(bucket == bk, table_ref[bk, h], bias)
            bias_ref[h] = jnp.where(in_window, bias, -jnp.inf)

    kk = jnp.concatenate([kp_ref[0], kc_ref[0]], axis=0)
    vv = jnp.concatenate([vp_ref[0], vc_ref[0]], axis=0)
    kk_sw = _swap_halves(kk)
    vv_sw = _swap_halves(vv)
    lane = lax.broadcasted_iota(jnp.int32, (blk, LANES), 1)
    low = lane < HEAD_DIM
    scale = HEAD_DIM ** -0.5
    group = SW_HEADS // SW_KV_HEADS
    has_prev = (s >= blk) | (n > 0)

    for j in range(qb):
        rows = slice(j * blk, (j + 1) * blk)
        keys = slice(j * blk, (j + 2) * blk)
        for p in range(SW_HEADS // 2):
            q2 = q_ref[0, rows, p * LANES:(p + 1) * LANES]
            zero = jnp.zeros_like(q2)
            outs = []
            for half in range(2):
                h = 2 * p + half
                kv = h // group
                qh = (jnp.where(low, q2, zero) if half == 0 else jnp.where(low, zero, q2)) * scale
                k_use = (kk if kv == half else kk_sw)[keys]
                v_use = (vv if kv == half else vv_sw)[keys]
                logits = _dot_nt(qh, k_use) + bias_ref[h]
                if j == 0:
                    logits = jnp.where(has_prev, logits, -jnp.inf)
                sink = sinks_ref[0, h]
                m = jnp.maximum(jnp.max(logits, axis=-1, keepdims=True), sink)
                pr = jnp.exp(logits - m)
                denom = jnp.sum(pr, axis=-1, keepdims=True) + jnp.exp(sink - m)
                outs.append(_dot(pr.astype(BF16), v_use) / denom)
            o_ref[0, rows, p * LANES:(p + 1) * LANES] = jnp.where(low, outs[0], outs[1]).astype(o_ref.dtype)


def _swa_attention(proj3, table, sinks, *, qb):
    b, s, _ = proj3.shape
    rows = qb * WINDOW
    q_blk = (3 * SB_WIDTH) // SW_Q_WIDTH
    k_blk = (3 * SB_WIDTH + SW_Q_WIDTH) // LANES
    v_blk = k_blk + 1
    bucket = jnp.asarray(_t5_bucket_table())
    smem = pl.BlockSpec(memory_space=pltpu.SMEM)
    prev = lambda c: (lambda bi, n: (bi, jnp.maximum(n * qb - 1, 0), c))
    cur = lambda c: (lambda bi, n: (bi, n, c))
    return pl.pallas_call(
        functools.partial(_swa_kernel, qb=qb),
        out_shape=jax.ShapeDtypeStruct((b, s, SW_Q_WIDTH), BF16),
        grid=(b, s // rows),
        in_specs=[smem, smem,
                  pl.BlockSpec((WINDOW, 2 * WINDOW), lambda bi, n: (0, 0)),
                  pl.BlockSpec((1, rows, SW_Q_WIDTH), cur(q_blk)),
                  pl.BlockSpec((1, WINDOW, LANES), prev(k_blk)),
                  pl.BlockSpec((1, rows, LANES), cur(k_blk)),
                  pl.BlockSpec((1, WINDOW, LANES), prev(v_blk)),
                  pl.BlockSpec((1, rows, LANES), cur(v_blk))],
        out_specs=pl.BlockSpec((1, rows, SW_Q_WIDTH), lambda bi, n: (bi, n, 0)),
        scratch_shapes=[pltpu.VMEM((SW_HEADS, WINDOW, 2 * WINDOW), F32)],
        compiler_params=_cparams(2),
        name="swa_attention",
    )(table, sinks, bucket, proj3, proj3, proj3, proj3, proj3)


def _to_bf16_kernel(w_ref, o_ref):
    o_ref[...] = w_ref[...].astype(BF16)


def _to_bf16(w, *, tr):
    n, r, c = w.shape
    spec = pl.BlockSpec((1, tr, c), lambda l, i: (l, i, 0))
    return pl.pallas_call(
        _to_bf16_kernel,
        out_shape=jax.ShapeDtypeStruct(w.shape, BF16),
        grid=(n, r // tr),
        in_specs=[spec],
        out_specs=spec,
        compiler_params=_cparams(2),
        name="to_bf16",
    )(w)


def _out_ffn_kernel(*refs, n_parts, final_norm):
    x_ref = refs[0]
    part_refs = refs[1:1 + n_parts]
    wo_refs = refs[1 + n_parts:1 + 2 * n_parts]
    g_ref, wg_ref, wu_ref, wd_ref, fg_ref, o_ref = refs[1 + 2 * n_parts:]
    x = x_ref[...]
    for a_ref, w_ref in zip(part_refs, wo_refs):
        x = x + _dot(a_ref[...], w_ref[...])
    h = _rms(x, g_ref[...]).astype(BF16)
    gate = _dot(h, wg_ref[...])
    up = _dot(h, wu_ref[...])
    act = (gate / (1.0 + jnp.exp(-gate)) * up).astype(BF16)
    y = x + _dot(act, wd_ref[...])
    if final_norm:
        y = _rms(y, fg_ref[...])
    o_ref[...] = y


def _out_ffn(x2, parts, w_out, g, wg, wu, wd, fg, *, layer, tm, final_norm):
    t, d = x2.shape
    hid = wg.shape[2]
    n_parts = len(parts)
    kp = w_out.shape[0] // n_parts
    resident = dict(pipeline_mode=pl.Buffered(1))
    row_tile = lambda width: pl.BlockSpec((tm, width), lambda i: (i, 0))
    const = lambda r, c: pl.BlockSpec((None, r, c), lambda i: (layer, 0, 0), **resident)
    in_specs = [row_tile(d)] + [row_tile(kp) for _ in parts]
    in_specs += [pl.BlockSpec((kp, d), functools.partial(lambda i, c: (c, 0), c=c), **resident)
                 for c in range(n_parts)]
    in_specs += [pl.BlockSpec((1, d), lambda i: (0, 0)), const(d, hid), const(d, hid),
                 const(hid, d), pl.BlockSpec((1, d), lambda i: (0, 0))]
    return pl.pallas_call(
        functools.partial(_out_ffn_kernel, n_parts=n_parts, final_norm=final_norm),
        out_shape=jax.ShapeDtypeStruct((t, d), F32),
        grid=(t // tm,),
        in_specs=in_specs,
        out_specs=row_tile(d),
        compiler_params=_cparams(1),
        name="out_ffn_final" if final_norm else "out_ffn",
    )(x2, *parts, *([w_out] * n_parts), g, wg, wu, wd, fg)


def _mla_proj_kernel(x_ref, pos_ref, freq_ref, g_ref, wd_ref, qg_ref, wq_ref,
                     kg_ref, wk_ref, wv_ref, q_ref, k_ref, v_ref):
    tm = x_ref.shape[0]
    h = _rms(x_ref[...], g_ref[...]).astype(BF16)
    down = _dot(h, wd_ref[...])
    c_q = down[:, :MLA_Q_RANK]
    c_kv = down[:, MLA_Q_RANK:MLA_Q_RANK + MLA_KV_RANK]
    kr = down[:, MLA_Q_RANK + MLA_KV_RANK:]

    half = MLA_ROPE_DIM // 2
    ang = pos_ref[...].astype(F32) * freq_ref[...]
    lane = lax.broadcasted_iota(jnp.int32, (tm, LANES), 1)
    first = lane < half
    second = (lane >= LANES // 2) & (lane < LANES // 2 + half)

    def spread(dense):
        rows8 = jnp.broadcast_to(dense[:, None, :], (tm // 8, 8, LANES)).reshape(tm, LANES)
        y = pltpu.roll(rows8, 0, 1, stride=half, stride_axis=0)
        return y, pltpu.roll(y, LANES // 2, 1)

    cos_1, cos_2 = spread(jnp.cos(ang))
    sin_1, sin_2 = spread(jnp.sin(ang))
    cos_p = jnp.where(first, cos_1, jnp.where(second, cos_2, 1.0))
    sin_p = jnp.where(first, -sin_1, jnp.where(second, sin_2, 0.0))

    def rope(blk):
        return blk * cos_p + pltpu.roll(blk, LANES // 2, 1) * sin_p

    cqn = _rms(c_q, qg_ref[...]).astype(BF16)
    ckvn = _rms(c_kv, kg_ref[...]).astype(BF16)
    k_rope = rope(kr)
    q = _dot(cqn, wq_ref[...])
    kn = _dot(ckvn, wk_ref[...])
    for hd in range(MLA_HEADS):
        sl = slice(hd * LANES, (hd + 1) * LANES)
        q_ref[:, sl] = rope(q[:, sl]).astype(q_ref.dtype)
        k_ref[:, sl] = (kn[:, sl] + k_rope).astype(k_ref.dtype)
    v_ref[...] = _dot(ckvn, wv_ref[...]).astype(v_ref.dtype)


def _mla_proj(x2, pos2, freq, g, wd, qg, wq, kg, wk, wv, *, tm):
    t, d = x2.shape
    full = lambda a: pl.BlockSpec(a.shape, lambda i: (0,) * a.ndim)
    qk_w = MLA_HEADS * LANES
    v_w = MLA_HEADS * MLA_V_DIM
    return pl.pallas_call(
        _mla_proj_kernel,
        out_shape=(jax.ShapeDtypeStruct((t, qk_w), BF16),
                   jax.ShapeDtypeStruct((t, qk_w), BF16),
                   jax.ShapeDtypeStruct((t, v_w), BF16)),
        grid=(t // tm,),
        in_specs=[pl.BlockSpec((tm, d), lambda i: (i, 0)),
                  pl.BlockSpec((tm // 8, LANES), lambda i: (i, 0)),
                  full(freq), full(g), full(wd), full(qg), full(wq),
                  full(kg), full(wk), full(wv)],
        out_specs=(pl.BlockSpec((tm, qk_w), lambda i: (i, 0)),
                   pl.BlockSpec((tm, qk_w), lambda i: (i, 0)),
                   pl.BlockSpec((tm, v_w), lambda i: (i, 0))),
        compiler_params=_cparams(1),
        name="mla_proj",
    )(x2, pos2, freq, g, wd, qg, wq, kg, wk, wv)


def _mla_attn_kernel(q_ref, k_ref, v_ref, o_ref, va_ref, vb_ref, m_ref, acc_ref, *, tq, tk, unroll):
    qi = pl.program_id(2)
    lane_s = lax.broadcasted_iota(jnp.int32, va_ref.shape, 1)

    @pl.when(qi == 0)
    def _():
        v_all = v_ref[0]
        one = jnp.ones_like(v_all)
        va_ref[...] = jnp.where(lane_s < MLA_V_DIM, v_all, one)
        vb_ref[...] = jnp.where(lane_s < MLA_V_DIM, one, v_all)

    r = tq // tk
    lower = (lax.broadcasted_iota(jnp.int32, (tk, tk), 1)
             <= lax.broadcasted_iota(jnp.int32, (tk, tk), 0))
    c2 = (MLA_NOPE_DIM + MLA_ROPE_DIM) ** -0.5 * LOG2E
    vaug = (va_ref, vb_ref)

    m_ref[...] = jnp.full(m_ref.shape, -jnp.inf, F32)
    acc_ref[...] = jnp.zeros(acc_ref.shape, F32)

    def tile(hh, r0, nrows, j, masked):
        rows = pl.ds(r0, nrows)
        q = q_ref[0, rows, hh * LANES:(hh + 1) * LANES]
        start = pl.multiple_of(j * tk, tk)
        kblk = k_ref[0, pl.ds(start, tk), hh * LANES:(hh + 1) * LANES]
        vblk = vaug[hh][pl.ds(start, tk), :]
        s2 = _dot_nt(q, kblk) * c2
        if masked:
            top = jnp.where(lower, s2[:tk], -jnp.inf)
            s2 = top if nrows == tk else jnp.concatenate([top, s2[tk:]], axis=0)
        m = m_ref[hh, rows, :]
        m_new = jnp.maximum(m, jnp.max(s2, axis=-1, keepdims=True))
        alpha = jnp.exp2(m - m_new)
        p = jnp.exp2(s2 - jnp.tile(m_new, (1, tk // LANES)))
        acc_ref[hh, rows, :] = alpha * acc_ref[hh, rows, :] + _dot(p.astype(BF16), vblk)
        m_ref[hh, rows, :] = m_new

    def body(jj, carry):
        for t in range(unroll):
            for hh in range(2):
                tile(hh, 0, tq, jj * unroll + t, False)
        return carry

    lax.fori_loop(0, (qi * r) // unroll, body, 0)
    for u in range(r):
        for hh in range(2):
            tile(hh, u * tk, tq - u * tk, qi * r + u, True)

    lane = lax.broadcasted_iota(jnp.int32, (tq, LANES), 1)
    acc_a = acc_ref[0]
    acc_b = acc_ref[1]
    out_a = acc_a / pltpu.roll(acc_a, MLA_V_DIM, 1)
    out_b = acc_b / pltpu.roll(acc_b, MLA_V_DIM, 1)
    o_ref[0] = jnp.where(lane < MLA_V_DIM, out_a, out_b).astype(o_ref.dtype)


def _mla_attention(q3, k3, v3, *, tq, tk, unroll):
    b, s, _ = q3.shape
    assert (tq // tk) % unroll == 0
    n_pairs = MLA_HEADS // 2
    return pl.pallas_call(
        functools.partial(_mla_attn_kernel, tq=tq, tk=tk, unroll=unroll),
        out_shape=jax.ShapeDtypeStruct((b, s, MLA_HEADS * MLA_V_DIM), BF16),
        grid=(b, n_pairs, s // tq),
        in_specs=[pl.BlockSpec((1, tq, 2 * LANES), lambda bi, p, i: (bi, i, p)),
                  pl.BlockSpec((1, s, 2 * LANES), lambda bi, p, i: (bi, 0, p)),
                  pl.BlockSpec((1, s, LANES), lambda bi, p, i: (bi, 0, p))],
        out_specs=pl.BlockSpec((1, tq, LANES), lambda bi, p, i: (bi, i, p)),
        scratch_shapes=[pltpu.VMEM((s, LANES), BF16), pltpu.VMEM((s, LANES), BF16),
                        pltpu.VMEM((2, tq, LANES), F32), pltpu.VMEM((2, tq, LANES), F32)],
        compiler_params=_cparams(3),
        name="mla_attention",
    )(q3, k3, v3)


def _head_lanes(nope, rope):
    half = MLA_ROPE_DIM // 2
    split = LANES // 2 - half
    zero = jnp.zeros(nope.shape[:-1] + (LANES - MLA_NOPE_DIM - MLA_ROPE_DIM,), nope.dtype)
    return jnp.concatenate([rope[..., :half], nope[..., :split], rope[..., half:], nope[..., split:], zero],
                           axis=-1)


def _mla_weight_layout(w_down, w_uq, w_ukv):
    d = w_down.shape[0]
    w_kr = w_down[:, MLA_Q_RANK + MLA_KV_RANK:]
    wd = jnp.concatenate([w_down[:, :MLA_Q_RANK + MLA_KV_RANK],
                          _head_lanes(jnp.zeros((d, MLA_NOPE_DIM), F32), w_kr)], axis=1)

    wq = w_uq.reshape(MLA_Q_RANK, MLA_HEADS, MLA_NOPE_DIM + MLA_ROPE_DIM)
    wq = _head_lanes(wq[..., :MLA_NOPE_DIM], wq[..., MLA_NOPE_DIM:]).reshape(MLA_Q_RANK, MLA_HEADS * LANES)

    wkv = w_ukv.reshape(MLA_KV_RANK, MLA_HEADS, MLA_NOPE_DIM + MLA_V_DIM)
    wk_nope, wv = wkv[..., :MLA_NOPE_DIM], wkv[..., MLA_NOPE_DIM:]
    wk = _head_lanes(wk_nope, jnp.zeros((MLA_KV_RANK, MLA_HEADS, MLA_ROPE_DIM), F32))
    wk = wk.reshape(MLA_KV_RANK, MLA_HEADS * LANES)
    wv = wv.reshape(MLA_KV_RANK, MLA_HEADS * MLA_V_DIM)
    return wd.astype(BF16), wq.astype(BF16), wk.astype(BF16), wv.astype(BF16)


def _rope_dense_layout(positions):
    half = MLA_ROPE_DIM // 2
    n_blk = LANES // half
    freqs = ROPE_THETA ** (-jnp.arange(half, dtype=F32) / half)
    order = np.array([(-b) % n_blk for b in range(n_blk)])
    pos_dense = jnp.repeat(positions.reshape(-1, n_blk)[:, order], half, axis=1)
    return pos_dense, jnp.tile(freqs, n_blk)[None, :]


def kernel(x, positions, attn_norm, ffn_norm, even_w_in, even_sinks, even_w_out, rel_bias_table, mla_w_down, mla_q_norm, mla_w_uq, mla_kv_norm, mla_w_ukv, mla_w_o, ffn_w_gate, ffn_w_up, ffn_w_down, final_norm):
    b, s, d = x.shape
    t = b * s
    x2 = x.reshape(t, d)
    row = lambda v: v.reshape(1, -1)

    proj = _norm_proj(x2, row(attn_norm[0]), even_w_in[0].astype(BF16), tm=1024)
    proj3 = proj.reshape(b, s, EVEN_IN_WIDTH)
    o_a = _sb_attention(proj3, tq=1024, sub=256)
    o_b = _swa_attention(proj3, rel_bias_table, even_sinks[0].reshape(1, SW_HEADS), qb=8)
    wg, wu, wd_ffn = _to_bf16(ffn_w_gate, tr=256), _to_bf16(ffn_w_up, tr=256), _to_bf16(ffn_w_down, tr=704)
    x2 = _out_ffn(x2, [o_a.reshape(t, SB_WIDTH), o_b.reshape(t, SW_Q_WIDTH)], even_w_out[0].astype(BF16),
                  row(ffn_norm[0]), wg, wu, wd_ffn, row(final_norm), layer=0, tm=512, final_norm=False)

    wd, wq, wk, wv = _mla_weight_layout(mla_w_down[0], mla_w_uq[0], mla_w_ukv[0])
    pos_dense, freq_dense = _rope_dense_layout(positions)
    q, k, v = _mla_proj(x2, pos_dense, freq_dense, row(attn_norm[1]), wd,
                        row(mla_q_norm[0]), wq, row(mla_kv_norm[0]), wk, wv, tm=512)
    o = _mla_attention(q.reshape(b, s, -1), k.reshape(b, s, -1), v.reshape(b, s, -1), tq=2048, tk=512, unroll=2)
    x2 = _out_ffn(x2, [o.reshape(t, MLA_HEADS * MLA_V_DIM)], mla_w_o[0].astype(BF16),
                  row(ffn_norm[1]), wg, wu, wd_ffn, row(final_norm), layer=1, tm=512, final_norm=True)
    return x2.reshape(b, s, d)
```

```python
import functools
import math

import numpy as np
import jax
import jax.numpy as jnp
from jax import lax
from jax.experimental import pallas as pl
from jax.experimental.pallas import tpu as pltpu

F32 = jnp.float32
BF16 = jnp.bfloat16

D_MODEL = 1024
EPS = 1e-6
HEAD_DIM = 64
LANES = 128
SB_HEADS = 8
SW_HEADS = 8
SW_KV_HEADS = 2
WINDOW = 128
SB_WIDTH = SB_HEADS * HEAD_DIM
SW_Q_WIDTH = SW_HEADS * HEAD_DIM
SW_KV_WIDTH = SW_KV_HEADS * HEAD_DIM
EVEN_IN_WIDTH = 3 * SB_WIDTH + SW_Q_WIDTH + 2 * SW_KV_WIDTH
REL_BUCKETS = 32
REL_MAX_DIST = 128
MLA_HEADS = 16
MLA_NOPE_DIM = 64
MLA_ROPE_DIM = 32
MLA_V_DIM = 64
MLA_Q_RANK = 384
MLA_KV_RANK = 256
ROPE_THETA = 10000.0
LOG2E = math.log2(math.e)
SB_SATURATED = 128.0
VMEM_LIMIT_BYTES = 56 * 1024 * 1024


def _cparams(n_axes):
    return pltpu.CompilerParams(
        dimension_semantics=("arbitrary",) * n_axes,
        vmem_limit_bytes=VMEM_LIMIT_BYTES)


def _rms(x, g):
    return x * lax.rsqrt(jnp.mean(x * x, axis=-1, keepdims=True) + EPS) * g


def _dot(a, b):
    return jnp.dot(a, b, preferred_element_type=F32)


def _dot_nt(a, b):
    return lax.dot_general(a, b, (((1,), (1,)), ((), ())), preferred_element_type=F32)


def _norm_proj_kernel(x_ref, g_ref, w_ref, o_ref):
    h = _rms(x_ref[...], g_ref[...]).astype(BF16)
    o_ref[...] = _dot(h, w_ref[...]).astype(o_ref.dtype)


def _norm_proj(x2, g, w, *, tm):
    t, d = x2.shape
    n = w.shape[1]
    return pl.pallas_call(
        _norm_proj_kernel,
        out_shape=jax.ShapeDtypeStruct((t, n), BF16),
        grid=(t // tm,),
        in_specs=[pl.BlockSpec((tm, d), lambda i: (i, 0)),
                  pl.BlockSpec((1, d), lambda i: (0, 0)),
                  pl.BlockSpec((d, n), lambda i: (0, 0))],
        out_specs=pl.BlockSpec((tm, n), lambda i: (i, 0)),
        compiler_params=_cparams(1),
        name="norm_proj",
    )(x2, g, w)


def _softplus(z):
    return jnp.maximum(z, 0.0) + jnp.log(1.0 + jnp.exp2(jnp.abs(z) * (-LOG2E)))


def _sb_kernel(q_ref, k_ref, v_ref, o_ref, r_ref, acc_ref, *, tq, sub):
    qi = pl.program_id(2)
    r = tq // sub
    q2 = q_ref[0]
    lane = lax.broadcasted_iota(jnp.int32, (tq, LANES), 1)
    low = lane < HEAD_DIM
    zero = jnp.zeros_like(q2)
    scale = HEAD_DIM ** -0.5
    qm = (jnp.where(low, q2, zero) * scale, jnp.where(low, zero, q2) * scale)

    tri_row = lax.broadcasted_iota(jnp.int32, (sub, sub), 0)
    tri_col = lax.broadcasted_iota(jnp.int32, (sub, sub), 1)
    strictly_lower = tri_col < tri_row
    tri = strictly_lower.astype(BF16)

    r_ref[...] = jnp.zeros(r_ref.shape, F32)
    acc_ref[...] = jnp.zeros(acc_ref.shape, F32)

    def tile(hh, r0, nrows, j, masked, enabled=None):
        rows = pl.ds(r0, nrows)
        start = pl.multiple_of(j * sub, sub)
        kblk = k_ref[0, pl.ds(start, sub), :]
        vblk = v_ref[0, pl.ds(start, sub), :]
        z = _dot_nt(qm[hh][r0:r0 + nrows], kblk)
        if enabled is not None:
            z = jnp.where(enabled, z, -jnp.inf)
        if masked:
            top = jnp.where(strictly_lower, z[:sub], -jnp.inf)
            z = top if nrows == sub else jnp.concatenate([top, z[sub:]], axis=0)
        sp = _softplus(z)
        log_beta = z - sp
        later = r_ref[hh, rows, :]
        e = _dot(sp.astype(BF16), tri) + jnp.tile(later, (1, sub // LANES))
        w = jnp.exp(log_beta - e)
        acc_ref[hh, rows, :] += _dot(w.astype(BF16), vblk)
        r_ref[hh, rows, :] = jnp.broadcast_to(e[:, 0:1] + sp[:, 0:1], (nrows, LANES))

    def unsaturated(first_row):
        return jnp.min(r_ref[:, first_row:, :]) < SB_SATURATED

    band = 2 * sub
    n_blocks = qi * r
    for u in reversed(range(r)):
        for hh in range(2):
            tile(hh, u * sub, min(band, tq - u * sub), n_blocks + u, True)
    for hh in range(2):
        tile(hh, 0, sub, jnp.maximum(n_blocks - 1, 0), False, enabled=qi > 0)

    @pl.when(unsaturated(0))
    def _():
        for u in reversed(range(r)):
            r0 = u * sub + band
            if r0 < tq:
                @pl.when(unsaturated(r0))
                def _():
                    for hh in range(2):
                        tile(hh, r0, tq - r0, n_blocks + u, False)

        @pl.when(jnp.logical_and(qi > 0, unsaturated(sub)))
        def _():
            for hh in range(2):
                tile(hh, sub, tq - sub, n_blocks - 1, False)

        def alive():
            top = jnp.min(r_ref[:, :sub, :]) < SB_SATURATED
            rest = jnp.min(r_ref[:, sub:, :]) < SB_SATURATED
            return top, rest

        def cond(state):
            jj, top, rest = state
            return jnp.logical_and(jj < n_blocks, jnp.logical_or(top, rest))

        def body(state):
            jj, _, rest = state
            j = n_blocks - 1 - jj

            @pl.when(rest)
            def _():
                for hh in range(2):
                    tile(hh, 0, tq, j, False)

            @pl.when(jnp.logical_not(rest))
            def _():
                for hh in range(2):
                    tile(hh, 0, sub, j, False)

            return (jj + 1,) + alive()

        lax.while_loop(cond, body, (jnp.int32(1),) + alive())

    o_ref[0] = jnp.where(low, acc_ref[0], acc_ref[1]).astype(o_ref.dtype)


def _sb_attention(proj3, *, tq, sub):
    b, s, _ = proj3.shape
    n_pairs = SB_WIDTH // LANES
    k_off = SB_WIDTH // LANES
    v_off = 2 * SB_WIDTH // LANES
    return pl.pallas_call(
        functools.partial(_sb_kernel, tq=tq, sub=sub),
        out_shape=jax.ShapeDtypeStruct((b, s, SB_WIDTH), BF16),
        grid=(b, n_pairs, s // tq),
        in_specs=[pl.BlockSpec((1, tq, LANES), lambda bi, p, i: (bi, i, p)),
                  pl.BlockSpec((1, s, LANES), lambda bi, p, i: (bi, 0, k_off + p)),
                  pl.BlockSpec((1, s, LANES), lambda bi, p, i: (bi, 0, v_off + p))],
        out_specs=pl.BlockSpec((1, tq, LANES), lambda bi, p, i: (bi, i, p)),
        scratch_shapes=[pltpu.VMEM((2, tq, LANES), F32), pltpu.VMEM((2, tq, LANES), F32)],
        compiler_params=_cparams(3),
        name="sb_attention",
    )(proj3, proj3, proj3)


def _t5_bucket_table():
    max_exact = REL_BUCKETS // 2
    rel = WINDOW + np.arange(WINDOW)[:, None] - np.arange(2 * WINDOW)[None, :]
    rel = np.maximum(rel, 0)
    relf = np.maximum(rel, 1).astype(np.float32)
    large = max_exact + (np.log(relf / np.float32(max_exact)) / np.float32(math.log(REL_MAX_DIST / max_exact))
                         * np.float32(REL_BUCKETS - max_exact)).astype(np.int32)
    large = np.minimum(large, REL_BUCKETS - 1)
    return np.where(rel < max_exact, rel, large).astype(np.int32)


def _swap_halves(x):
    return jnp.concatenate([x[:, HEAD_DIM:], x[:, :HEAD_DIM]], axis=1)


def _swa_kernel(table_ref, sinks_ref, bucket_ref, q_ref, kp_ref, kc_ref, vp_ref, vc_ref,
                o_ref, bias_ref, *, qb):
    bi = pl.program_id(0)
    n = pl.program_id(1)
    blk = WINDOW
    t = lax.broadcasted_iota(jnp.int32, (blk, 2 * blk), 0)
    s = lax.broadcasted_iota(jnp.int32, (blk, 2 * blk), 1)

    @pl.when(jnp.logical_and(bi == 0, n == 0))
    def _():
        bucket = bucket_ref[...]
        rel = blk + t - s
        in_window = (rel >= 0) & (rel < WINDOW)
        for h in range(SW_HEADS):
            bias = jnp.zeros((blk, 2 * blk), F32)
            for bk in range(REL_BUCKETS):
                bias = jnp.where(bucket == bk, table_ref[bk, h], bias)
            bias_ref[h] = jnp.where(in_window, bias, -jnp.inf)

    kk = jnp.concatenate([kp_ref[0], kc_ref[0]], axis=0)
    vv = jnp.concatenate([vp_ref[0], vc_ref[0]], axis=0)
    kk_sw = _swap_halves(kk)
    vv_sw = _swap_halves(vv)
    lane = lax.broadcasted_iota(jnp.int32, (blk, LANES), 1)
    low = lane < HEAD_DIM
    scale = HEAD_DIM ** -0.5
    group = SW_HEADS // SW_KV_HEADS
    has_prev = (s >= blk) | (n > 0)

    for j in range(qb):
        rows = slice(j * blk, (j + 1) * blk)
        keys = slice(j * blk, (j + 2) * blk)
        for p in range(SW_HEADS // 2):
            q2 = q_ref[0, rows, p * LANES:(p + 1) * LANES]
            zero = jnp.zeros_like(q2)
            outs = []
            for half in range(2):
                h = 2 * p + half
                kv = h // group
                qh = (jnp.where(low, q2, zero) if half == 0 else jnp.where(low, zero, q2)) * scale
                k_use = (kk if kv == half else kk_sw)[keys]
                v_use = (vv if kv == half else vv_sw)[keys]
                logits = _dot_nt(qh, k_use) + bias_ref[h]
                if j == 0:
                    logits = jnp.where(has_prev, logits, -jnp.inf)
                sink = sinks_ref[0, h]
                m = jnp.maximum(jnp.max(logits, axis=-1, keepdims=True), sink)
                pr = jnp.exp(logits - m)
                denom = jnp.sum(pr, axis=-1, keepdims=True) + jnp.exp(sink - m)
                outs.append(_dot(pr.astype(BF16), v_use) / denom)
            o_ref[0, rows, p * LANES:(p + 1) * LANES] = jnp.where(low, outs[0], outs[1]).astype(o_ref.dtype)


def _swa_attention(proj3, table, sinks, *, qb):
    b, s, _ = proj3.shape
    rows = qb * WINDOW
    q_blk = (3 * SB_WIDTH) // SW_Q_WIDTH
    k_blk = (3 * SB_WIDTH + SW_Q_WIDTH) // LANES
    v_blk = k_blk + 1
    bucket = jnp.asarray(_t5_bucket_table())
    smem = pl.BlockSpec(memory_space=pltpu.SMEM)
    prev = lambda c: (lambda bi, n: (bi, jnp.maximum(n * qb - 1, 0), c))
    cur = lambda c: (lambda bi, n: (bi, n, c))
    return pl.pallas_call(
        functools.partial(_swa_kernel, qb=qb),
        out_shape=jax.ShapeDtypeStruct((b, s, SW_Q_WIDTH), BF16),
        grid=(b, s // rows),
        in_specs=[smem, smem,
                  pl.BlockSpec((WINDOW, 2 * WINDOW), lambda bi, n: (0, 0)),
                  pl.BlockSpec((1, rows, SW_Q_WIDTH), cur(q_blk)),
                  pl.BlockSpec((1, WINDOW, LANES), prev(k_blk)),
                  pl.BlockSpec((1, rows, LANES), cur(k_blk)),
                  pl.BlockSpec((1, WINDOW, LANES), prev(v_blk)),
                  pl.BlockSpec((1, rows, LANES), cur(v_blk))],
        out_specs=pl.BlockSpec((1, rows, SW_Q_WIDTH), lambda bi, n: (bi, n, 0)),
        scratch_shapes=[pltpu.VMEM((SW_HEADS, WINDOW, 2 * WINDOW), F32)],
        compiler_params=_cparams(2),
        name="swa_attention",
    )(table, sinks, bucket, proj3, proj3, proj3, proj3, proj3)


def _to_bf16_kernel(w_ref, o_ref):
    o_ref[...] = w_ref[...].astype(BF16)


def _to_bf16(w, *, tr):
    n, r, c = w.shape
    spec = pl.BlockSpec((1, tr, c), lambda l, i: (l, i, 0))
    return pl.pallas_call(
        _to_bf16_kernel,
        out_shape=jax.ShapeDtypeStruct(w.shape, BF16),
        grid=(n, r // tr),
        in_specs=[spec],
        out_specs=spec,
        compiler_params=_cparams(2),
        name="to_bf16",
    )(w)


def _out_ffn_kernel(*refs, n_parts, final_norm):
    x_ref = refs[0]
    part_refs = refs[1:1 + n_parts]
    wo_refs = refs[1 + n_parts:1 + 2 * n_parts]
    g_ref, wg_ref, wu_ref, wd_ref, fg_ref, o_ref = refs[1 + 2 * n_parts:]
    x = x_ref[...]
    for a_ref, w_ref in zip(part_refs, wo_refs):
        x = x + _dot(a_ref[...], w_ref[...])
    h = _rms(x, g_ref[...]).astype(BF16)
    gate = _dot(h, wg_ref[...])
    up = _dot(h, wu_ref[...])
    act = (gate / (1.0 + jnp.exp(-gate)) * up).astype(BF16)
    y = x + _dot(act, wd_ref[...])
    if final_norm:
        y = _rms(y, fg_ref[...])
    o_ref[...] = y


def _out_ffn(x2, parts, w_out, g, wg, wu, wd, fg, *, layer, tm, final_norm):
    t, d = x2.shape
    hid = wg.shape[2]
    n_parts = len(parts)
    kp = w_out.shape[0] // n_parts
    resident = dict(pipeline_mode=pl.Buffered(1))
    row_tile = lambda width: pl.BlockSpec((tm, width), lambda i: (i, 0))
    const = lambda r, c: pl.BlockSpec((None, r, c), lambda i: (layer, 0, 0), **resident)
    in_specs = [row_tile(d)] + [row_tile(kp) for _ in parts]
    in_specs += [pl.BlockSpec((kp, d), functools.partial(lambda i, c: (c, 0), c=c), **resident)
                 for c in range(n_parts)]
    in_specs += [pl.BlockSpec((1, d), lambda i: (0, 0)), const(d, hid), const(d, hid),
                 const(hid, d), pl.BlockSpec((1, d), lambda i: (0, 0))]
    return pl.pallas_call(
        functools.partial(_out_ffn_kernel, n_parts=n_parts, final_norm=final_norm),
        out_shape=jax.ShapeDtypeStruct((t, d), F32),
        grid=(t // tm,),
        in_specs=in_specs,
        out_specs=row_tile(d),
        compiler_params=_cparams(1),
        name="out_ffn_final" if final_norm else "out_ffn",
    )(x2, *parts, *([w_out] * n_parts), g, wg, wu, wd, fg)


def _mla_proj_kernel(x_ref, pos_ref, freq_ref, g_ref, wd_ref, qg_ref, wq_ref,
                     kg_ref, wk_ref, wv_ref, q_ref, k_ref, v_ref):
    tm = x_ref.shape[0]
    h = _rms(x_ref[...], g_ref[...]).astype(BF16)
    down = _dot(h, wd_ref[...])
    c_q = down[:, :MLA_Q_RANK]
    c_kv = down[:, MLA_Q_RANK:MLA_Q_RANK + MLA_KV_RANK]
    kr = down[:, MLA_Q_RANK + MLA_KV_RANK:]

    half = MLA_ROPE_DIM // 2
    ang = pos_ref[...].astype(F32) * freq_ref[...]
    lane = lax.broadcasted_iota(jnp.int32, (tm, LANES), 1)
    first = lane < half
    second = (lane >= LANES // 2) & (lane < LANES // 2 + half)

    def spread(dense):
        rows8 = jnp.broadcast_to(dense[:, None, :], (tm // 8, 8, LANES)).reshape(tm, LANES)
        y = pltpu.roll(rows8, 0, 1, stride=half, stride_axis=0)
        return y, pltpu.roll(y, LANES // 2, 1)

    cos_1, cos_2 = spread(jnp.cos(ang))
    sin_1, sin_2 = spread(jnp.sin(ang))
    cos_p = jnp.where(first, cos_1, jnp.where(second, cos_2, 1.0))
    sin_p = jnp.where(first, -sin_1, jnp.where(second, sin_2, 0.0))

    def rope(blk):
        return blk * cos_p + pltpu.roll(blk, LANES // 2, 1) * sin_p

    cqn = _rms(c_q, qg_ref[...]).astype(BF16)
    ckvn = _rms(c_kv, kg_ref[...]).astype(BF16)
    k_rope = rope(kr)
    q = _dot(cqn, wq_ref[...])
    kn = _dot(ckvn, wk_ref[...])
    for hd in range(MLA_HEADS):
        sl = slice(hd * LANES, (hd + 1) * LANES)
        q_ref[:, sl] = rope(q[:, sl]).astype(q_ref.dtype)
        k_ref[:, sl] = (kn[:, sl] + k_rope).astype(k_ref.dtype)
    v_ref[...] = _dot(ckvn, wv_ref[...]).astype(v_ref.dtype)


def _mla_proj(x2, pos2, freq, g, wd, qg, wq, kg, wk, wv, *, tm):
    t, d = x2.shape
    full = lambda a: pl.BlockSpec(a.shape, lambda i: (0,) * a.ndim)
    qk_w = MLA_HEADS * LANES
    v_w = MLA_HEADS * MLA_V_DIM
    return pl.pallas_call(
        _mla_proj_kernel,
        out_shape=(jax.ShapeDtypeStruct((t, qk_w), BF16),
                   jax.ShapeDtypeStruct((t, qk_w), BF16),
                   jax.ShapeDtypeStruct((t, v_w), BF16)),
        grid=(t // tm,),
        in_specs=[pl.BlockSpec((tm, d), lambda i: (i, 0)),
                  pl.BlockSpec((tm // 8, LANES), lambda i: (i, 0)),
                  full(freq), full(g), full(wd), full(qg), full(wq),
                  full(kg), full(wk), full(wv)],
        out_specs=(pl.BlockSpec((tm, qk_w), lambda i: (i, 0)),
                   pl.BlockSpec((tm, qk_w), lambda i: (i, 0)),
                   pl.BlockSpec((tm, v_w), lambda i: (i, 0))),
        compiler_params=_cparams(1),
        name="mla_proj",
    )(x2, pos2, freq, g, wd, qg, wq, kg, wk, wv)


def _mla_attn_kernel(q_ref, k_ref, v_ref, o_ref, va_ref, vb_ref, m_ref, acc_ref, *, tq, tk, unroll):
    qi = pl.program_id(2)
    lane_s = lax.broadcasted_iota(jnp.int32, va_ref.shape, 1)

    @pl.when(qi == 0)
    def _():
        v_all = v_ref[0]
        one = jnp.ones_like(v_all)
        va_ref[...] = jnp.where(lane_s < MLA_V_DIM, v_all, one)
        vb_ref[...] = jnp.where(lane_s < MLA_V_DIM, one, v_all)

    r = tq // tk
    lower = (lax.broadcasted_iota(jnp.int32, (tk, tk), 1)
             <= lax.broadcasted_iota(jnp.int32, (tk, tk), 0))
    c2 = (MLA_NOPE_DIM + MLA_ROPE_DIM) ** -0.5 * LOG2E
    vaug = (va_ref, vb_ref)

    m_ref[...] = jnp.full(m_ref.shape, -jnp.inf, F32)
    acc_ref[...] = jnp.zeros(acc_ref.shape, F32)

    def tile(hh, r0, nrows, j, masked):
        rows = pl.ds(r0, nrows)
        q = q_ref[0, rows, hh * LANES:(hh + 1) * LANES]
        start = pl.multiple_of(j * tk, tk)
        kblk = k_ref[0, pl.ds(start, tk), hh * LANES:(hh + 1) * LANES]
        vblk = vaug[hh][pl.ds(start, tk), :]
        s2 = _dot_nt(q, kblk) * c2
        if masked:
            top = jnp.where(lower, s2[:tk], -jnp.inf)
            s2 = top if nrows == tk else jnp.concatenate([top, s2[tk:]], axis=0)
        m = m_ref[hh, rows, :]
        m_new = jnp.maximum(m, jnp.max(s2, axis=-1, keepdims=True))
        alpha = jnp.exp2(m - m_new)
        p = jnp.exp2(s2 - jnp.tile(m_new, (1, tk // LANES)))
        acc_ref[hh, rows, :] = alpha * acc_ref[hh, rows, :] + _dot(p.astype(BF16), vblk)
        m_ref[hh, rows, :] = m_new

    def body(jj, carry):
        for t in range(unroll):
            for hh in range(2):
                tile(hh, 0, tq, jj * unroll + t, False)
        return carry

    lax.fori_loop(0, (qi * r) // unroll, body, 0)
    for u in range(r):
        for hh in range(2):
            tile(hh, u * tk, tq - u * tk, qi * r + u, True)

    lane = lax.broadcasted_iota(jnp.int32, (tq, LANES), 1)
    acc_a = acc_ref[0]
    acc_b = acc_ref[1]
    out_a = acc_a / pltpu.roll(acc_a, MLA_V_DIM, 1)
    out_b = acc_b / pltpu.roll(acc_b, MLA_V_DIM, 1)
    o_ref[0] = jnp.where(lane < MLA_V_DIM, out_a, out_b).astype(o_ref.dtype)


def _mla_attention(q3, k3, v3, *, tq, tk, unroll):
    b, s, _ = q3.shape
    assert (tq // tk) % unroll == 0
    n_pairs = MLA_HEADS // 2
    return pl.pallas_call(
        functools.partial(_mla_attn_kernel, tq=tq, tk=tk, unroll=unroll),
        out_shape=jax.ShapeDtypeStruct((b, s, MLA_HEADS * MLA_V_DIM), BF16),
        grid=(b, n_pairs, s // tq),
        in_specs=[pl.BlockSpec((1, tq, 2 * LANES), lambda bi, p, i: (bi, i, p)),
                  pl.BlockSpec((1, s, 2 * LANES), lambda bi, p, i: (bi, 0, p)),
                  pl.BlockSpec((1, s, LANES), lambda bi, p, i: (bi, 0, p))],
        out_specs=pl.BlockSpec((1, tq, LANES), lambda bi, p, i: (bi, i, p)),
        scratch_shapes=[pltpu.VMEM((s, LANES), BF16), pltpu.VMEM((s, LANES), BF16),
                        pltpu.VMEM((2, tq, LANES), F32), pltpu.VMEM((2, tq, LANES), F32)],
        compiler_params=_cparams(3),
        name="mla_attention",
    )(q3, k3, v3)


def _head_lanes(nope, rope):
    half = MLA_ROPE_DIM // 2
    split = LANES // 2 - half
    zero = jnp.zeros(nope.shape[:-1] + (LANES - MLA_NOPE_DIM - MLA_ROPE_DIM,), nope.dtype)
    return jnp.concatenate([rope[..., :half], nope[..., :split], rope[..., half:], nope[..., split:], zero],
                           axis=-1)


def _mla_weight_layout(w_down, w_uq, w_ukv):
    d = w_down.shape[0]
    w_kr = w_down[:, MLA_Q_RANK + MLA_KV_RANK:]
    wd = jnp.concatenate([w_down[:, :MLA_Q_RANK + MLA_KV_RANK],
                          _head_lanes(jnp.zeros((d, MLA_NOPE_DIM), F32), w_kr)], axis=1)

    wq = w_uq.reshape(MLA_Q_RANK, MLA_HEADS, MLA_NOPE_DIM + MLA_ROPE_DIM)
    wq = _head_lanes(wq[..., :MLA_NOPE_DIM], wq[..., MLA_NOPE_DIM:]).reshape(MLA_Q_RANK, MLA_HEADS * LANES)

    wkv = w_ukv.reshape(MLA_KV_RANK, MLA_HEADS, MLA_NOPE_DIM + MLA_V_DIM)
    wk_nope, wv = wkv[..., :MLA_NOPE_DIM], wkv[..., MLA_NOPE_DIM:]
    wk = _head_lanes(wk_nope, jnp.zeros((MLA_KV_RANK, MLA_HEADS, MLA_ROPE_DIM), F32))
    wk = wk.reshape(MLA_KV_RANK, MLA_HEADS * LANES)
    wv = wv.reshape(MLA_KV_RANK, MLA_HEADS * MLA_V_DIM)
    return wd.astype(BF16), wq.astype(BF16), wk.astype(BF16), wv.astype(BF16)


def _rope_dense_layout(positions):
    half = MLA_ROPE_DIM // 2
    n_blk = LANES // half
    freqs = ROPE_THETA ** (-jnp.arange(half, dtype=F32) / half)
    order = np.array([(-b) % n_blk for b in range(n_blk)])
    pos_dense = jnp.repeat(positions.reshape(-1, n_blk)[:, order], half, axis=1)
    return pos_dense, jnp.tile(freqs, n_blk)[None, :]


def kernel(x, positions, attn_norm, ffn_norm, even_w_in, even_sinks, even_w_out, rel_bias_table, mla_w_down, mla_q_norm, mla_w_uq, mla_kv_norm, mla_w_ukv, mla_w_o, ffn_w_gate, ffn_w_up, ffn_w_down, final_norm):
    b, s, d = x.shape
    t = b * s
    x2 = x.reshape(t, d)
    row = lambda v: v.reshape(1, -1)

    proj = _norm_proj(x2, row(attn_norm[0]), even_w_in[0].astype(BF16), tm=1024)
    proj3 = proj.reshape(b, s, EVEN_IN_WIDTH)
    o_a = _sb_attention(proj3, tq=1024, sub=256)
    o_b = _swa_attention(proj3, rel_bias_table, even_sinks[0].reshape(1, SW_HEADS), qb=8)
    wg, wu, wd_ffn = _to_bf16(ffn_w_gate, tr=256), _to_bf16(ffn_w_up, tr=256), _to_bf16(ffn_w_down, tr=704)
    x2 = _out_ffn(x2, [o_a.reshape(t, SB_WIDTH), o_b.reshape(t, SW_Q_WIDTH)], even_w_out[0].astype(BF16),
                  row(ffn_norm[0]), wg, wu, wd_ffn, row(final_norm), layer=0, tm=512, final_norm=False)

    wd, wq, wk, wv = _mla_weight_layout(mla_w_down[0], mla_w_uq[0], mla_w_ukv[0])
    pos_dense, freq_dense = _rope_dense_layout(positions)
    q, k, v = _mla_proj(x2, pos_dense, freq_dense, row(attn_norm[1]), wd,
                        row(mla_q_norm[0]), wq, row(mla_kv_norm[0]), wk, wv, tm=512)
    o = _mla_attention(q.reshape(b, s, -1), k.reshape(b, s, -1), v.reshape(b, s, -1), tq=4096, tk=512, unroll=2)
    x2 = _out_ffn(x2, [o.reshape(t, MLA_HEADS * MLA_V_DIM)], mla_w_o[0].astype(BF16),
                  row(ffn_norm[1]), wg, wu, wd_ffn, row(final_norm), layer=1, tm=512, final_norm=True)
    return x2.reshape(b, s, d)
```

```python
import functools
import math

import numpy as np
import jax
import jax.numpy as jnp
from jax import lax
from jax.experimental import pallas as pl
from jax.experimental.pallas import tpu as pltpu

F32 = jnp.float32
BF16 = jnp.bfloat16

D_MODEL = 1024
EPS = 1e-6
HEAD_DIM = 64
LANES = 128
SB_HEADS = 8
SW_HEADS = 8
SW_KV_HEADS = 2
WINDOW = 128
SB_WIDTH = SB_HEADS * HEAD_DIM
SW_Q_WIDTH = SW_HEADS * HEAD_DIM
SW_KV_WIDTH = SW_KV_HEADS * HEAD_DIM
EVEN_IN_WIDTH = 3 * SB_WIDTH + SW_Q_WIDTH + 2 * SW_KV_WIDTH
REL_BUCKETS = 32
REL_MAX_DIST = 128
MLA_HEADS = 16
MLA_NOPE_DIM = 64
MLA_ROPE_DIM = 32
MLA_V_DIM = 64
MLA_Q_RANK = 384
MLA_KV_RANK = 256
ROPE_THETA = 10000.0
LOG2E = math.log2(math.e)
SB_SATURATED = 128.0
VMEM_LIMIT_BYTES = 56 * 1024 * 1024


def _cparams(n_axes):
    return pltpu.CompilerParams(
        dimension_semantics=("arbitrary",) * n_axes,
        vmem_limit_bytes=VMEM_LIMIT_BYTES)


def _rms(x, g):
    return x * lax.rsqrt(jnp.mean(x * x, axis=-1, keepdims=True) + EPS) * g


def _dot(a, b):
    return jnp.dot(a, b, preferred_element_type=F32)


def _dot_nt(a, b):
    return lax.dot_general(a, b, (((1,), (1,)), ((), ())), preferred_element_type=F32)


def _norm_proj_kernel(x_ref, g_ref, w_ref, o_ref):
    h = _rms(x_ref[...], g_ref[...]).astype(BF16)
    o_ref[...] = _dot(h, w_ref[...]).astype(o_ref.dtype)


def _norm_proj(x2, g, w, *, tm):
    t, d = x2.shape
    n = w.shape[1]
    return pl.pallas_call(
        _norm_proj_kernel,
        out_shape=jax.ShapeDtypeStruct((t, n), BF16),
        grid=(t // tm,),
        in_specs=[pl.BlockSpec((tm, d), lambda i: (i, 0)),
                  pl.BlockSpec((1, d), lambda i: (0, 0)),
                  pl.BlockSpec((d, n), lambda i: (0, 0))],
        out_specs=pl.BlockSpec((tm, n), lambda i: (i, 0)),
        compiler_params=_cparams(1),
        name="norm_proj",
    )(x2, g, w)


def _softplus(z):
    return jnp.maximum(z, 0.0) + jnp.log(1.0 + jnp.exp2(jnp.abs(z) * (-LOG2E)))


def _sb_kernel(q_ref, k_ref, v_ref, o_ref, r_ref, acc_ref, *, sub, group):
    tq = q_ref.shape[1]
    r = tq // sub
    q2 = q_ref[0]
    lane = lax.broadcasted_iota(jnp.int32, (tq, LANES), 1)
    low = lane < HEAD_DIM
    zero = jnp.zeros_like(q2)
    scale = HEAD_DIM ** -0.5
    qm = (jnp.where(low, q2, zero) * scale, jnp.where(low, zero, q2) * scale)

    tri_row = lax.broadcasted_iota(jnp.int32, (sub, sub), 0)
    tri_col = lax.broadcasted_iota(jnp.int32, (sub, sub), 1)
    strictly_lower = tri_col < tri_row
    tri = strictly_lower.astype(BF16)

    r_ref[...] = jnp.zeros(r_ref.shape, F32)
    acc_ref[...] = jnp.zeros(acc_ref.shape, F32)

    def tile(hh, r0, nrows, j, masked, first_row=None):
        rows = pl.ds(r0, nrows)
        start = pl.multiple_of(j * sub, sub)
        kblk = k_ref[0, pl.ds(start, sub), :]
        vblk = v_ref[0, pl.ds(start, sub), :]
        z = _dot_nt(qm[hh][r0:r0 + nrows], kblk)
        if first_row is not None:
            row = lax.broadcasted_iota(jnp.int32, (nrows, sub), 0) + r0
            z = jnp.where(row >= first_row, z, -jnp.inf)
        if masked:
            top = jnp.where(strictly_lower, z[:sub], -jnp.inf)
            z = top if nrows == sub else jnp.concatenate([top, z[sub:]], axis=0)
        sp = _softplus(z)
        log_beta = z - sp
        later = r_ref[hh, rows, :]
        e = _dot(sp.astype(BF16), tri) + jnp.tile(later, (1, sub // LANES))
        w = jnp.exp(log_beta - e)
        acc_ref[hh, rows, :] += _dot(w.astype(BF16), vblk)
        r_ref[hh, rows, :] = jnp.broadcast_to(e[:, 0:1] + sp[:, 0:1], (nrows, LANES))

    def unsaturated(first_row):
        return jnp.min(r_ref[:, first_row:, :]) < SB_SATURATED

    band = 2 * sub
    for u in reversed(range(r)):
        for hh in range(2):
            tile(hh, u * sub, min(band, tq - u * sub), u, True)

    @pl.when(unsaturated(band))
    def _():
        def body(jj, carry):
            j = r - 3 - jj
            first_row = (j + 2) * sub
            for g in range(tq // group):
                g0 = max(g * group, band)

                @pl.when(jnp.logical_and(first_row // group == g, unsaturated(g0)))
                def _():
                    for hh in range(2):
                        tile(hh, g * group, tq - g * group, j, False, first_row=first_row)
            return carry

        lax.fori_loop(0, r - 2, body, 0)

    o_ref[0] = jnp.where(low, acc_ref[0], acc_ref[1]).astype(o_ref.dtype)


def _sb_attention(proj3, *, sub, group):
    b, s, _ = proj3.shape
    n_pairs = SB_WIDTH // LANES
    k_off = SB_WIDTH // LANES
    v_off = 2 * SB_WIDTH // LANES
    seq = lambda c0: pl.BlockSpec((1, s, LANES), lambda bi, p: (bi, 0, c0 + p))
    return pl.pallas_call(
        functools.partial(_sb_kernel, sub=sub, group=group),
        out_shape=jax.ShapeDtypeStruct((b, s, SB_WIDTH), BF16),
        grid=(b, n_pairs),
        in_specs=[seq(0), seq(k_off), seq(v_off)],
        out_specs=seq(0),
        scratch_shapes=[pltpu.VMEM((2, s, LANES), F32), pltpu.VMEM((2, s, LANES), F32)],
        compiler_params=_cparams(2),
        name="sb_attention",
    )(proj3, proj3, proj3)


def _t5_bucket_table():
    max_exact = REL_BUCKETS // 2
    rel = WINDOW + np.arange(WINDOW)[:, None] - np.arange(2 * WINDOW)[None, :]
    rel = np.maximum(rel, 0)
    relf = np.maximum(rel, 1).astype(np.float32)
    large = max_exact + (np.log(relf / np.float32(max_exact)) / np.float32(math.log(REL_MAX_DIST / max_exact))
                         * np.float32(REL_BUCKETS - max_exact)).astype(np.int32)
    large = np.minimum(large, REL_BUCKETS - 1)
    return np.where(rel < max_exact, rel, large).astype(np.int32)


def _swap_halves(x):
    return jnp.concatenate([x[:, HEAD_DIM:], x[:, :HEAD_DIM]], axis=1)


def _swa_kernel(table_ref, sinks_ref, bucket_ref, q_ref, kp_ref, kc_ref, vp_ref, vc_ref,
                o_ref, bias_ref, *, qb):
    bi = pl.program_id(0)
    n = pl.program_id(1)
    blk = WINDOW
    t = lax.broadcasted_iota(jnp.int32, (blk, 2 * blk), 0)
    s = lax.broadcasted_iota(jnp.int32, (blk, 2 * blk), 1)

    @pl.when(jnp.logical_and(bi == 0, n == 0))
    def _():
        bucket = bucket_ref[...]
        rel = blk + t - s
        in_window = (rel >= 0) & (rel < WINDOW)
        for h in range(SW_HEADS):
            bias = jnp.zeros((blk, 2 * blk), F32)
            for bk in range(REL_BUCKETS):
                bias = jnp.where(bucket == bk, table_ref[bk, h], bias)
            bias_ref[h] = jnp.where(in_window, bias, -jnp.inf)

    kk = jnp.concatenate([kp_ref[0], kc_ref[0]], axis=0)
    vv = jnp.concatenate([vp_ref[0], vc_ref[0]], axis=0)
    kk_sw = _swap_halves(kk)
    vv_sw = _swap_halves(vv)
    lane = lax.broadcasted_iota(jnp.int32, (blk, LANES), 1)
    low = lane < HEAD_DIM
    scale = HEAD_DIM ** -0.5
    group = SW_HEADS // SW_KV_HEADS
    has_prev = (s >= blk) | (n > 0)

    for j in range(qb):
        rows = slice(j * blk, (j + 1) * blk)
        keys = slice(j * blk, (j + 2) * blk)
        for p in range(SW_HEADS // 2):
            q2 = q_ref[0, rows, p * LANES:(p + 1) * LANES]
            zero = jnp.zeros_like(q2)
            outs = []
            for half in range(2):
                h = 2 * p + half
                kv = h // group
                qh = (jnp.where(low, q2, zero) if half == 0 else jnp.where(low, zero, q2)) * scale
                k_use = (kk if kv == half else kk_sw)[keys]
                v_use = (vv if kv == half else vv_sw)[keys]
                logits = _dot_nt(qh, k_use) + bias_ref[h]
                if j == 0:
                    logits = jnp.where(has_prev, logits, -jnp.inf)
                sink = sinks_ref[0, h]
                m = jnp.maximum(jnp.max(logits, axis=-1, keepdims=True), sink)
                pr = jnp.exp(logits - m)
                denom = jnp.sum(pr, axis=-1, keepdims=True) + jnp.exp(sink - m)
                outs.append(_dot(pr.astype(BF16), v_use) / denom)
            o_ref[0, rows, p * LANES:(p + 1) * LANES] = jnp.where(low, outs[0], outs[1]).astype(o_ref.dtype)


def _swa_attention(proj3, table, sinks, *, qb):
    b, s, _ = proj3.shape
    rows = qb * WINDOW
    q_blk = (3 * SB_WIDTH) // SW_Q_WIDTH
    k_blk = (3 * SB_WIDTH + SW_Q_WIDTH) // LANES
    v_blk = k_blk + 1
    bucket = jnp.asarray(_t5_bucket_table())
    smem = pl.BlockSpec(memory_space=pltpu.SMEM)
    prev = lambda c: (lambda bi, n: (bi, jnp.maximum(n * qb - 1, 0), c))
    cur = lambda c: (lambda bi, n: (bi, n, c))
    return pl.pallas_call(
        functools.partial(_swa_kernel, qb=qb),
        out_shape=jax.ShapeDtypeStruct((b, s, SW_Q_WIDTH), BF16),
        grid=(b, s // rows),
        in_specs=[smem, smem,
                  pl.BlockSpec((WINDOW, 2 * WINDOW), lambda bi, n: (0, 0)),
                  pl.BlockSpec((1, rows, SW_Q_WIDTH), cur(q_blk)),
                  pl.BlockSpec((1, WINDOW, LANES), prev(k_blk)),
                  pl.BlockSpec((1, rows, LANES), cur(k_blk)),
                  pl.BlockSpec((1, WINDOW, LANES), prev(v_blk)),
                  pl.BlockSpec((1, rows, LANES), cur(v_blk))],
        out_specs=pl.BlockSpec((1, rows, SW_Q_WIDTH), lambda bi, n: (bi, n, 0)),
        scratch_shapes=[pltpu.VMEM((SW_HEADS, WINDOW, 2 * WINDOW), F32)],
        compiler_params=_cparams(2),
        name="swa_attention",
    )(table, sinks, bucket, proj3, proj3, proj3, proj3, proj3)


def _to_bf16_kernel(w_ref, o_ref):
    o_ref[...] = w_ref[...].astype(BF16)


def _to_bf16(w, *, tr):
    n, r, c = w.shape
    spec = pl.BlockSpec((1, tr, c), lambda l, i: (l, i, 0))
    return pl.pallas_call(
        _to_bf16_kernel,
        out_shape=jax.ShapeDtypeStruct(w.shape, BF16),
        grid=(n, r // tr),
        in_specs=[spec],
        out_specs=spec,
        compiler_params=_cparams(2),
        name="to_bf16",
    )(w)


def _out_ffn_kernel(*refs, n_parts, final_norm):
    x_ref = refs[0]
    part_refs = refs[1:1 + n_parts]
    wo_refs = refs[1 + n_parts:1 + 2 * n_parts]
    g_ref, wg_ref, wu_ref, wd_ref, fg_ref, o_ref = refs[1 + 2 * n_parts:]
    x = x_ref[...]
    for a_ref, w_ref in zip(part_refs, wo_refs):
        x = x + _dot(a_ref[...], w_ref[...])
    h = _rms(x, g_ref[...]).astype(BF16)
    gate = _dot(h, wg_ref[...])
    up = _dot(h, wu_ref[...])
    act = (gate / (1.0 + jnp.exp(-gate)) * up).astype(BF16)
    y = x + _dot(act, wd_ref[...])
    if final_norm:
        y = _rms(y, fg_ref[...])
    o_ref[...] = y


def _out_ffn(x2, parts, w_out, g, wg, wu, wd, fg, *, layer, tm, final_norm):
    t, d = x2.shape
    hid = wg.shape[2]
    n_parts = len(parts)
    kp = w_out.shape[0] // n_parts
    resident = dict(pipeline_mode=pl.Buffered(1))
    row_tile = lambda width: pl.BlockSpec((tm, width), lambda i: (i, 0))
    const = lambda r, c: pl.BlockSpec((None, r, c), lambda i: (layer, 0, 0), **resident)
    in_specs = [row_tile(d)] + [row_tile(kp) for _ in parts]
    in_specs += [pl.BlockSpec((kp, d), functools.partial(lambda i, c: (c, 0), c=c), **resident)
                 for c in range(n_parts)]
    in_specs += [pl.BlockSpec((1, d), lambda i: (0, 0)), const(d, hid), const(d, hid),
                 const(hid, d), pl.BlockSpec((1, d), lambda i: (0, 0))]
    return pl.pallas_call(
        functools.partial(_out_ffn_kernel, n_parts=n_parts, final_norm=final_norm),
        out_shape=jax.ShapeDtypeStruct((t, d), F32),
        grid=(t // tm,),
        in_specs=in_specs,
        out_specs=row_tile(d),
        compiler_params=_cparams(1),
        name="out_ffn_final" if final_norm else "out_ffn",
    )(x2, *parts, *([w_out] * n_parts), g, wg, wu, wd, fg)


def _mla_proj_kernel(x_ref, pos_ref, freq_ref, g_ref, wd_ref, qg_ref, wq_ref,
                     kg_ref, wk_ref, wv_ref, q_ref, k_ref, v_ref):
    tm = x_ref.shape[0]
    h = _rms(x_ref[...], g_ref[...]).astype(BF16)
    down = _dot(h, wd_ref[...])
    c_q = down[:, :MLA_Q_RANK]
    c_kv = down[:, MLA_Q_RANK:MLA_Q_RANK + MLA_KV_RANK]
    kr = down[:, MLA_Q_RANK + MLA_KV_RANK:]

    half = MLA_ROPE_DIM // 2
    ang = pos_ref[...].astype(F32) * freq_ref[...]
    lane = lax.broadcasted_iota(jnp.int32, (tm, LANES), 1)
    first = lane < half
    second = (lane >= LANES // 2) & (lane < LANES // 2 + half)

    def spread(dense):
        rows8 = jnp.broadcast_to(dense[:, None, :], (tm // 8, 8, LANES)).reshape(tm, LANES)
        y = pltpu.roll(rows8, 0, 1, stride=half, stride_axis=0)
        return y, pltpu.roll(y, LANES // 2, 1)

    cos_1, cos_2 = spread(jnp.cos(ang))
    sin_1, sin_2 = spread(jnp.sin(ang))
    cos_p = jnp.where(first, cos_1, jnp.where(second, cos_2, 1.0))
    sin_p = jnp.where(first, -sin_1, jnp.where(second, sin_2, 0.0))

    def rope(blk):
        return blk * cos_p + pltpu.roll(blk, LANES // 2, 1) * sin_p

    cqn = _rms(c_q, qg_ref[...]).astype(BF16)
    ckvn = _rms(c_kv, kg_ref[...]).astype(BF16)
    k_rope = rope(kr)
    q = _dot(cqn, wq_ref[...])
    kn = _dot(ckvn, wk_ref[...])
    for hd in range(MLA_HEADS):
        sl = slice(hd * LANES, (hd + 1) * LANES)
        q_ref[:, sl] = rope(q[:, sl]).astype(q_ref.dtype)
        k_ref[:, sl] = (kn[:, sl] + k_rope).astype(k_ref.dtype)
    v_ref[...] = _dot(ckvn, wv_ref[...]).astype(v_ref.dtype)


def _mla_proj(x2, pos2, freq, g, wd, qg, wq, kg, wk, wv, *, tm):
    t, d = x2.shape
    full = lambda a: pl.BlockSpec(a.shape, lambda i: (0,) * a.ndim)
    qk_w = MLA_HEADS * LANES
    v_w = MLA_HEADS * MLA_V_DIM
    return pl.pallas_call(
        _mla_proj_kernel,
        out_shape=(jax.ShapeDtypeStruct((t, qk_w), BF16),
                   jax.ShapeDtypeStruct((t, qk_w), BF16),
                   jax.ShapeDtypeStruct((t, v_w), BF16)),
        grid=(t // tm,),
        in_specs=[pl.BlockSpec((tm, d), lambda i: (i, 0)),
                  pl.BlockSpec((tm // 8, LANES), lambda i: (i, 0)),
                  full(freq), full(g), full(wd), full(qg), full(wq),
                  full(kg), full(wk), full(wv)],
        out_specs=(pl.BlockSpec((tm, qk_w), lambda i: (i, 0)),
                   pl.BlockSpec((tm, qk_w), lambda i: (i, 0)),
                   pl.BlockSpec((tm, v_w), lambda i: (i, 0))),
        compiler_params=_cparams(1),
        name="mla_proj",
    )(x2, pos2, freq, g, wd, qg, wq, kg, wk, wv)


def _mla_attn_kernel(q_ref, k_ref, v_ref, o_ref, va_ref, vb_ref, m_ref, acc_ref, *, tq, tk, unroll):
    qi = pl.program_id(2)
    lane_s = lax.broadcasted_iota(jnp.int32, va_ref.shape, 1)

    @pl.when(qi == 0)
    def _():
        v_all = v_ref[0]
        one = jnp.ones_like(v_all)
        va_ref[...] = jnp.where(lane_s < MLA_V_DIM, v_all, one)
        vb_ref[...] = jnp.where(lane_s < MLA_V_DIM, one, v_all)

    r = tq // tk
    lower = (lax.broadcasted_iota(jnp.int32, (tk, tk), 1)
             <= lax.broadcasted_iota(jnp.int32, (tk, tk), 0))
    c2 = (MLA_NOPE_DIM + MLA_ROPE_DIM) ** -0.5 * LOG2E
    vaug = (va_ref, vb_ref)

    m_ref[...] = jnp.full(m_ref.shape, -jnp.inf, F32)
    acc_ref[...] = jnp.zeros(acc_ref.shape, F32)

    def tile(hh, r0, nrows, j, masked):
        rows = pl.ds(r0, nrows)
        q = q_ref[0, rows, hh * LANES:(hh + 1) * LANES]
        start = pl.multiple_of(j * tk, tk)
        kblk = k_ref[0, pl.ds(start, tk), hh * LANES:(hh + 1) * LANES]
        vblk = vaug[hh][pl.ds(start, tk), :]
        s2 = _dot_nt(q, kblk) * c2
        if masked:
            top = jnp.where(lower, s2[:tk], -jnp.inf)
            s2 = top if nrows == tk else jnp.concatenate([top, s2[tk:]], axis=0)
        m = m_ref[hh, rows, :]
        m_new = jnp.maximum(m, jnp.max(s2, axis=-1, keepdims=True))
        alpha = jnp.exp2(m - m_new)
        p = jnp.exp2(s2 - jnp.tile(m_new, (1, tk // LANES)))
        acc_ref[hh, rows, :] = alpha * acc_ref[hh, rows, :] + _dot(p.astype(BF16), vblk)
        m_ref[hh, rows, :] = m_new

    def body(jj, carry):
        for t in range(unroll):
            for hh in range(2):
                tile(hh, 0, tq, jj * unroll + t, False)
        return carry

    lax.fori_loop(0, (qi * r) // unroll, body, 0)
    for u in range(r):
        for hh in range(2):
            tile(hh, u * tk, tq - u * tk, qi * r + u, True)

    lane = lax.broadcasted_iota(jnp.int32, (tq, LANES), 1)
    acc_a = acc_ref[0]
    acc_b = acc_ref[1]
    out_a = acc_a / pltpu.roll(acc_a, MLA_V_DIM, 1)
    out_b = acc_b / pltpu.roll(acc_b, MLA_V_DIM, 1)
    o_ref[0] = jnp.where(lane < MLA_V_DIM, out_a, out_b).astype(o_ref.dtype)


def _mla_attention(q3, k3, v3, *, tq, tk, unroll):
    b, s, _ = q3.shape
    assert (tq // tk) % unroll == 0
    n_pairs = MLA_HEADS // 2
    return pl.pallas_call(
        functools.partial(_mla_attn_kernel, tq=tq, tk=tk, unroll=unroll),
        out_shape=jax.ShapeDtypeStruct((b, s, MLA_HEADS * MLA_V_DIM), BF16),
        grid=(b, n_pairs, s // tq),
        in_specs=[pl.BlockSpec((1, tq, 2 * LANES), lambda bi, p, i: (bi, i, p)),
                  pl.BlockSpec((1, s, 2 * LANES), lambda bi, p, i: (bi, 0, p)),
                  pl.BlockSpec((1, s, LANES), lambda bi, p, i: (bi, 0, p))],
        out_specs=pl.BlockSpec((1, tq, LANES), lambda bi, p, i: (bi, i, p)),
        scratch_shapes=[pltpu.VMEM((s, LANES), BF16), pltpu.VMEM((s, LANES), BF16),
                        pltpu.VMEM((2, tq, LANES), F32), pltpu.VMEM((2, tq, LANES), F32)],
        compiler_params=_cparams(3),
        name="mla_attention",
    )(q3, k3, v3)


def _head_lanes(nope, rope):
    half = MLA_ROPE_DIM // 2
    split = LANES // 2 - half
    zero = jnp.zeros(nope.shape[:-1] + (LANES - MLA_NOPE_DIM - MLA_ROPE_DIM,), nope.dtype)
    return jnp.concatenate([rope[..., :half], nope[..., :split], rope[..., half:], nope[..., split:], zero],
                           axis=-1)


def _mla_weight_layout(w_down, w_uq, w_ukv):
    d = w_down.shape[0]
    w_kr = w_down[:, MLA_Q_RANK + MLA_KV_RANK:]
    wd = jnp.concatenate([w_down[:, :MLA_Q_RANK + MLA_KV_RANK],
                          _head_lanes(jnp.zeros((d, MLA_NOPE_DIM), F32), w_kr)], axis=1)

    wq = w_uq.reshape(MLA_Q_RANK, MLA_HEADS, MLA_NOPE_DIM + MLA_ROPE_DIM)
    wq = _head_lanes(wq[..., :MLA_NOPE_DIM], wq[..., MLA_NOPE_DIM:]).reshape(MLA_Q_RANK, MLA_HEADS * LANES)

    wkv = w_ukv.reshape(MLA_KV_RANK, MLA_HEADS, MLA_NOPE_DIM + MLA_V_DIM)
    wk_nope, wv = wkv[..., :MLA_NOPE_DIM], wkv[..., MLA_NOPE_DIM:]
    wk = _head_lanes(wk_nope, jnp.zeros((MLA_KV_RANK, MLA_HEADS, MLA_ROPE_DIM), F32))
    wk = wk.reshape(MLA_KV_RANK, MLA_HEADS * LANES)
    wv = wv.reshape(MLA_KV_RANK, MLA_HEADS * MLA_V_DIM)
    return wd.astype(BF16), wq.astype(BF16), wk.astype(BF16), wv.astype(BF16)


def _rope_dense_layout(positions):
    half = MLA_ROPE_DIM // 2
    n_blk = LANES // half
    freqs = ROPE_THETA ** (-jnp.arange(half, dtype=F32) / half)
    order = np.array([(-b) % n_blk for b in range(n_blk)])
    pos_dense = jnp.repeat(positions.reshape(-1, n_blk)[:, order], half, axis=1)
    return pos_dense, jnp.tile(freqs, n_blk)[None, :]


def kernel(x, positions, attn_norm, ffn_norm, even_w_in, even_sinks, even_w_out, rel_bias_table, mla_w_down, mla_q_norm, mla_w_uq, mla_kv_norm, mla_w_ukv, mla_w_o, ffn_w_gate, ffn_w_up, ffn_w_down, final_norm):
    b, s, d = x.shape
    t = b * s
    x2 = x.reshape(t, d)
    row = lambda v: v.reshape(1, -1)

    proj = _norm_proj(x2, row(attn_norm[0]), even_w_in[0].astype(BF16), tm=1024)
    proj3 = proj.reshape(b, s, EVEN_IN_WIDTH)
    o_a = _sb_attention(proj3, sub=256, group=1024)
    o_b = _swa_attention(proj3, rel_bias_table, even_sinks[0].reshape(1, SW_HEADS), qb=8)
    wg, wu, wd_ffn = _to_bf16(ffn_w_gate, tr=256), _to_bf16(ffn_w_up, tr=256), _to_bf16(ffn_w_down, tr=704)
    x2 = _out_ffn(x2, [o_a.reshape(t, SB_WIDTH), o_b.reshape(t, SW_Q_WIDTH)], even_w_out[0].astype(BF16),
                  row(ffn_norm[0]), wg, wu, wd_ffn, row(final_norm), layer=0, tm=512, final_norm=False)

    wd, wq, wk, wv = _mla_weight_layout(mla_w_down[0], mla_w_uq[0], mla_w_ukv[0])
    pos_dense, freq_dense = _rope_dense_layout(positions)
    q, k, v = _mla_proj(x2, pos_dense, freq_dense, row(attn_norm[1]), wd,
                        row(mla_q_norm[0]), wq, row(mla_kv_norm[0]), wk, wv, tm=512)
    o = _mla_attention(q.reshape(b, s, -1), k.reshape(b, s, -1), v.reshape(b, s, -1), tq=4096, tk=512, unroll=2)
    x2 = _out_ffn(x2, [o.reshape(t, MLA_HEADS * MLA_V_DIM)], mla_w_o[0].astype(BF16),
                  row(ffn_norm[1]), wg, wu, wd_ffn, row(final_norm), layer=1, tm=512, final_norm=True)
    return x2.reshape(b, s, d)
```

```python
import functools
import math

import numpy as np
import jax
import jax.numpy as jnp
from jax import lax
from jax.experimental import pallas as pl
from jax.experimental.pallas import tpu as pltpu

F32 = jnp.float32
BF16 = jnp.bfloat16

D_MODEL = 1024
EPS = 1e-6
HEAD_DIM = 64
LANES = 128
SB_HEADS = 8
SW_HEADS = 8
SW_KV_HEADS = 2
WINDOW = 128
SB_WIDTH = SB_HEADS * HEAD_DIM
SW_Q_WIDTH = SW_HEADS * HEAD_DIM
SW_KV_WIDTH = SW_KV_HEADS * HEAD_DIM
EVEN_IN_WIDTH = 3 * SB_WIDTH + SW_Q_WIDTH + 2 * SW_KV_WIDTH
REL_BUCKETS = 32
REL_MAX_DIST = 128
MLA_HEADS = 16
MLA_NOPE_DIM = 64
MLA_ROPE_DIM = 32
MLA_V_DIM = 64
MLA_Q_RANK = 384
MLA_KV_RANK = 256
ROPE_THETA = 10000.0
LOG2E = math.log2(math.e)
SB_SATURATED = 128.0
VMEM_LIMIT_BYTES = 56 * 1024 * 1024


def _cparams(n_axes):
    return pltpu.CompilerParams(
        dimension_semantics=("arbitrary",) * n_axes,
        vmem_limit_bytes=VMEM_LIMIT_BYTES)


def _rms(x, g):
    return x * lax.rsqrt(jnp.mean(x * x, axis=-1, keepdims=True) + EPS) * g


def _dot(a, b):
    return jnp.dot(a, b, preferred_element_type=F32)


def _dot_nt(a, b):
    return lax.dot_general(a, b, (((1,), (1,)), ((), ())), preferred_element_type=F32)


def _norm_proj_kernel(x_ref, g_ref, w_ref, o_ref):
    h = _rms(x_ref[...], g_ref[...]).astype(BF16)
    o_ref[...] = _dot(h, w_ref[...]).astype(o_ref.dtype)


def _norm_proj(x2, g, w, *, tm):
    t, d = x2.shape
    n = w.shape[1]
    return pl.pallas_call(
        _norm_proj_kernel,
        out_shape=jax.ShapeDtypeStruct((t, n), BF16),
        grid=(t // tm,),
        in_specs=[pl.BlockSpec((tm, d), lambda i: (i, 0)),
                  pl.BlockSpec((1, d), lambda i: (0, 0)),
                  pl.BlockSpec((d, n), lambda i: (0, 0))],
        out_specs=pl.BlockSpec((tm, n), lambda i: (i, 0)),
        compiler_params=_cparams(1),
        name="norm_proj",
    )(x2, g, w)


def _softplus(z):
    return jnp.maximum(z, 0.0) + jnp.log(1.0 + jnp.exp2(jnp.abs(z) * (-LOG2E)))


def _sb_kernel(q_ref, k_ref, v_ref, o_ref, r_ref, acc_ref, *, sub, group):
    tq = q_ref.shape[1]
    r = tq // sub
    q2 = q_ref[0]
    lane = lax.broadcasted_iota(jnp.int32, (tq, LANES), 1)
    low = lane < HEAD_DIM
    zero = jnp.zeros_like(q2)
    scale = HEAD_DIM ** -0.5
    qm = (jnp.where(low, q2, zero) * scale, jnp.where(low, zero, q2) * scale)

    tri_row = lax.broadcasted_iota(jnp.int32, (sub, sub), 0)
    tri_col = lax.broadcasted_iota(jnp.int32, (sub, sub), 1)
    strictly_lower = tri_col < tri_row
    tri = strictly_lower.astype(BF16)

    r_ref[...] = jnp.zeros(r_ref.shape, F32)
    acc_ref[...] = jnp.zeros(acc_ref.shape, F32)

    def tile(hh, r0, nrows, j, masked, first_row=None):
        rows = pl.ds(r0, nrows)
        start = pl.multiple_of(j * sub, sub)
        kblk = k_ref[0, pl.ds(start, sub), :]
        vblk = v_ref[0, pl.ds(start, sub), :]
        z = _dot_nt(qm[hh][r0:r0 + nrows], kblk)
        if first_row is not None:
            row = lax.broadcasted_iota(jnp.int32, (nrows, sub), 0) + r0
            z = jnp.where(row >= first_row, z, -jnp.inf)
        if masked:
            top = jnp.where(strictly_lower, z[:sub], -jnp.inf)
            z = top if nrows == sub else jnp.concatenate([top, z[sub:]], axis=0)
        sp = _softplus(z)
        log_beta = z - sp
        later = r_ref[hh, rows, :]
        e = _dot(sp.astype(BF16), tri) + jnp.tile(later, (1, sub // LANES))
        w = jnp.exp(log_beta - e)
        acc_ref[hh, rows, :] += _dot(w.astype(BF16), vblk)
        r_ref[hh, rows, :] = jnp.broadcast_to(e[:, 0:1] + sp[:, 0:1], (nrows, LANES))

    def unsaturated(first_row):
        return jnp.min(r_ref[:, first_row:, :]) < SB_SATURATED

    band = 2 * sub
    for u in reversed(range(r)):
        for hh in range(2):
            tile(hh, u * sub, min(band, tq - u * sub), u, True)

    @pl.when(unsaturated(band))
    def _():
        def body(jj, carry):
            j = r - 3 - jj
            first_row = (j + 2) * sub
            for g in range(tq // group):
                g0 = max(g * group, band)

                @pl.when(jnp.logical_and(first_row // group == g, unsaturated(g0)))
                def _():
                    for hh in range(2):
                        tile(hh, g * group, tq - g * group, j, False, first_row=first_row)
            return carry

        lax.fori_loop(0, r - 2, body, 0)

    o_ref[0] = jnp.where(low, acc_ref[0], acc_ref[1]).astype(o_ref.dtype)


def _sb_attention(proj3, *, sub, group):
    b, s, _ = proj3.shape
    n_pairs = SB_WIDTH // LANES
    k_off = SB_WIDTH // LANES
    v_off = 2 * SB_WIDTH // LANES
    seq = lambda c0: pl.BlockSpec((1, s, LANES), lambda bi, p: (bi, 0, c0 + p))
    return pl.pallas_call(
        functools.partial(_sb_kernel, sub=sub, group=group),
        out_shape=jax.ShapeDtypeStruct((b, s, SB_WIDTH), BF16),
        grid=(b, n_pairs),
        in_specs=[seq(0), seq(k_off), seq(v_off)],
        out_specs=seq(0),
        scratch_shapes=[pltpu.VMEM((2, s, LANES), F32), pltpu.VMEM((2, s, LANES), F32)],
        compiler_params=_cparams(2),
        name="sb_attention",
    )(proj3, proj3, proj3)


def _t5_bucket_table():
    max_exact = REL_BUCKETS // 2
    rel = WINDOW + np.arange(WINDOW)[:, None] - np.arange(2 * WINDOW)[None, :]
    rel = np.maximum(rel, 0)
    relf = np.maximum(rel, 1).astype(np.float32)
    large = max_exact + (np.log(relf / np.float32(max_exact)) / np.float32(math.log(REL_MAX_DIST / max_exact))
                         * np.float32(REL_BUCKETS - max_exact)).astype(np.int32)
    large = np.minimum(large, REL_BUCKETS - 1)
    return np.where(rel < max_exact, rel, large).astype(np.int32)


def _swap_halves(x):
    return jnp.concatenate([x[:, HEAD_DIM:], x[:, :HEAD_DIM]], axis=1)


def _swa_kernel(table_ref, sinks_ref, bucket_ref, q_ref, kp_ref, kc_ref, vp_ref, vc_ref,
                o_ref, bias_ref, *, qb):
    bi = pl.program_id(0)
    n = pl.program_id(1)
    blk = WINDOW
    t = lax.broadcasted_iota(jnp.int32, (blk, 2 * blk), 0)
    s = lax.broadcasted_iota(jnp.int32, (blk, 2 * blk), 1)

    @pl.when(jnp.logical_and(bi == 0, n == 0))
    def _():
        bucket = bucket_ref[...]
        rel = blk + t - s
        in_window = (rel >= 0) & (rel < WINDOW)
        for h in range(SW_HEADS):
            bias = jnp.zeros((blk, 2 * blk), F32)
            for bk in range(REL_BUCKETS):
                bias = jnp.where(bucket == bk, table_ref[bk, h], bias)
            bias_ref[h] = jnp.where(in_window, bias, -jnp.inf)

    kk = jnp.concatenate([kp_ref[0], kc_ref[0]], axis=0)
    vv = jnp.concatenate([vp_ref[0], vc_ref[0]], axis=0)
    kk_sw = _swap_halves(kk)
    vv_sw = _swap_halves(vv)
    lane = lax.broadcasted_iota(jnp.int32, (blk, LANES), 1)
    low = lane < HEAD_DIM
    scale = HEAD_DIM ** -0.5
    group = SW_HEADS // SW_KV_HEADS
    has_prev = (s >= blk) | (n > 0)

    for j in range(qb):
        rows = slice(j * blk, (j + 1) * blk)
        keys = slice(j * blk, (j + 2) * blk)
        for p in range(SW_HEADS // 2):
            q2 = q_ref[0, rows, p * LANES:(p + 1) * LANES]
            zero = jnp.zeros_like(q2)
            outs = []
            for half in range(2):
                h = 2 * p + half
                kv = h // group
                qh = (jnp.where(low, q2, zero) if half == 0 else jnp.where(low, zero, q2)) * scale
                k_use = (kk if kv == half else kk_sw)[keys]
                v_use = (vv if kv == half else vv_sw)[keys]
                logits = _dot_nt(qh, k_use) + bias_ref[h]
                if j == 0:
                    logits = jnp.where(has_prev, logits, -jnp.inf)
                sink = sinks_ref[0, h]
                m = jnp.maximum(jnp.max(logits, axis=-1, keepdims=True), sink)
                pr = jnp.exp(logits - m)
                denom = jnp.sum(pr, axis=-1, keepdims=True) + jnp.exp(sink - m)
                outs.append(_dot(pr.astype(BF16), v_use) / denom)
            o_ref[0, rows, p * LANES:(p + 1) * LANES] = jnp.where(low, outs[0], outs[1]).astype(o_ref.dtype)


def _swa_attention(proj3, table, sinks, *, qb):
    b, s, _ = proj3.shape
    rows = qb * WINDOW
    q_blk = (3 * SB_WIDTH) // SW_Q_WIDTH
    k_blk = (3 * SB_WIDTH + SW_Q_WIDTH) // LANES
    v_blk = k_blk + 1
    bucket = jnp.asarray(_t5_bucket_table())
    smem = pl.BlockSpec(memory_space=pltpu.SMEM)
    prev = lambda c: (lambda bi, n: (bi, jnp.maximum(n * qb - 1, 0), c))
    cur = lambda c: (lambda bi, n: (bi, n, c))
    return pl.pallas_call(
        functools.partial(_swa_kernel, qb=qb),
        out_shape=jax.ShapeDtypeStruct((b, s, SW_Q_WIDTH), BF16),
        grid=(b, s // rows),
        in_specs=[smem, smem,
                  pl.BlockSpec((WINDOW, 2 * WINDOW), lambda bi, n: (0, 0)),
                  pl.BlockSpec((1, rows, SW_Q_WIDTH), cur(q_blk)),
                  pl.BlockSpec((1, WINDOW, LANES), prev(k_blk)),
                  pl.BlockSpec((1, rows, LANES), cur(k_blk)),
                  pl.BlockSpec((1, WINDOW, LANES), prev(v_blk)),
                  pl.BlockSpec((1, rows, LANES), cur(v_blk))],
        out_specs=pl.BlockSpec((1, rows, SW_Q_WIDTH), lambda bi, n: (bi, n, 0)),
        scratch_shapes=[pltpu.VMEM((SW_HEADS, WINDOW, 2 * WINDOW), F32)],
        compiler_params=_cparams(2),
        name="swa_attention",
    )(table, sinks, bucket, proj3, proj3, proj3, proj3, proj3)


def _to_bf16_kernel(w_ref, o_ref):
    o_ref[...] = w_ref[...].astype(BF16)


def _to_bf16(w, *, tr):
    n, r, c = w.shape
    spec = pl.BlockSpec((1, tr, c), lambda l, i: (l, i, 0))
    return pl.pallas_call(
        _to_bf16_kernel,
        out_shape=jax.ShapeDtypeStruct(w.shape, BF16),
        grid=(n, r // tr),
        in_specs=[spec],
        out_specs=spec,
        compiler_params=_cparams(2),
        name="to_bf16",
    )(w)


def _out_ffn_kernel(*refs, n_parts, final_norm):
    x_ref = refs[0]
    part_refs = refs[1:1 + n_parts]
    wo_refs = refs[1 + n_parts:1 + 2 * n_parts]
    g_ref, wg_ref, wu_ref, wd_ref, fg_ref, o_ref = refs[1 + 2 * n_parts:]
    x = x_ref[...]
    for a_ref, w_ref in zip(part_refs, wo_refs):
        x = x + _dot(a_ref[...], w_ref[...])
    h = _rms(x, g_ref[...]).astype(BF16)
    gate = _dot(h, wg_ref[...])
    up = _dot(h, wu_ref[...])
    act = (gate / (1.0 + jnp.exp(-gate)) * up).astype(BF16)
    y = x + _dot(act, wd_ref[...])
    if final_norm:
        y = _rms(y, fg_ref[...])
    o_ref[...] = y


def _out_ffn(x2, parts, w_out, g, wg, wu, wd, fg, *, layer, tm, final_norm):
    t, d = x2.shape
    hid = wg.shape[2]
    n_parts = len(parts)
    kp = w_out.shape[0] // n_parts
    resident = dict(pipeline_mode=pl.Buffered(1))
    row_tile = lambda width: pl.BlockSpec((tm, width), lambda i: (i, 0))
    const = lambda r, c: pl.BlockSpec((None, r, c), lambda i: (layer, 0, 0), **resident)
    in_specs = [row_tile(d)] + [row_tile(kp) for _ in parts]
    in_specs += [pl.BlockSpec((kp, d), functools.partial(lambda i, c: (c, 0), c=c), **resident)
                 for c in range(n_parts)]
    in_specs += [pl.BlockSpec((1, d), lambda i: (0, 0)), const(d, hid), const(d, hid),
                 const(hid, d), pl.BlockSpec((1, d), lambda i: (0, 0))]
    return pl.pallas_call(
        functools.partial(_out_ffn_kernel, n_parts=n_parts, final_norm=final_norm),
        out_shape=jax.ShapeDtypeStruct((t, d), F32),
        grid=(t // tm,),
        in_specs=in_specs,
        out_specs=row_tile(d),
        compiler_params=_cparams(1),
        name="out_ffn_final" if final_norm else "out_ffn",
    )(x2, *parts, *([w_out] * n_parts), g, wg, wu, wd, fg)


def _mla_proj_kernel(x_ref, pos_ref, freq_ref, g_ref, wd_ref, qg_ref, wq_ref,
                     kg_ref, wk_ref, wv_ref, q_ref, k_ref, v_ref):
    tm = x_ref.shape[0]
    h = _rms(x_ref[...], g_ref[...]).astype(BF16)
    down = _dot(h, wd_ref[...])
    c_q = down[:, :MLA_Q_RANK]
    c_kv = down[:, MLA_Q_RANK:MLA_Q_RANK + MLA_KV_RANK]
    kr = down[:, MLA_Q_RANK + MLA_KV_RANK:]

    half = MLA_ROPE_DIM // 2
    ang = pos_ref[...].astype(F32) * freq_ref[...]
    lane = lax.broadcasted_iota(jnp.int32, (tm, LANES), 1)
    first = lane < half
    second = (lane >= LANES // 2) & (lane < LANES // 2 + half)

    def spread(dense):
        rows8 = jnp.broadcast_to(dense[:, None, :], (tm // 8, 8, LANES)).reshape(tm, LANES)
        y = pltpu.roll(rows8, 0, 1, stride=half, stride_axis=0)
        return y, pltpu.roll(y, LANES // 2, 1)

    cos_1, cos_2 = spread(jnp.cos(ang))
    sin_1, sin_2 = spread(jnp.sin(ang))
    cos_p = jnp.where(first, cos_1, jnp.where(second, cos_2, 1.0))
    sin_p = jnp.where(first, -sin_1, jnp.where(second, sin_2, 0.0))

    def rope(blk):
        return blk * cos_p + pltpu.roll(blk, LANES // 2, 1) * sin_p

    cqn = _rms(c_q, qg_ref[...]).astype(BF16)
    ckvn = _rms(c_kv, kg_ref[...]).astype(BF16)
    k_rope = rope(kr)
    q = _dot(cqn, wq_ref[...])
    kn = _dot(ckvn, wk_ref[...])
    for hd in range(MLA_HEADS):
        sl = slice(hd * LANES, (hd + 1) * LANES)
        q_ref[:, sl] = rope(q[:, sl]).astype(q_ref.dtype)
        k_ref[:, sl] = (kn[:, sl] + k_rope).astype(k_ref.dtype)
    v_ref[...] = _dot(ckvn, wv_ref[...]).astype(v_ref.dtype)


def _mla_proj(x2, pos2, freq, g, wd, qg, wq, kg, wk, wv, *, tm):
    t, d = x2.shape
    full = lambda a: pl.BlockSpec(a.shape, lambda i: (0,) * a.ndim)
    qk_w = MLA_HEADS * LANES
    v_w = MLA_HEADS * MLA_V_DIM
    return pl.pallas_call(
        _mla_proj_kernel,
        out_shape=(jax.ShapeDtypeStruct((t, qk_w), BF16),
                   jax.ShapeDtypeStruct((t, qk_w), BF16),
                   jax.ShapeDtypeStruct((t, v_w), BF16)),
        grid=(t // tm,),
        in_specs=[pl.BlockSpec((tm, d), lambda i: (i, 0)),
                  pl.BlockSpec((tm // 8, LANES), lambda i: (i, 0)),
                  full(freq), full(g), full(wd), full(qg), full(wq),
                  full(kg), full(wk), full(wv)],
        out_specs=(pl.BlockSpec((tm, qk_w), lambda i: (i, 0)),
                   pl.BlockSpec((tm, qk_w), lambda i: (i, 0)),
                   pl.BlockSpec((tm, v_w), lambda i: (i, 0))),
        compiler_params=_cparams(1),
        name="mla_proj",
    )(x2, pos2, freq, g, wd, qg, wq, kg, wk, wv)


def _mla_attn_kernel(q_ref, k_ref, v_ref, o_ref, m_ref, acc_ref, *, tk):
    s_len = q_ref.shape[1]
    r = s_len // tk
    v_all = v_ref[0]
    lane_s = lax.broadcasted_iota(jnp.int32, v_all.shape, 1)
    one = jnp.ones_like(v_all)
    vaug = (jnp.where(lane_s < MLA_V_DIM, v_all, one), jnp.where(lane_s < MLA_V_DIM, one, v_all))

    lower = (lax.broadcasted_iota(jnp.int32, (tk, tk), 1)
             <= lax.broadcasted_iota(jnp.int32, (tk, tk), 0))
    c2 = (MLA_NOPE_DIM + MLA_ROPE_DIM) ** -0.5 * LOG2E

    def tile(hh, u):
        r0 = u * tk
        rows = pl.ds(r0, s_len - r0)
        q = q_ref[0, rows, hh * LANES:(hh + 1) * LANES]
        kblk = k_ref[0, r0:r0 + tk, hh * LANES:(hh + 1) * LANES]
        vblk = vaug[hh][r0:r0 + tk]
        s2 = _dot_nt(q, kblk) * c2
        top = jnp.where(lower, s2[:tk], -jnp.inf)
        s2 = top if u == r - 1 else jnp.concatenate([top, s2[tk:]], axis=0)
        row_max = jnp.max(s2, axis=-1, keepdims=True)
        if u == 0:
            m_new = jnp.broadcast_to(row_max, (s_len, LANES))
            p = jnp.exp2(s2 - jnp.tile(m_new, (1, tk // LANES)))
            acc_ref[hh, rows, :] = _dot(p.astype(BF16), vblk)
        else:
            m = m_ref[hh, rows, :]
            m_new = jnp.maximum(m, row_max)
            alpha = jnp.exp2(m - m_new)
            p = jnp.exp2(s2 - jnp.tile(m_new, (1, tk // LANES)))
            acc_ref[hh, rows, :] = alpha * acc_ref[hh, rows, :] + _dot(p.astype(BF16), vblk)
        m_ref[hh, rows, :] = m_new

    for u in range(r):
        for hh in range(2):
            tile(hh, u)

    lane = lax.broadcasted_iota(jnp.int32, (s_len, LANES), 1)
    acc_a = acc_ref[0]
    acc_b = acc_ref[1]
    out_a = acc_a / pltpu.roll(acc_a, MLA_V_DIM, 1)
    out_b = acc_b / pltpu.roll(acc_b, MLA_V_DIM, 1)
    o_ref[0] = jnp.where(lane < MLA_V_DIM, out_a, out_b).astype(o_ref.dtype)


def _mla_attention(q3, k3, v3, *, tk):
    b, s, _ = q3.shape
    n_pairs = MLA_HEADS // 2
    seq = lambda width: pl.BlockSpec((1, s, width), lambda bi, p: (bi, 0, p))
    return pl.pallas_call(
        functools.partial(_mla_attn_kernel, tk=tk),
        out_shape=jax.ShapeDtypeStruct((b, s, MLA_HEADS * MLA_V_DIM), BF16),
        grid=(b, n_pairs),
        in_specs=[seq(2 * LANES), seq(2 * LANES), seq(LANES)],
        out_specs=seq(LANES),
        scratch_shapes=[pltpu.VMEM((2, s, LANES), F32), pltpu.VMEM((2, s, LANES), F32)],
        compiler_params=_cparams(2),
        name="mla_attention",
    )(q3, k3, v3)


def _head_lanes(nope, rope):
    half = MLA_ROPE_DIM // 2
    split = LANES // 2 - half
    zero = jnp.zeros(nope.shape[:-1] + (LANES - MLA_NOPE_DIM - MLA_ROPE_DIM,), nope.dtype)
    return jnp.concatenate([rope[..., :half], nope[..., :split], rope[..., half:], nope[..., split:], zero],
                           axis=-1)


def _mla_weight_layout(w_down, w_uq, w_ukv):
    d = w_down.shape[0]
    w_kr = w_down[:, MLA_Q_RANK + MLA_KV_RANK:]
    wd = jnp.concatenate([w_down[:, :MLA_Q_RANK + MLA_KV_RANK],
                          _head_lanes(jnp.zeros((d, MLA_NOPE_DIM), F32), w_kr)], axis=1)

    wq = w_uq.reshape(MLA_Q_RANK, MLA_HEADS, MLA_NOPE_DIM + MLA_ROPE_DIM)
    wq = _head_lanes(wq[..., :MLA_NOPE_DIM], wq[..., MLA_NOPE_DIM:]).reshape(MLA_Q_RANK, MLA_HEADS * LANES)

    wkv = w_ukv.reshape(MLA_KV_RANK, MLA_HEADS, MLA_NOPE_DIM + MLA_V_DIM)
    wk_nope, wv = wkv[..., :MLA_NOPE_DIM], wkv[..., MLA_NOPE_DIM:]
    wk = _head_lanes(wk_nope, jnp.zeros((MLA_KV_RANK, MLA_HEADS, MLA_ROPE_DIM), F32))
    wk = wk.reshape(MLA_KV_RANK, MLA_HEADS * LANES)
    wv = wv.reshape(MLA_KV_RANK, MLA_HEADS * MLA_V_DIM)
    return wd.astype(BF16), wq.astype(BF16), wk.astype(BF16), wv.astype(BF16)


def _rope_dense_layout(positions):
    half = MLA_ROPE_DIM // 2
    n_blk = LANES // half
    freqs = ROPE_THETA ** (-jnp.arange(half, dtype=F32) / half)
    order = np.array([(-b) % n_blk for b in range(n_blk)])
    pos_dense = jnp.repeat(positions.reshape(-1, n_blk)[:, order], half, axis=1)
    return pos_dense, jnp.tile(freqs, n_blk)[None, :]


def kernel(x, positions, attn_norm, ffn_norm, even_w_in, even_sinks, even_w_out, rel_bias_table, mla_w_down, mla_q_norm, mla_w_uq, mla_kv_norm, mla_w_ukv, mla_w_o, ffn_w_gate, ffn_w_up, ffn_w_down, final_norm):
    b, s, d = x.shape
    t = b * s
    x2 = x.reshape(t, d)
    row = lambda v: v.reshape(1, -1)

    proj = _norm_proj(x2, row(attn_norm[0]), even_w_in[0].astype(BF16), tm=1024)
    proj3 = proj.reshape(b, s, EVEN_IN_WIDTH)
    o_a = _sb_attention(proj3, sub=256, group=1024)
    o_b = _swa_attention(proj3, rel_bias_table, even_sinks[0].reshape(1, SW_HEADS), qb=8)
    wg, wu, wd_ffn = _to_bf16(ffn_w_gate, tr=256), _to_bf16(ffn_w_up, tr=256), _to_bf16(ffn_w_down, tr=704)
    x2 = _out_ffn(x2, [o_a.reshape(t, SB_WIDTH), o_b.reshape(t, SW_Q_WIDTH)], even_w_out[0].astype(BF16),
                  row(ffn_norm[0]), wg, wu, wd_ffn, row(final_norm), layer=0, tm=512, final_norm=False)

    wd, wq, wk, wv = _mla_weight_layout(mla_w_down[0], mla_w_uq[0], mla_w_ukv[0])
    pos_dense, freq_dense = _rope_dense_layout(positions)
    q, k, v = _mla_proj(x2, pos_dense, freq_dense, row(attn_norm[1]), wd,
                        row(mla_q_norm[0]), wq, row(mla_kv_norm[0]), wk, wv, tm=512)
    o = _mla_attention(q.reshape(b, s, -1), k.reshape(b, s, -1), v.reshape(b, s, -1), tk=512)
    x2 = _out_ffn(x2, [o.reshape(t, MLA_HEADS * MLA_V_DIM)], mla_w_o[0].astype(BF16),
                  row(ffn_norm[1]), wg, wu, wd_ffn, row(final_norm), layer=1, tm=512, final_norm=True)
    return x2.reshape(b, s, d)
```

```python
import functools
import math

import numpy as np
import jax
import jax.numpy as jnp
from jax import lax
from jax.experimental import pallas as pl
from jax.experimental.pallas import tpu as pltpu

F32 = jnp.float32
BF16 = jnp.bfloat16

D_MODEL = 1024
EPS = 1e-6
HEAD_DIM = 64
LANES = 128
SB_HEADS = 8
SW_HEADS = 8
SW_KV_HEADS = 2
WINDOW = 128
SB_WIDTH = SB_HEADS * HEAD_DIM
SW_Q_WIDTH = SW_HEADS * HEAD_DIM
SW_KV_WIDTH = SW_KV_HEADS * HEAD_DIM
EVEN_IN_WIDTH = 3 * SB_WIDTH + SW_Q_WIDTH + 2 * SW_KV_WIDTH
REL_BUCKETS = 32
REL_MAX_DIST = 128
MLA_HEADS = 16
MLA_NOPE_DIM = 64
MLA_ROPE_DIM = 32
MLA_V_DIM = 64
MLA_Q_RANK = 384
MLA_KV_RANK = 256
ROPE_THETA = 10000.0
LOG2E = math.log2(math.e)
SB_SATURATED = 128.0
VMEM_LIMIT_BYTES = 56 * 1024 * 1024


def _cparams(n_axes):
    return pltpu.CompilerParams(
        dimension_semantics=("arbitrary",) * n_axes,
        vmem_limit_bytes=VMEM_LIMIT_BYTES)


def _rms(x, g):
    return x * lax.rsqrt(jnp.mean(x * x, axis=-1, keepdims=True) + EPS) * g


def _dot(a, b):
    return jnp.dot(a, b, preferred_element_type=F32)


def _dot_nt(a, b):
    return lax.dot_general(a, b, (((1,), (1,)), ((), ())), preferred_element_type=F32)


def _norm_proj_kernel(x_ref, g_ref, w_ref, o_ref):
    h = _rms(x_ref[...], g_ref[...]).astype(BF16)
    o_ref[...] = _dot(h, w_ref[...]).astype(o_ref.dtype)


def _norm_proj(x2, g, w, *, tm):
    t, d = x2.shape
    n = w.shape[1]
    return pl.pallas_call(
        _norm_proj_kernel,
        out_shape=jax.ShapeDtypeStruct((t, n), BF16),
        grid=(t // tm,),
        in_specs=[pl.BlockSpec((tm, d), lambda i: (i, 0)),
                  pl.BlockSpec((1, d), lambda i: (0, 0)),
                  pl.BlockSpec((d, n), lambda i: (0, 0))],
        out_specs=pl.BlockSpec((tm, n), lambda i: (i, 0)),
        compiler_params=_cparams(1),
        name="norm_proj",
    )(x2, g, w)


def _softplus(z):
    return jnp.maximum(z, 0.0) + jnp.log(1.0 + jnp.exp2(jnp.abs(z) * (-LOG2E)))


def _sb_kernel(q_ref, k_ref, v_ref, o_ref, r_ref, acc_ref, *, sub, group):
    tq = q_ref.shape[1]
    r = tq // sub
    q2 = q_ref[0]
    lane = lax.broadcasted_iota(jnp.int32, (tq, LANES), 1)
    low = lane < HEAD_DIM
    zero = jnp.zeros_like(q2)
    scale = HEAD_DIM ** -0.5
    qm = (jnp.where(low, q2, zero) * scale, jnp.where(low, zero, q2) * scale)

    tri_row = lax.broadcasted_iota(jnp.int32, (sub, sub), 0)
    tri_col = lax.broadcasted_iota(jnp.int32, (sub, sub), 1)
    strictly_lower = tri_col < tri_row
    tri = strictly_lower.astype(BF16)

    r_ref[...] = jnp.zeros(r_ref.shape, F32)
    acc_ref[...] = jnp.zeros(acc_ref.shape, F32)

    def tile(hh, r0, nrows, j, masked, first_row=None):
        rows = pl.ds(r0, nrows)
        start = pl.multiple_of(j * sub, sub)
        kblk = k_ref[0, pl.ds(start, sub), :]
        vblk = v_ref[0, pl.ds(start, sub), :]
        z = _dot_nt(qm[hh][r0:r0 + nrows], kblk)
        if first_row is not None:
            row = lax.broadcasted_iota(jnp.int32, (nrows, sub), 0) + r0
            z = jnp.where(row >= first_row, z, -jnp.inf)
        if masked:
            top = jnp.where(strictly_lower, z[:sub], -jnp.inf)
            z = top if nrows == sub else jnp.concatenate([top, z[sub:]], axis=0)
        sp = _softplus(z)
        log_beta = z - sp
        later = r_ref[hh, rows, :]
        e = _dot(sp.astype(BF16), tri) + jnp.tile(later, (1, sub // LANES))
        w = jnp.exp(log_beta - e)
        acc_ref[hh, rows, :] += _dot(w.astype(BF16), vblk)
        r_ref[hh, rows, :] = jnp.broadcast_to(e[:, 0:1] + sp[:, 0:1], (nrows, LANES))

    def unsaturated(first_row):
        return jnp.min(r_ref[:, first_row:, :]) < SB_SATURATED

    band = 2 * sub
    for u in reversed(range(r)):
        for hh in range(2):
            tile(hh, u * sub, min(band, tq - u * sub), u, True)

    @pl.when(unsaturated(band))
    def _():
        def body(jj, carry):
            j = r - 3 - jj
            first_row = (j + 2) * sub
            for g in range(tq // group):
                g0 = max(g * group, band)

                @pl.when(jnp.logical_and(first_row // group == g, unsaturated(g0)))
                def _():
                    for hh in range(2):
                        tile(hh, g * group, tq - g * group, j, False, first_row=first_row)
            return carry

        lax.fori_loop(0, r - 2, body, 0)

    o_ref[0] = jnp.where(low, acc_ref[0], acc_ref[1]).astype(o_ref.dtype)


def _sb_attention(proj3, *, sub, group):
    b, s, _ = proj3.shape
    n_pairs = SB_WIDTH // LANES
    k_off = SB_WIDTH // LANES
    v_off = 2 * SB_WIDTH // LANES
    seq = lambda c0: pl.BlockSpec((1, s, LANES), lambda bi, p: (bi, 0, c0 + p))
    return pl.pallas_call(
        functools.partial(_sb_kernel, sub=sub, group=group),
        out_shape=jax.ShapeDtypeStruct((b, s, SB_WIDTH), BF16),
        grid=(b, n_pairs),
        in_specs=[seq(0), seq(k_off), seq(v_off)],
        out_specs=seq(0),
        scratch_shapes=[pltpu.VMEM((2, s, LANES), F32), pltpu.VMEM((2, s, LANES), F32)],
        compiler_params=_cparams(2),
        name="sb_attention",
    )(proj3, proj3, proj3)


def _t5_bucket_table():
    max_exact = REL_BUCKETS // 2
    rel = WINDOW + np.arange(WINDOW)[:, None] - np.arange(2 * WINDOW)[None, :]
    rel = np.maximum(rel, 0)
    relf = np.maximum(rel, 1).astype(np.float32)
    large = max_exact + (np.log(relf / np.float32(max_exact)) / np.float32(math.log(REL_MAX_DIST / max_exact))
                         * np.float32(REL_BUCKETS - max_exact)).astype(np.int32)
    large = np.minimum(large, REL_BUCKETS - 1)
    return np.where(rel < max_exact, rel, large).astype(np.int32)


def _swap_halves(x):
    return jnp.concatenate([x[:, HEAD_DIM:], x[:, :HEAD_DIM]], axis=1)


def _swa_kernel(table_ref, sinks_ref, bucket_ref, q_ref, kp_ref, kc_ref, vp_ref, vc_ref,
                o_ref, bias_ref, *, qb):
    bi = pl.program_id(0)
    n = pl.program_id(1)
    blk = WINDOW
    t = lax.broadcasted_iota(jnp.int32, (blk, 2 * blk), 0)
    s = lax.broadcasted_iota(jnp.int32, (blk, 2 * blk), 1)

    @pl.when(jnp.logical_and(bi == 0, n == 0))
    def _():
        bucket = bucket_ref[...]
        rel = blk + t - s
        in_window = (rel >= 0) & (rel < WINDOW)
        for h in range(SW_HEADS):
            bias = jnp.zeros((blk, 2 * blk), F32)
            for bk in range(REL_BUCKETS):
                bias = jnp.where(bucket == bk, table_ref[bk, h], bias)
            bias_ref[h] = jnp.where(in_window, bias, -jnp.inf)

    kk = jnp.concatenate([kp_ref[0], kc_ref[0]], axis=0)
    vv = jnp.concatenate([vp_ref[0], vc_ref[0]], axis=0)
    kk_sw = _swap_halves(kk)
    vv_sw = _swap_halves(vv)
    lane = lax.broadcasted_iota(jnp.int32, (blk, LANES), 1)
    low = lane < HEAD_DIM
    scale = HEAD_DIM ** -0.5
    group = SW_HEADS // SW_KV_HEADS
    has_prev = (s >= blk) | (n > 0)

    for j in range(qb):
        rows = slice(j * blk, (j + 1) * blk)
        keys = slice(j * blk, (j + 2) * blk)
        for p in range(SW_HEADS // 2):
            q2 = q_ref[0, rows, p * LANES:(p + 1) * LANES]
            zero = jnp.zeros_like(q2)
            outs = []
            for half in range(2):
                h = 2 * p + half
                kv = h // group
                qh = (jnp.where(low, q2, zero) if half == 0 else jnp.where(low, zero, q2)) * scale
                k_use = (kk if kv == half else kk_sw)[keys]
                v_use = (vv if kv == half else vv_sw)[keys]
                logits = _dot_nt(qh, k_use) + bias_ref[h]
                if j == 0:
                    logits = jnp.where(has_prev, logits, -jnp.inf)
                sink = sinks_ref[0, h]
                m = jnp.maximum(jnp.max(logits, axis=-1, keepdims=True), sink)
                pr = jnp.exp(logits - m)
                denom = jnp.sum(pr, axis=-1, keepdims=True) + jnp.exp(sink - m)
                outs.append(_dot(pr.astype(BF16), v_use) / denom)
            o_ref[0, rows, p * LANES:(p + 1) * LANES] = jnp.where(low, outs[0], outs[1]).astype(o_ref.dtype)


def _swa_attention(proj3, table, sinks, *, qb):
    b, s, _ = proj3.shape
    rows = qb * WINDOW
    q_blk = (3 * SB_WIDTH) // SW_Q_WIDTH
    k_blk = (3 * SB_WIDTH + SW_Q_WIDTH) // LANES
    v_blk = k_blk + 1
    bucket = jnp.asarray(_t5_bucket_table())
    smem = pl.BlockSpec(memory_space=pltpu.SMEM)
    prev = lambda c: (lambda bi, n: (bi, jnp.maximum(n * qb - 1, 0), c))
    cur = lambda c: (lambda bi, n: (bi, n, c))
    return pl.pallas_call(
        functools.partial(_swa_kernel, qb=qb),
        out_shape=jax.ShapeDtypeStruct((b, s, SW_Q_WIDTH), BF16),
        grid=(b, s // rows),
        in_specs=[smem, smem,
                  pl.BlockSpec((WINDOW, 2 * WINDOW), lambda bi, n: (0, 0)),
                  pl.BlockSpec((1, rows, SW_Q_WIDTH), cur(q_blk)),
                  pl.BlockSpec((1, WINDOW, LANES), prev(k_blk)),
                  pl.BlockSpec((1, rows, LANES), cur(k_blk)),
                  pl.BlockSpec((1, WINDOW, LANES), prev(v_blk)),
                  pl.BlockSpec((1, rows, LANES), cur(v_blk))],
        out_specs=pl.BlockSpec((1, rows, SW_Q_WIDTH), lambda bi, n: (bi, n, 0)),
        scratch_shapes=[pltpu.VMEM((SW_HEADS, WINDOW, 2 * WINDOW), F32)],
        compiler_params=_cparams(2),
        name="swa_attention",
    )(table, sinks, bucket, proj3, proj3, proj3, proj3, proj3)


def _to_bf16_kernel(w_ref, o_ref):
    o_ref[...] = w_ref[...].astype(BF16)


def _to_bf16(w, *, tr):
    n, r, c = w.shape
    spec = pl.BlockSpec((1, tr, c), lambda l, i: (l, i, 0))
    return pl.pallas_call(
        _to_bf16_kernel,
        out_shape=jax.ShapeDtypeStruct(w.shape, BF16),
        grid=(n, r // tr),
        in_specs=[spec],
        out_specs=spec,
        compiler_params=_cparams(2),
        name="to_bf16",
    )(w)


def _out_ffn_kernel(*refs, n_parts, final_norm):
    x_ref = refs[0]
    part_refs = refs[1:1 + n_parts]
    wo_refs = refs[1 + n_parts:1 + 2 * n_parts]
    g_ref, wg_ref, wu_ref, wd_ref, fg_ref, o_ref = refs[1 + 2 * n_parts:]
    x = x_ref[...]
    for a_ref, w_ref in zip(part_refs, wo_refs):
        x = x + _dot(a_ref[...], w_ref[...])
    h = _rms(x, g_ref[...]).astype(BF16)
    gate = _dot(h, wg_ref[...])
    up = _dot(h, wu_ref[...])
    act = (gate / (1.0 + jnp.exp(-gate)) * up).astype(BF16)
    y = x + _dot(act, wd_ref[...])
    if final_norm:
        y = _rms(y, fg_ref[...])
    o_ref[...] = y


def _out_ffn(x2, parts, w_out, g, wg, wu, wd, fg, *, layer, tm, final_norm):
    t, d = x2.shape
    hid = wg.shape[2]
    n_parts = len(parts)
    kp = w_out.shape[0] // n_parts
    resident = dict(pipeline_mode=pl.Buffered(1))
    row_tile = lambda width: pl.BlockSpec((tm, width), lambda i: (i, 0))
    const = lambda r, c: pl.BlockSpec((None, r, c), lambda i: (layer, 0, 0), **resident)
    in_specs = [row_tile(d)] + [row_tile(kp) for _ in parts]
    in_specs += [pl.BlockSpec((kp, d), functools.partial(lambda i, c: (c, 0), c=c), **resident)
                 for c in range(n_parts)]
    in_specs += [pl.BlockSpec((1, d), lambda i: (0, 0)), const(d, hid), const(d, hid),
                 const(hid, d), pl.BlockSpec((1, d), lambda i: (0, 0))]
    return pl.pallas_call(
        functools.partial(_out_ffn_kernel, n_parts=n_parts, final_norm=final_norm),
        out_shape=jax.ShapeDtypeStruct((t, d), F32),
        grid=(t // tm,),
        in_specs=in_specs,
        out_specs=row_tile(d),
        compiler_params=_cparams(1),
        name="out_ffn_final" if final_norm else "out_ffn",
    )(x2, *parts, *([w_out] * n_parts), g, wg, wu, wd, fg)


def _mla_proj_kernel(x_ref, pos_ref, freq_ref, g_ref, wd_ref, qg_ref, wq_ref,
                     kg_ref, wk_ref, wv_ref, q_ref, k_ref, v_ref):
    tm = x_ref.shape[0]
    h = _rms(x_ref[...], g_ref[...]).astype(BF16)
    down = _dot(h, wd_ref[...])
    c_q = down[:, :MLA_Q_RANK]
    c_kv = down[:, MLA_Q_RANK:MLA_Q_RANK + MLA_KV_RANK]
    kr = down[:, MLA_Q_RANK + MLA_KV_RANK:]

    half = MLA_ROPE_DIM // 2
    ang = pos_ref[...].astype(F32) * freq_ref[...]
    lane = lax.broadcasted_iota(jnp.int32, (tm, LANES), 1)
    first = lane < half
    second = (lane >= LANES // 2) & (lane < LANES // 2 + half)

    def spread(dense):
        rows8 = jnp.broadcast_to(dense[:, None, :], (tm // 8, 8, LANES)).reshape(tm, LANES)
        y = pltpu.roll(rows8, 0, 1, stride=half, stride_axis=0)
        return y, pltpu.roll(y, LANES // 2, 1)

    cos_1, cos_2 = spread(jnp.cos(ang))
    sin_1, sin_2 = spread(jnp.sin(ang))
    cos_p = jnp.where(first, cos_1, jnp.where(second, cos_2, 1.0))
    sin_p = jnp.where(first, -sin_1, jnp.where(second, sin_2, 0.0))

    def rope(blk):
        return blk * cos_p + pltpu.roll(blk, LANES // 2, 1) * sin_p

    cqn = _rms(c_q, qg_ref[...]).astype(BF16)
    ckvn = _rms(c_kv, kg_ref[...]).astype(BF16)
    k_rope = rope(kr)
    q = _dot(cqn, wq_ref[...])
    kn = _dot(ckvn, wk_ref[...])
    for hd in range(MLA_HEADS):
        sl = slice(hd * LANES, (hd + 1) * LANES)
        q_ref[:, sl] = rope(q[:, sl]).astype(q_ref.dtype)
        k_ref[:, sl] = (kn[:, sl] + k_rope).astype(k_ref.dtype)
    v_ref[...] = _dot(ckvn, wv_ref[...]).astype(v_ref.dtype)


def _mla_proj(x2, pos2, freq, g, wd, qg, wq, kg, wk, wv, *, tm):
    t, d = x2.shape
    full = lambda a: pl.BlockSpec(a.shape, lambda i: (0,) * a.ndim)
    qk_w = MLA_HEADS * LANES
    v_w = MLA_HEADS * MLA_V_DIM
    return pl.pallas_call(
        _mla_proj_kernel,
        out_shape=(jax.ShapeDtypeStruct((t, qk_w), BF16),
                   jax.ShapeDtypeStruct((t, qk_w), BF16),
                   jax.ShapeDtypeStruct((t, v_w), BF16)),
        grid=(t // tm,),
        in_specs=[pl.BlockSpec((tm, d), lambda i: (i, 0)),
                  pl.BlockSpec((tm // 8, LANES), lambda i: (i, 0)),
                  full(freq), full(g), full(wd), full(qg), full(wq),
                  full(kg), full(wk), full(wv)],
        out_specs=(pl.BlockSpec((tm, qk_w), lambda i: (i, 0)),
                   pl.BlockSpec((tm, qk_w), lambda i: (i, 0)),
                   pl.BlockSpec((tm, v_w), lambda i: (i, 0))),
        compiler_params=_cparams(1),
        name="mla_proj",
    )(x2, pos2, freq, g, wd, qg, wq, kg, wk, wv)


def _mla_attn_kernel(q_ref, k_ref, v_ref, o_ref, va_ref, vb_ref, m_ref, acc_ref, *, tq, tk, unroll):
    qi = pl.program_id(2)
    lane_s = lax.broadcasted_iota(jnp.int32, va_ref.shape, 1)

    @pl.when(qi == 0)
    def _():
        v_all = v_ref[0]
        one = jnp.ones_like(v_all)
        va_ref[...] = jnp.where(lane_s < MLA_V_DIM, v_all, one)
        vb_ref[...] = jnp.where(lane_s < MLA_V_DIM, one, v_all)

    r = tq // tk
    lower = (lax.broadcasted_iota(jnp.int32, (tk, tk), 1)
             <= lax.broadcasted_iota(jnp.int32, (tk, tk), 0))
    c2 = (MLA_NOPE_DIM + MLA_ROPE_DIM) ** -0.5 * LOG2E
    vaug = (va_ref, vb_ref)

    m_ref[...] = jnp.full(m_ref.shape, -jnp.inf, F32)
    acc_ref[...] = jnp.zeros(acc_ref.shape, F32)

    def tile(hh, r0, nrows, j, masked):
        rows = pl.ds(r0, nrows)
        q = q_ref[0, rows, hh * LANES:(hh + 1) * LANES]
        start = pl.multiple_of(j * tk, tk)
        kblk = k_ref[0, pl.ds(start, tk), hh * LANES:(hh + 1) * LANES]
        vblk = vaug[hh][pl.ds(start, tk), :]
        s2 = _dot_nt(q, kblk) * c2
        if masked:
            top = jnp.where(lower, s2[:tk], -jnp.inf)
            s2 = top if nrows == tk else jnp.concatenate([top, s2[tk:]], axis=0)
        m = m_ref[hh, rows, :]
        m_new = jnp.maximum(m, jnp.max(s2, axis=-1, keepdims=True))
        alpha = jnp.exp2(m - m_new)
        p = jnp.exp2(s2 - jnp.tile(m_new, (1, tk // LANES)))
        acc_ref[hh, rows, :] = alpha * acc_ref[hh, rows, :] + _dot(p.astype(BF16), vblk)
        m_ref[hh, rows, :] = m_new

    def body(jj, carry):
        for t in range(unroll):
            for hh in range(2):
                tile(hh, 0, tq, jj * unroll + t, False)
        return carry

    lax.fori_loop(0, (qi * r) // unroll, body, 0)
    for u in range(r):
        for hh in range(2):
            tile(hh, u * tk, tq - u * tk, qi * r + u, True)

    lane = lax.broadcasted_iota(jnp.int32, (tq, LANES), 1)
    acc_a = acc_ref[0]
    acc_b = acc_ref[1]
    out_a = acc_a / pltpu.roll(acc_a, MLA_V_DIM, 1)
    out_b = acc_b / pltpu.roll(acc_b, MLA_V_DIM, 1)
    o_ref[0] = jnp.where(lane < MLA_V_DIM, out_a, out_b).astype(o_ref.dtype)


def _mla_attention(q3, k3, v3, *, tq, tk, unroll):
    b, s, _ = q3.shape
    assert (tq // tk) % unroll == 0
    n_pairs = MLA_HEADS // 2
    return pl.pallas_call(
        functools.partial(_mla_attn_kernel, tq=tq, tk=tk, unroll=unroll),
        out_shape=jax.ShapeDtypeStruct((b, s, MLA_HEADS * MLA_V_DIM), BF16),
        grid=(b, n_pairs, s // tq),
        in_specs=[pl.BlockSpec((1, tq, 2 * LANES), lambda bi, p, i: (bi, i, p)),
                  pl.BlockSpec((1, s, 2 * LANES), lambda bi, p, i: (bi, 0, p)),
                  pl.BlockSpec((1, s, LANES), lambda bi, p, i: (bi, 0, p))],
        out_specs=pl.BlockSpec((1, tq, LANES), lambda bi, p, i: (bi, i, p)),
        scratch_shapes=[pltpu.VMEM((s, LANES), BF16), pltpu.VMEM((s, LANES), BF16),
                        pltpu.VMEM((2, tq, LANES), F32), pltpu.VMEM((2, tq, LANES), F32)],
        compiler_params=_cparams(3),
        name="mla_attention",
    )(q3, k3, v3)


def _head_lanes(nope, rope):
    half = MLA_ROPE_DIM // 2
    split = LANES // 2 - half
    zero = jnp.zeros(nope.shape[:-1] + (LANES - MLA_NOPE_DIM - MLA_ROPE_DIM,), nope.dtype)
    return jnp.concatenate([rope[..., :half], nope[..., :split], rope[..., half:], nope[..., split:], zero],
                           axis=-1)


def _mla_weight_layout(w_down, w_uq, w_ukv):
    d = w_down.shape[0]
    w_kr = w_down[:, MLA_Q_RANK + MLA_KV_RANK:]
    wd = jnp.concatenate([w_down[:, :MLA_Q_RANK + MLA_KV_RANK],
                          _head_lanes(jnp.zeros((d, MLA_NOPE_DIM), F32), w_kr)], axis=1)

    wq = w_uq.reshape(MLA_Q_RANK, MLA_HEADS, MLA_NOPE_DIM + MLA_ROPE_DIM)
    wq = _head_lanes(wq[..., :MLA_NOPE_DIM], wq[..., MLA_NOPE_DIM:]).reshape(MLA_Q_RANK, MLA_HEADS * LANES)

    wkv = w_ukv.reshape(MLA_KV_RANK, MLA_HEADS, MLA_NOPE_DIM + MLA_V_DIM)
    wk_nope, wv = wkv[..., :MLA_NOPE_DIM], wkv[..., MLA_NOPE_DIM:]
    wk = _head_lanes(wk_nope, jnp.zeros((MLA_KV_RANK, MLA_HEADS, MLA_ROPE_DIM), F32))
    wk = wk.reshape(MLA_KV_RANK, MLA_HEADS * LANES)
    wv = wv.reshape(MLA_KV_RANK, MLA_HEADS * MLA_V_DIM)
    return wd.astype(BF16), wq.astype(BF16), wk.astype(BF16), wv.astype(BF16)


def _rope_dense_layout(positions):
    half = MLA_ROPE_DIM // 2
    n_blk = LANES // half
    freqs = ROPE_THETA ** (-jnp.arange(half, dtype=F32) / half)
    order = np.array([(-b) % n_blk for b in range(n_blk)])
    pos_dense = jnp.repeat(positions.reshape(-1, n_blk)[:, order], half, axis=1)
    return pos_dense, jnp.tile(freqs, n_blk)[None, :]


def kernel(x, positions, attn_norm, ffn_norm, even_w_in, even_sinks, even_w_out, rel_bias_table, mla_w_down, mla_q_norm, mla_w_uq, mla_kv_norm, mla_w_ukv, mla_w_o, ffn_w_gate, ffn_w_up, ffn_w_down, final_norm):
    b, s, d = x.shape
    t = b * s
    x2 = x.reshape(t, d)
    row = lambda v: v.reshape(1, -1)

    proj = _norm_proj(x2, row(attn_norm[0]), even_w_in[0].astype(BF16), tm=1024)
    proj3 = proj.reshape(b, s, EVEN_IN_WIDTH)
    o_a = _sb_attention(proj3, sub=256, group=1024)
    o_b = _swa_attention(proj3, rel_bias_table, even_sinks[0].reshape(1, SW_HEADS), qb=8)
    wg, wu, wd_ffn = _to_bf16(ffn_w_gate, tr=256), _to_bf16(ffn_w_up, tr=256), _to_bf16(ffn_w_down, tr=704)
    x2 = _out_ffn(x2, [o_a.reshape(t, SB_WIDTH), o_b.reshape(t, SW_Q_WIDTH)], even_w_out[0].astype(BF16),
                  row(ffn_norm[0]), wg, wu, wd_ffn, row(final_norm), layer=0, tm=512, final_norm=False)

    wd, wq, wk, wv = _mla_weight_layout(mla_w_down[0], mla_w_uq[0], mla_w_ukv[0])
    pos_dense, freq_dense = _rope_dense_layout(positions)
    q, k, v = _mla_proj(x2, pos_dense, freq_dense, row(attn_norm[1]), wd,
                        row(mla_q_norm[0]), wq, row(mla_kv_norm[0]), wk, wv, tm=512)
    o = _mla_attention(q.reshape(b, s, -1), k.reshape(b, s, -1), v.reshape(b, s, -1), tq=4096, tk=256, unroll=2)
    x2 = _out_ffn(x2, [o.reshape(t, MLA_HEADS * MLA_V_DIM)], mla_w_o[0].astype(BF16),
                  row(ffn_norm[1]), wg, wu, wd_ffn, row(final_norm), layer=1, tm=512, final_norm=True)
    return x2.reshape(b, s, d)
```

```python
import functools
import math

import numpy as np
import jax
import jax.numpy as jnp
from jax import lax
from jax.experimental import pallas as pl
from jax.experimental.pallas import tpu as pltpu

F32 = jnp.float32
BF16 = jnp.bfloat16

D_MODEL = 1024
EPS = 1e-6
HEAD_DIM = 64
LANES = 128
SB_HEADS = 8
SW_HEADS = 8
SW_KV_HEADS = 2
WINDOW = 128
SB_WIDTH = SB_HEADS * HEAD_DIM
SW_Q_WIDTH = SW_HEADS * HEAD_DIM
SW_KV_WIDTH = SW_KV_HEADS * HEAD_DIM
EVEN_IN_WIDTH = 3 * SB_WIDTH + SW_Q_WIDTH + 2 * SW_KV_WIDTH
REL_BUCKETS = 32
REL_MAX_DIST = 128
MLA_HEADS = 16
MLA_NOPE_DIM = 64
MLA_ROPE_DIM = 32
MLA_V_DIM = 64
MLA_Q_RANK = 384
MLA_KV_RANK = 256
ROPE_THETA = 10000.0
LOG2E = math.log2(math.e)
SB_SATURATED = 128.0
VMEM_LIMIT_BYTES = 56 * 1024 * 1024


def _cparams(n_axes):
    return pltpu.CompilerParams(
        dimension_semantics=("arbitrary",) * n_axes,
        vmem_limit_bytes=VMEM_LIMIT_BYTES)


def _rms(x, g):
    return x * lax.rsqrt(jnp.mean(x * x, axis=-1, keepdims=True) + EPS) * g


def _dot(a, b):
    return jnp.dot(a, b, preferred_element_type=F32)


def _dot_nt(a, b):
    return lax.dot_general(a, b, (((1,), (1,)), ((), ())), preferred_element_type=F32)


def _norm_proj_kernel(x_ref, g_ref, w_ref, o_ref):
    h = _rms(x_ref[...], g_ref[...]).astype(BF16)
    o_ref[...] = _dot(h, w_ref[...]).astype(o_ref.dtype)


def _norm_proj(x2, g, w, *, tm):
    t, d = x2.shape
    n = w.shape[1]
    return pl.pallas_call(
        _norm_proj_kernel,
        out_shape=jax.ShapeDtypeStruct((t, n), BF16),
        grid=(t // tm,),
        in_specs=[pl.BlockSpec((tm, d), lambda i: (i, 0)),
                  pl.BlockSpec((1, d), lambda i: (0, 0)),
                  pl.BlockSpec((d, n), lambda i: (0, 0))],
        out_specs=pl.BlockSpec((tm, n), lambda i: (i, 0)),
        compiler_params=_cparams(1),
        name="norm_proj",
    )(x2, g, w)


def _softplus(z):
    return jnp.maximum(z, 0.0) + jnp.log(1.0 + jnp.exp2(jnp.abs(z) * (-LOG2E)))


def _sb_kernel(q_ref, k_ref, v_ref, o_ref, r_ref, acc_ref, *, sub, group):
    tq = q_ref.shape[1]
    r = tq // sub
    q2 = q_ref[0]
    lane = lax.broadcasted_iota(jnp.int32, (tq, LANES), 1)
    low = lane < HEAD_DIM
    zero = jnp.zeros_like(q2)
    scale = HEAD_DIM ** -0.5
    qm = (jnp.where(low, q2, zero) * scale, jnp.where(low, zero, q2) * scale)

    tri_row = lax.broadcasted_iota(jnp.int32, (sub, sub), 0)
    tri_col = lax.broadcasted_iota(jnp.int32, (sub, sub), 1)
    strictly_lower = tri_col < tri_row
    tri = strictly_lower.astype(BF16)

    def tile(hh, r0, nrows, j, masked, first_row=None, fresh=0):
        rows = pl.ds(r0, nrows)
        start = pl.multiple_of(j * sub, sub)
        kblk = k_ref[0, pl.ds(start, sub), :]
        vblk = v_ref[0, pl.ds(start, sub), :]
        z = _dot_nt(qm[hh][r0:r0 + nrows], kblk)
        if first_row is not None:
            row = lax.broadcasted_iota(jnp.int32, (nrows, sub), 0) + r0
            z = jnp.where(row >= first_row, z, -jnp.inf)
        if masked:
            top = jnp.where(strictly_lower, z[:sub], -jnp.inf)
            z = top if nrows == sub else jnp.concatenate([top, z[sub:]], axis=0)
        sp = _softplus(z)
        log_beta = z - sp
        old = pl.ds(r0 + fresh, nrows - fresh)
        later = r_ref[hh, old, :] if fresh < nrows else None
        if fresh:
            zeros = jnp.zeros((fresh, LANES), F32)
            later = zeros if later is None else jnp.concatenate([zeros, later], axis=0)
        e = _dot(sp.astype(BF16), tri) + jnp.tile(later, (1, sub // LANES))
        w = jnp.exp(log_beta - e)
        pv = _dot(w.astype(BF16), vblk)
        if fresh:
            acc_ref[hh, pl.ds(r0, fresh), :] = pv[:fresh]
        if fresh < nrows:
            acc_ref[hh, old, :] += pv[fresh:]
        r_ref[hh, rows, :] = jnp.broadcast_to(e[:, 0:1] + sp[:, 0:1], (nrows, LANES))

    def unsaturated(first_row):
        return jnp.min(r_ref[:, first_row:, :]) < SB_SATURATED

    band = 2 * sub
    for u in reversed(range(r)):
        for hh in range(2):
            tile(hh, u * sub, min(band, tq - u * sub), u, True, fresh=sub)

    @pl.when(unsaturated(band))
    def _():
        def body(jj, carry):
            j = r - 3 - jj
            first_row = (j + 2) * sub
            for g in range(tq // group):
                g0 = max(g * group, band)

                @pl.when(jnp.logical_and(first_row // group == g, unsaturated(g0)))
                def _():
                    for hh in range(2):
                        tile(hh, g * group, tq - g * group, j, False, first_row=first_row)
            return carry

        lax.fori_loop(0, r - 2, body, 0)

    o_ref[0] = jnp.where(low, acc_ref[0], acc_ref[1]).astype(o_ref.dtype)


def _sb_attention(proj3, *, sub, group):
    b, s, _ = proj3.shape
    n_pairs = SB_WIDTH // LANES
    k_off = SB_WIDTH // LANES
    v_off = 2 * SB_WIDTH // LANES
    seq = lambda c0: pl.BlockSpec((1, s, LANES), lambda bi, p: (bi, 0, c0 + p))
    return pl.pallas_call(
        functools.partial(_sb_kernel, sub=sub, group=group),
        out_shape=jax.ShapeDtypeStruct((b, s, SB_WIDTH), BF16),
        grid=(b, n_pairs),
        in_specs=[seq(0), seq(k_off), seq(v_off)],
        out_specs=seq(0),
        scratch_shapes=[pltpu.VMEM((2, s, LANES), F32), pltpu.VMEM((2, s, LANES), F32)],
        compiler_params=_cparams(2),
        name="sb_attention",
    )(proj3, proj3, proj3)


def _t5_bucket_table():
    max_exact = REL_BUCKETS // 2
    rel = WINDOW + np.arange(WINDOW)[:, None] - np.arange(2 * WINDOW)[None, :]
    rel = np.maximum(rel, 0)
    relf = np.maximum(rel, 1).astype(np.float32)
    large = max_exact + (np.log(relf / np.float32(max_exact)) / np.float32(math.log(REL_MAX_DIST / max_exact))
                         * np.float32(REL_BUCKETS - max_exact)).astype(np.int32)
    large = np.minimum(large, REL_BUCKETS - 1)
    return np.where(rel < max_exact, rel, large).astype(np.int32)


def _swap_halves(x):
    return jnp.concatenate([x[:, HEAD_DIM:], x[:, :HEAD_DIM]], axis=1)


def _swa_kernel(table_ref, sinks_ref, bucket_ref, q_ref, kp_ref, kc_ref, vp_ref, vc_ref,
                o_ref, bias_ref, *, qb):
    bi = pl.program_id(0)
    n = pl.program_id(1)
    blk = WINDOW
    t = lax.broadcasted_iota(jnp.int32, (blk, 2 * blk), 0)
    s = lax.broadcasted_iota(jnp.int32, (blk, 2 * blk), 1)

    @pl.when(jnp.logical_and(bi == 0, n == 0))
    def _():
        bucket = bucket_ref[...]
        rel = blk + t - s
        in_window = (rel >= 0) & (rel < WINDOW)
        for h in range(SW_HEADS):
            bias = jnp.zeros((blk, 2 * blk), F32)
            for bk in range(REL_BUCKETS):
                bias = jnp.where(bucket == bk, table_ref[bk, h], bias)
            bias_ref[h] = jnp.where(in_window, bias, -jnp.inf)

    kk = jnp.concatenate([kp_ref[0], kc_ref[0]], axis=0)
    vv = jnp.concatenate([vp_ref[0], vc_ref[0]], axis=0)
    kk_sw = _swap_halves(kk)
    vv_sw = _swap_halves(vv)
    lane = lax.broadcasted_iota(jnp.int32, (blk, LANES), 1)
    low = lane < HEAD_DIM
    scale = HEAD_DIM ** -0.5
    group = SW_HEADS // SW_KV_HEADS
    has_prev = (s >= blk) | (n > 0)

    for j in range(qb):
        rows = slice(j * blk, (j + 1) * blk)
        keys = slice(j * blk, (j + 2) * blk)
        for p in range(SW_HEADS // 2):
            q2 = q_ref[0, rows, p * LANES:(p + 1) * LANES]
            zero = jnp.zeros_like(q2)
            outs = []
            for half in range(2):
                h = 2 * p + half
                kv = h // group
                qh = (jnp.where(low, q2, zero) if half == 0 else jnp.where(low, zero, q2)) * scale
                k_use = (kk if kv == half else kk_sw)[keys]
                v_use = (vv if kv == half else vv_sw)[keys]
                logits = _dot_nt(qh, k_use) + bias_ref[h]
                if j == 0:
                    logits = jnp.where(has_prev, logits, -jnp.inf)
                sink = sinks_ref[0, h]
                m = jnp.maximum(jnp.max(logits, axis=-1, keepdims=True), sink)
                pr = jnp.exp(logits - m)
                denom = jnp.sum(pr, axis=-1, keepdims=True) + jnp.exp(sink - m)
                outs.append(_dot(pr.astype(BF16), v_use) / denom)
            o_ref[0, rows, p * LANES:(p + 1) * LANES] = jnp.where(low, outs[0], outs[1]).astype(o_ref.dtype)


def _swa_attention(proj3, table, sinks, *, qb):
    b, s, _ = proj3.shape
    rows = qb * WINDOW
    q_blk = (3 * SB_WIDTH) // SW_Q_WIDTH
    k_blk = (3 * SB_WIDTH + SW_Q_WIDTH) // LANES
    v_blk = k_blk + 1
    bucket = jnp.asarray(_t5_bucket_table())
    smem = pl.BlockSpec(memory_space=pltpu.SMEM)
    prev = lambda c: (lambda bi, n: (bi, jnp.maximum(n * qb - 1, 0), c))
    cur = lambda c: (lambda bi, n: (bi, n, c))
    return pl.pallas_call(
        functools.partial(_swa_kernel, qb=qb),
        out_shape=jax.ShapeDtypeStruct((b, s, SW_Q_WIDTH), BF16),
        grid=(b, s // rows),
        in_specs=[smem, smem,
                  pl.BlockSpec((WINDOW, 2 * WINDOW), lambda bi, n: (0, 0)),
                  pl.BlockSpec((1, rows, SW_Q_WIDTH), cur(q_blk)),
                  pl.BlockSpec((1, WINDOW, LANES), prev(k_blk)),
                  pl.BlockSpec((1, rows, LANES), cur(k_blk)),
                  pl.BlockSpec((1, WINDOW, LANES), prev(v_blk)),
                  pl.BlockSpec((1, rows, LANES), cur(v_blk))],
        out_specs=pl.BlockSpec((1, rows, SW_Q_WIDTH), lambda bi, n: (bi, n, 0)),
        scratch_shapes=[pltpu.VMEM((SW_HEADS, WINDOW, 2 * WINDOW), F32)],
        compiler_params=_cparams(2),
        name="swa_attention",
    )(table, sinks, bucket, proj3, proj3, proj3, proj3, proj3)


def _to_bf16_kernel(w_ref, o_ref):
    o_ref[...] = w_ref[...].astype(BF16)


def _to_bf16(w, *, tr):
    n, r, c = w.shape
    spec = pl.BlockSpec((1, tr, c), lambda l, i: (l, i, 0))
    return pl.pallas_call(
        _to_bf16_kernel,
        out_shape=jax.ShapeDtypeStruct(w.shape, BF16),
        grid=(n, r // tr),
        in_specs=[spec],
        out_specs=spec,
        compiler_params=_cparams(2),
        name="to_bf16",
    )(w)


def _out_ffn_kernel(*refs, n_parts, final_norm):
    x_ref = refs[0]
    part_refs = refs[1:1 + n_parts]
    wo_refs = refs[1 + n_parts:1 + 2 * n_parts]
    g_ref, wg_ref, wu_ref, wd_ref, fg_ref, o_ref = refs[1 + 2 * n_parts:]
    x = x_ref[...]
    for a_ref, w_ref in zip(part_refs, wo_refs):
        x = x + _dot(a_ref[...], w_ref[...])
    h = _rms(x, g_ref[...]).astype(BF16)
    gate = _dot(h, wg_ref[...])
    up = _dot(h, wu_ref[...])
    act = (gate / (1.0 + jnp.exp(-gate)) * up).astype(BF16)
    y = x + _dot(act, wd_ref[...])
    if final_norm:
        y = _rms(y, fg_ref[...])
    o_ref[...] = y


def _out_ffn(x2, parts, w_out, g, wg, wu, wd, fg, *, layer, tm, final_norm):
    t, d = x2.shape
    hid = wg.shape[2]
    n_parts = len(parts)
    kp = w_out.shape[0] // n_parts
    resident = dict(pipeline_mode=pl.Buffered(1))
    row_tile = lambda width: pl.BlockSpec((tm, width), lambda i: (i, 0))
    const = lambda r, c: pl.BlockSpec((None, r, c), lambda i: (layer, 0, 0), **resident)
    in_specs = [row_tile(d)] + [row_tile(kp) for _ in parts]
    in_specs += [pl.BlockSpec((kp, d), functools.partial(lambda i, c: (c, 0), c=c), **resident)
                 for c in range(n_parts)]
    in_specs += [pl.BlockSpec((1, d), lambda i: (0, 0)), const(d, hid), const(d, hid),
                 const(hid, d), pl.BlockSpec((1, d), lambda i: (0, 0))]
    return pl.pallas_call(
        functools.partial(_out_ffn_kernel, n_parts=n_parts, final_norm=final_norm),
        out_shape=jax.ShapeDtypeStruct((t, d), F32),
        grid=(t // tm,),
        in_specs=in_specs,
        out_specs=row_tile(d),
        compiler_params=_cparams(1),
        name="out_ffn_final" if final_norm else "out_ffn",
    )(x2, *parts, *([w_out] * n_parts), g, wg, wu, wd, fg)


def _mla_proj_kernel(x_ref, pos_ref, freq_ref, g_ref, wd_ref, qg_ref, wq_ref,
                     kg_ref, wk_ref, wv_ref, q_ref, k_ref, v_ref):
    tm = x_ref.shape[0]
    h = _rms(x_ref[...], g_ref[...]).astype(BF16)
    down = _dot(h, wd_ref[...])
    c_q = down[:, :MLA_Q_RANK]
    c_kv = down[:, MLA_Q_RANK:MLA_Q_RANK + MLA_KV_RANK]
    kr = down[:, MLA_Q_RANK + MLA_KV_RANK:]

    half = MLA_ROPE_DIM // 2
    ang = pos_ref[...].astype(F32) * freq_ref[...]
    lane = lax.broadcasted_iota(jnp.int32, (tm, LANES), 1)
    first = lane < half
    second = (lane >= LANES // 2) & (lane < LANES // 2 + half)

    def spread(dense):
        rows8 = jnp.broadcast_to(dense[:, None, :], (tm // 8, 8, LANES)).reshape(tm, LANES)
        y = pltpu.roll(rows8, 0, 1, stride=half, stride_axis=0)
        return y, pltpu.roll(y, LANES // 2, 1)

    cos_1, cos_2 = spread(jnp.cos(ang))
    sin_1, sin_2 = spread(jnp.sin(ang))
    cos_p = jnp.where(first, cos_1, jnp.where(second, cos_2, 1.0))
    sin_p = jnp.where(first, -sin_1, jnp.where(second, sin_2, 0.0))

    def rope(blk):
        return blk * cos_p + pltpu.roll(blk, LANES // 2, 1) * sin_p

    cqn = _rms(c_q, qg_ref[...]).astype(BF16)
    ckvn = _rms(c_kv, kg_ref[...]).astype(BF16)
    k_rope = rope(kr)
    q = _dot(cqn, wq_ref[...])
    kn = _dot(ckvn, wk_ref[...])
    for hd in range(MLA_HEADS):
        sl = slice(hd * LANES, (hd + 1) * LANES)
        q_ref[:, sl] = rope(q[:, sl]).astype(q_ref.dtype)
        k_ref[:, sl] = (kn[:, sl] + k_rope).astype(k_ref.dtype)
    v_ref[...] = _dot(ckvn, wv_ref[...]).astype(v_ref.dtype)


def _mla_proj(x2, pos2, freq, g, wd, qg, wq, kg, wk, wv, *, tm):
    t, d = x2.shape
    full = lambda a: pl.BlockSpec(a.shape, lambda i: (0,) * a.ndim)
    qk_w = MLA_HEADS * LANES
    v_w = MLA_HEADS * MLA_V_DIM
    return pl.pallas_call(
        _mla_proj_kernel,
        out_shape=(jax.ShapeDtypeStruct((t, qk_w), BF16),
                   jax.ShapeDtypeStruct((t, qk_w), BF16),
                   jax.ShapeDtypeStruct((t, v_w), BF16)),
        grid=(t // tm,),
        in_specs=[pl.BlockSpec((tm, d), lambda i: (i, 0)),
                  pl.BlockSpec((tm // 8, LANES), lambda i: (i, 0)),
                  full(freq), full(g), full(wd), full(qg), full(wq),
                  full(kg), full(wk), full(wv)],
        out_specs=(pl.BlockSpec((tm, qk_w), lambda i: (i, 0)),
                   pl.BlockSpec((tm, qk_w), lambda i: (i, 0)),
                   pl.BlockSpec((tm, v_w), lambda i: (i, 0))),
        compiler_params=_cparams(1),
        name="mla_proj",
    )(x2, pos2, freq, g, wd, qg, wq, kg, wk, wv)


def _mla_attn_kernel(q_ref, k_ref, v_ref, o_ref, va_ref, vb_ref, m_ref, acc_ref, *, tq, tk, unroll):
    qi = pl.program_id(2)
    lane_s = lax.broadcasted_iota(jnp.int32, va_ref.shape, 1)

    @pl.when(qi == 0)
    def _():
        v_all = v_ref[0]
        one = jnp.ones_like(v_all)
        va_ref[...] = jnp.where(lane_s < MLA_V_DIM, v_all, one)
        vb_ref[...] = jnp.where(lane_s < MLA_V_DIM, one, v_all)

    r = tq // tk
    lower = (lax.broadcasted_iota(jnp.int32, (tk, tk), 1)
             <= lax.broadcasted_iota(jnp.int32, (tk, tk), 0))
    c2 = (MLA_NOPE_DIM + MLA_ROPE_DIM) ** -0.5 * LOG2E
    vaug = (va_ref, vb_ref)

    m_ref[...] = jnp.full(m_ref.shape, -jnp.inf, F32)
    acc_ref[...] = jnp.zeros(acc_ref.shape, F32)

    def tile(hh, r0, nrows, j, masked):
        rows = pl.ds(r0, nrows)
        q = q_ref[0, rows, hh * LANES:(hh + 1) * LANES]
        start = pl.multiple_of(j * tk, tk)
        kblk = k_ref[0, pl.ds(start, tk), hh * LANES:(hh + 1) * LANES]
        vblk = vaug[hh][pl.ds(start, tk), :]
        s2 = _dot_nt(q, kblk) * c2
        if masked:
            top = jnp.where(lower, s2[:tk], -jnp.inf)
            s2 = top if nrows == tk else jnp.concatenate([top, s2[tk:]], axis=0)
        m = m_ref[hh, rows, :]
        m_new = jnp.maximum(m, jnp.max(s2, axis=-1, keepdims=True))
        alpha = jnp.exp2(m - m_new)
        p = jnp.exp2(s2 - jnp.tile(m_new, (1, tk // LANES)))
        acc_ref[hh, rows, :] = alpha * acc_ref[hh, rows, :] + _dot(p.astype(BF16), vblk)
        m_ref[hh, rows, :] = m_new

    def body(jj, carry):
        for t in range(unroll):
            for hh in range(2):
                tile(hh, 0, tq, jj * unroll + t, False)
        return carry

    lax.fori_loop(0, (qi * r) // unroll, body, 0)
    for u in range(r):
        for hh in range(2):
            tile(hh, u * tk, tq - u * tk, qi * r + u, True)

    lane = lax.broadcasted_iota(jnp.int32, (tq, LANES), 1)
    acc_a = acc_ref[0]
    acc_b = acc_ref[1]
    out_a = acc_a / pltpu.roll(acc_a, MLA_V_DIM, 1)
    out_b = acc_b / pltpu.roll(acc_b, MLA_V_DIM, 1)
    o_ref[0] = jnp.where(lane < MLA_V_DIM, out_a, out_b).astype(o_ref.dtype)


def _mla_attention(q3, k3, v3, *, tq, tk, unroll):
    b, s, _ = q3.shape
    assert (tq // tk) % unroll == 0
    n_pairs = MLA_HEADS // 2
    return pl.pallas_call(
        functools.partial(_mla_attn_kernel, tq=tq, tk=tk, unroll=unroll),
        out_shape=jax.ShapeDtypeStruct((b, s, MLA_HEADS * MLA_V_DIM), BF16),
        grid=(b, n_pairs, s // tq),
        in_specs=[pl.BlockSpec((1, tq, 2 * LANES), lambda bi, p, i: (bi, i, p)),
                  pl.BlockSpec((1, s, 2 * LANES), lambda bi, p, i: (bi, 0, p)),
                  pl.BlockSpec((1, s, LANES), lambda bi, p, i: (bi, 0, p))],
        out_specs=pl.BlockSpec((1, tq, LANES), lambda bi, p, i: (bi, i, p)),
        scratch_shapes=[pltpu.VMEM((s, LANES), BF16), pltpu.VMEM((s, LANES), BF16),
                        pltpu.VMEM((2, tq, LANES), F32), pltpu.VMEM((2, tq, LANES), F32)],
        compiler_params=_cparams(3),
        name="mla_attention",
    )(q3, k3, v3)


def _head_lanes(nope, rope):
    half = MLA_ROPE_DIM // 2
    split = LANES // 2 - half
    zero = jnp.zeros(nope.shape[:-1] + (LANES - MLA_NOPE_DIM - MLA_ROPE_DIM,), nope.dtype)
    return jnp.concatenate([rope[..., :half], nope[..., :split], rope[..., half:], nope[..., split:], zero],
                           axis=-1)


def _mla_weight_layout(w_down, w_uq, w_ukv):
    d = w_down.shape[0]
    w_kr = w_down[:, MLA_Q_RANK + MLA_KV_RANK:]
    wd = jnp.concatenate([w_down[:, :MLA_Q_RANK + MLA_KV_RANK],
                          _head_lanes(jnp.zeros((d, MLA_NOPE_DIM), F32), w_kr)], axis=1)

    wq = w_uq.reshape(MLA_Q_RANK, MLA_HEADS, MLA_NOPE_DIM + MLA_ROPE_DIM)
    wq = _head_lanes(wq[..., :MLA_NOPE_DIM], wq[..., MLA_NOPE_DIM:]).reshape(MLA_Q_RANK, MLA_HEADS * LANES)

    wkv = w_ukv.reshape(MLA_KV_RANK, MLA_HEADS, MLA_NOPE_DIM + MLA_V_DIM)
    wk_nope, wv = wkv[..., :MLA_NOPE_DIM], wkv[..., MLA_NOPE_DIM:]
    wk = _head_lanes(wk_nope, jnp.zeros((MLA_KV_RANK, MLA_HEADS, MLA_ROPE_DIM), F32))
    wk = wk.reshape(MLA_KV_RANK, MLA_HEADS * LANES)
    wv = wv.reshape(MLA_KV_RANK, MLA_HEADS * MLA_V_DIM)
    return wd.astype(BF16), wq.astype(BF16), wk.astype(BF16), wv.astype(BF16)


def _rope_dense_layout(positions):
    half = MLA_ROPE_DIM // 2
    n_blk = LANES // half
    freqs = ROPE_THETA ** (-jnp.arange(half, dtype=F32) / half)
    order = np.array([(-b) % n_blk for b in range(n_blk)])
    pos_dense = jnp.repeat(positions.reshape(-1, n_blk)[:, order], half, axis=1)
    return pos_dense, jnp.tile(freqs, n_blk)[None, :]


def kernel(x, positions, attn_norm, ffn_norm, even_w_in, even_sinks, even_w_out, rel_bias_table, mla_w_down, mla_q_norm, mla_w_uq, mla_kv_norm, mla_w_ukv, mla_w_o, ffn_w_gate, ffn_w_up, ffn_w_down, final_norm):
    b, s, d = x.shape
    t = b * s
    x2 = x.reshape(t, d)
    row = lambda v: v.reshape(1, -1)

    proj = _norm_proj(x2, row(attn_norm[0]), even_w_in[0].astype(BF16), tm=1024)
    proj3 = proj.reshape(b, s, EVEN_IN_WIDTH)
    o_a = _sb_attention(proj3, sub=256, group=1024)
    o_b = _swa_attention(proj3, rel_bias_table, even_sinks[0].reshape(1, SW_HEADS), qb=8)
    wg, wu, wd_ffn = _to_bf16(ffn_w_gate, tr=256), _to_bf16(ffn_w_up, tr=256), _to_bf16(ffn_w_down, tr=704)
    x2 = _out_ffn(x2, [o_a.reshape(t, SB_WIDTH), o_b.reshape(t, SW_Q_WIDTH)], even_w_out[0].astype(BF16),
                  row(ffn_norm[0]), wg, wu, wd_ffn, row(final_norm), layer=0, tm=512, final_norm=False)

    wd, wq, wk, wv = _mla_weight_layout(mla_w_down[0], mla_w_uq[0], mla_w_ukv[0])
    pos_dense, freq_dense = _rope_dense_layout(positions)
    q, k, v = _mla_proj(x2, pos_dense, freq_dense, row(attn_norm[1]), wd,
                        row(mla_q_norm[0]), wq, row(mla_kv_norm[0]), wk, wv, tm=512)
    o = _mla_attention(q.reshape(b, s, -1), k.reshape(b, s, -1), v.reshape(b, s, -1), tq=4096, tk=512, unroll=2)
    x2 = _out_ffn(x2, [o.reshape(t, MLA_HEADS * MLA_V_DIM)], mla_w_o[0].astype(BF16),
                  row(ffn_norm[1]), wg, wu, wd_ffn, row(final_norm), layer=1, tm=512, final_norm=True)
    return x2.reshape(b, s, d)
```

```python
import functools
import math

import numpy as np
import jax
import jax.numpy as jnp
from jax import lax
from jax.experimental import pallas as pl
from jax.experimental.pallas import tpu as pltpu

F32 = jnp.float32
BF16 = jnp.bfloat16

D_MODEL = 1024
EPS = 1e-6
HEAD_DIM = 64
LANES = 128
SB_HEADS = 8
SW_HEADS = 8
SW_KV_HEADS = 2
WINDOW = 128
SB_WIDTH = SB_HEADS * HEAD_DIM
SW_Q_WIDTH = SW_HEADS * HEAD_DIM
SW_KV_WIDTH = SW_KV_HEADS * HEAD_DIM
EVEN_IN_WIDTH = 3 * SB_WIDTH + SW_Q_WIDTH + 2 * SW_KV_WIDTH
REL_BUCKETS = 32
REL_MAX_DIST = 128
MLA_HEADS = 16
MLA_NOPE_DIM = 64
MLA_ROPE_DIM = 32
MLA_V_DIM = 64
MLA_Q_RANK = 384
MLA_KV_RANK = 256
ROPE_THETA = 10000.0
LOG2E = math.log2(math.e)
SB_SATURATED = 128.0
VMEM_LIMIT_BYTES = 56 * 1024 * 1024


def _cparams(n_axes):
    return pltpu.CompilerParams(
        dimension_semantics=("arbitrary",) * n_axes,
        vmem_limit_bytes=VMEM_LIMIT_BYTES)


def _rms(x, g):
    return x * lax.rsqrt(jnp.mean(x * x, axis=-1, keepdims=True) + EPS) * g


def _dot(a, b):
    return jnp.dot(a, b, preferred_element_type=F32)


def _dot_nt(a, b):
    return lax.dot_general(a, b, (((1,), (1,)), ((), ())), preferred_element_type=F32)


def _norm_proj_kernel(x_ref, g_ref, w_ref, o_ref):
    h = _rms(x_ref[...], g_ref[...]).astype(BF16)
    o_ref[...] = _dot(h, w_ref[...]).astype(o_ref.dtype)


def _norm_proj(x2, g, w, *, tm):
    t, d = x2.shape
    n = w.shape[1]
    return pl.pallas_call(
        _norm_proj_kernel,
        out_shape=jax.ShapeDtypeStruct((t, n), BF16),
        grid=(t // tm,),
        in_specs=[pl.BlockSpec((tm, d), lambda i: (i, 0)),
                  pl.BlockSpec((1, d), lambda i: (0, 0)),
                  pl.BlockSpec((d, n), lambda i: (0, 0))],
        out_specs=pl.BlockSpec((tm, n), lambda i: (i, 0)),
        compiler_params=_cparams(1),
        name="norm_proj",
    )(x2, g, w)


def _softplus(z):
    return jnp.maximum(z, 0.0) + jnp.log(1.0 + jnp.exp2(jnp.abs(z) * (-LOG2E)))


def _sb_kernel(q_ref, k_ref, v_ref, o_ref, r_ref, acc_ref, *, sub, group):
    tq = q_ref.shape[1]
    r = tq // sub
    q2 = q_ref[0]
    lane = lax.broadcasted_iota(jnp.int32, (tq, LANES), 1)
    low = lane < HEAD_DIM
    zero = jnp.zeros_like(q2)
    scale = HEAD_DIM ** -0.5
    qm = (jnp.where(low, q2, zero) * scale, jnp.where(low, zero, q2) * scale)

    tri_row = lax.broadcasted_iota(jnp.int32, (sub, sub), 0)
    tri_col = lax.broadcasted_iota(jnp.int32, (sub, sub), 1)
    strictly_lower = tri_col < tri_row
    tri = strictly_lower.astype(BF16)

    def tile(hh, r0, nrows, j, masked, first_row=None, fresh=0):
        rows = pl.ds(r0, nrows)
        start = pl.multiple_of(j * sub, sub)
        kblk = k_ref[0, pl.ds(start, sub), :]
        vblk = v_ref[0, pl.ds(start, sub), :]
        z = _dot_nt(qm[hh][r0:r0 + nrows], kblk)
        if first_row is not None:
            row = lax.broadcasted_iota(jnp.int32, (nrows, sub), 0) + r0
            z = jnp.where(row >= first_row, z, -jnp.inf)
        if masked:
            top = jnp.where(strictly_lower, z[:sub], -jnp.inf)
            z = top if nrows == sub else jnp.concatenate([top, z[sub:]], axis=0)
        sp = _softplus(z)
        log_beta = z - sp
        old = pl.ds(r0 + fresh, nrows - fresh)
        later = r_ref[hh, old, :] if fresh < nrows else None
        if fresh:
            zeros = jnp.zeros((fresh, LANES), F32)
            later = zeros if later is None else jnp.concatenate([zeros, later], axis=0)
        e = _dot(sp.astype(BF16), tri) + jnp.tile(later, (1, sub // LANES))
        w = jnp.exp(log_beta - e)
        pv = _dot(w.astype(BF16), vblk)
        if fresh:
            acc_ref[hh, pl.ds(r0, fresh), :] = pv[:fresh]
        if fresh < nrows:
            acc_ref[hh, old, :] += pv[fresh:]
        r_ref[hh, rows, :] = jnp.broadcast_to(e[:, 0:1] + sp[:, 0:1], (nrows, LANES))

    def unsaturated(first_row):
        return jnp.min(r_ref[:, first_row:, :]) < SB_SATURATED

    band = 2 * sub
    for u in reversed(range(r)):
        for hh in range(2):
            tile(hh, u * sub, min(band, tq - u * sub), u, True, fresh=sub)

    @pl.when(unsaturated(band))
    def _():
        def body(jj, carry):
            j = r - 3 - jj
            first_row = (j + 2) * sub
            for g in range(tq // group):
                g0 = max(g * group, band)

                @pl.when(jnp.logical_and(first_row // group == g, unsaturated(g0)))
                def _():
                    for hh in range(2):
                        tile(hh, g * group, tq - g * group, j, False, first_row=first_row)
            return carry

        lax.fori_loop(0, r - 2, body, 0)

    o_ref[0] = jnp.where(low, acc_ref[0], acc_ref[1]).astype(o_ref.dtype)


def _sb_attention(proj3, *, sub, group):
    b, s, _ = proj3.shape
    n_pairs = SB_WIDTH // LANES
    k_off = SB_WIDTH // LANES
    v_off = 2 * SB_WIDTH // LANES
    seq = lambda c0: pl.BlockSpec((1, s, LANES), lambda bi, p: (bi, 0, c0 + p))
    return pl.pallas_call(
        functools.partial(_sb_kernel, sub=sub, group=group),
        out_shape=jax.ShapeDtypeStruct((b, s, SB_WIDTH), BF16),
        grid=(b, n_pairs),
        in_specs=[seq(0), seq(k_off), seq(v_off)],
        out_specs=seq(0),
        scratch_shapes=[pltpu.VMEM((2, s, LANES), F32), pltpu.VMEM((2, s, LANES), F32)],
        compiler_params=_cparams(2),
        name="sb_attention",
    )(proj3, proj3, proj3)


def _t5_bucket_table():
    max_exact = REL_BUCKETS // 2
    rel = WINDOW + np.arange(WINDOW)[:, None] - np.arange(2 * WINDOW)[None, :]
    rel = np.maximum(rel, 0)
    relf = np.maximum(rel, 1).astype(np.float32)
    large = max_exact + (np.log(relf / np.float32(max_exact)) / np.float32(math.log(REL_MAX_DIST / max_exact))
                         * np.float32(REL_BUCKETS - max_exact)).astype(np.int32)
    large = np.minimum(large, REL_BUCKETS - 1)
    return np.where(rel < max_exact, rel, large).astype(np.int32)


def _swap_halves(x):
    return jnp.concatenate([x[:, HEAD_DIM:], x[:, :HEAD_DIM]], axis=1)


def _swa_kernel(table_ref, sinks_ref, bucket_ref, q_ref, kp_ref, kc_ref, vp_ref, vc_ref,
                o_ref, bias_ref, *, qb):
    bi = pl.program_id(0)
    n = pl.program_id(1)
    blk = WINDOW
    t = lax.broadcasted_iota(jnp.int32, (blk, 2 * blk), 0)
    s = lax.broadcasted_iota(jnp.int32, (blk, 2 * blk), 1)

    @pl.when(jnp.logical_and(bi == 0, n == 0))
    def _():
        bucket = bucket_ref[...]
        rel = blk + t - s
        in_window = (rel >= 0) & (rel < WINDOW)
        for h in range(SW_HEADS):
            bias = jnp.zeros((blk, 2 * blk), F32)
            for bk in range(REL_BUCKETS):
                bias = jnp.where(bucket == bk, table_ref[bk, h], bias)
            bias_ref[h] = jnp.where(in_window, bias, -jnp.inf)

    kk = jnp.concatenate([kp_ref[0], kc_ref[0]], axis=0)
    vv = jnp.concatenate([vp_ref[0], vc_ref[0]], axis=0)
    kk_sw = _swap_halves(kk)
    vv_sw = _swap_halves(vv)
    lane = lax.broadcasted_iota(jnp.int32, (blk, LANES), 1)
    low = lane < HEAD_DIM
    scale = HEAD_DIM ** -0.5
    group = SW_HEADS // SW_KV_HEADS
    has_prev = (s >= blk) | (n > 0)

    for j in range(qb):
        rows = slice(j * blk, (j + 1) * blk)
        keys = slice(j * blk, (j + 2) * blk)
        for p in range(SW_HEADS // 2):
            q2 = q_ref[0, rows, p * LANES:(p + 1) * LANES]
            zero = jnp.zeros_like(q2)
            outs = []
            for half in range(2):
                h = 2 * p + half
                kv = h // group
                qh = (jnp.where(low, q2, zero) if half == 0 else jnp.where(low, zero, q2)) * scale
                k_use = (kk if kv == half else kk_sw)[keys]
                v_use = (vv if kv == half else vv_sw)[keys]
                logits = _dot_nt(qh, k_use) + bias_ref[h]
                if j == 0:
                    logits = jnp.where(has_prev, logits, -jnp.inf)
                sink = sinks_ref[0, h]
                m = jnp.maximum(jnp.max(logits, axis=-1, keepdims=True), sink)
                pr = jnp.exp(logits - m)
                denom = jnp.sum(pr, axis=-1, keepdims=True) + jnp.exp(sink - m)
                outs.append(_dot(pr.astype(BF16), v_use) / denom)
            o_ref[0, rows, p * LANES:(p + 1) * LANES] = jnp.where(low, outs[0], outs[1]).astype(o_ref.dtype)


def _swa_attention(proj3, table, sinks, *, qb):
    b, s, _ = proj3.shape
    rows = qb * WINDOW
    q_blk = (3 * SB_WIDTH) // SW_Q_WIDTH
    k_blk = (3 * SB_WIDTH + SW_Q_WIDTH) // LANES
    v_blk = k_blk + 1
    bucket = jnp.asarray(_t5_bucket_table())
    smem = pl.BlockSpec(memory_space=pltpu.SMEM)
    prev = lambda c: (lambda bi, n: (bi, jnp.maximum(n * qb - 1, 0), c))
    cur = lambda c: (lambda bi, n: (bi, n, c))
    return pl.pallas_call(
        functools.partial(_swa_kernel, qb=qb),
        out_shape=jax.ShapeDtypeStruct((b, s, SW_Q_WIDTH), BF16),
        grid=(b, s // rows),
        in_specs=[smem, smem,
                  pl.BlockSpec((WINDOW, 2 * WINDOW), lambda bi, n: (0, 0)),
                  pl.BlockSpec((1, rows, SW_Q_WIDTH), cur(q_blk)),
                  pl.BlockSpec((1, WINDOW, LANES), prev(k_blk)),
                  pl.BlockSpec((1, rows, LANES), cur(k_blk)),
                  pl.BlockSpec((1, WINDOW, LANES), prev(v_blk)),
                  pl.BlockSpec((1, rows, LANES), cur(v_blk))],
        out_specs=pl.BlockSpec((1, rows, SW_Q_WIDTH), lambda bi, n: (bi, n, 0)),
        scratch_shapes=[pltpu.VMEM((SW_HEADS, WINDOW, 2 * WINDOW), F32)],
        compiler_params=_cparams(2),
        name="swa_attention",
    )(table, sinks, bucket, proj3, proj3, proj3, proj3, proj3)


def _to_bf16_kernel(w_ref, o_ref):
    o_ref[...] = w_ref[...].astype(BF16)


def _to_bf16(w, *, tr):
    n, r, c = w.shape
    spec = pl.BlockSpec((1, tr, c), lambda l, i: (l, i, 0))
    return pl.pallas_call(
        _to_bf16_kernel,
        out_shape=jax.ShapeDtypeStruct(w.shape, BF16),
        grid=(n, r // tr),
        in_specs=[spec],
        out_specs=spec,
        compiler_params=_cparams(2),
        name="to_bf16",
    )(w)


def _out_ffn_kernel(*refs, n_parts, final_norm):
    x_ref = refs[0]
    part_refs = refs[1:1 + n_parts]
    wo_refs = refs[1 + n_parts:1 + 2 * n_parts]
    g_ref, wg_ref, wu_ref, wd_ref, fg_ref, o_ref = refs[1 + 2 * n_parts:]
    x = x_ref[...]
    for a_ref, w_ref in zip(part_refs, wo_refs):
        x = x + _dot(a_ref[...], w_ref[...])
    h = _rms(x, g_ref[...]).astype(BF16)
    gate = _dot(h, wg_ref[...])
    up = _dot(h, wu_ref[...])
    act = (gate / (1.0 + jnp.exp(-gate)) * up).astype(BF16)
    y = x + _dot(act, wd_ref[...])
    if final_norm:
        y = _rms(y, fg_ref[...])
    o_ref[...] = y


def _out_ffn(x2, parts, w_out, g, wg, wu, wd, fg, *, layer, tm, final_norm):
    t, d = x2.shape
    hid = wg.shape[2]
    n_parts = len(parts)
    kp = w_out.shape[0] // n_parts
    resident = dict(pipeline_mode=pl.Buffered(1))
    row_tile = lambda width: pl.BlockSpec((tm, width), lambda i: (i, 0))
    const = lambda r, c: pl.BlockSpec((None, r, c), lambda i: (layer, 0, 0), **resident)
    in_specs = [row_tile(d)] + [row_tile(kp) for _ in parts]
    in_specs += [pl.BlockSpec((kp, d), functools.partial(lambda i, c: (c, 0), c=c), **resident)
                 for c in range(n_parts)]
    in_specs += [pl.BlockSpec((1, d), lambda i: (0, 0)), const(d, hid), const(d, hid),
                 const(hid, d), pl.BlockSpec((1, d), lambda i: (0, 0))]
    return pl.pallas_call(
        functools.partial(_out_ffn_kernel, n_parts=n_parts, final_norm=final_norm),
        out_shape=jax.ShapeDtypeStruct((t, d), F32),
        grid=(t // tm,),
        in_specs=in_specs,
        out_specs=row_tile(d),
        compiler_params=_cparams(1),
        name="out_ffn_final" if final_norm else "out_ffn",
    )(x2, *parts, *([w_out] * n_parts), g, wg, wu, wd, fg)


def _mla_proj_kernel(x_ref, pos_ref, freq_ref, g_ref, wd_ref, qg_ref, wq_ref,
                     kg_ref, wk_ref, wv_ref, q_ref, k_ref, v_ref):
    tm = x_ref.shape[0]
    h = _rms(x_ref[...], g_ref[...]).astype(BF16)
    down = _dot(h, wd_ref[...])
    c_q = down[:, :MLA_Q_RANK]
    c_kv = down[:, MLA_Q_RANK:MLA_Q_RANK + MLA_KV_RANK]
    kr = down[:, MLA_Q_RANK + MLA_KV_RANK:]

    half = MLA_ROPE_DIM // 2
    ang = pos_ref[...].astype(F32) * freq_ref[...]
    lane = lax.broadcasted_iota(jnp.int32, (tm, LANES), 1)
    first = lane < half
    second = (lane >= LANES // 2) & (lane < LANES // 2 + half)

    def spread(dense):
        rows8 = jnp.broadcast_to(dense[:, None, :], (tm // 8, 8, LANES)).reshape(tm, LANES)
        y = pltpu.roll(rows8, 0, 1, stride=half, stride_axis=0)
        return y, pltpu.roll(y, LANES // 2, 1)

    cos_1, cos_2 = spread(jnp.cos(ang))
    sin_1, sin_2 = spread(jnp.sin(ang))
    cos_p = jnp.where(first, cos_1, jnp.where(second, cos_2, 1.0))
    sin_p = jnp.where(first, -sin_1, jnp.where(second, sin_2, 0.0))

    def rope(blk):
        return blk * cos_p + pltpu.roll(blk, LANES // 2, 1) * sin_p

    cqn = _rms(c_q, qg_ref[...]).astype(BF16)
    ckvn = _rms(c_kv, kg_ref[...]).astype(BF16)
    k_rope = rope(kr)
    q = _dot(cqn, wq_ref[...])
    kn = _dot(ckvn, wk_ref[...])
    for hd in range(MLA_HEADS):
        sl = slice(hd * LANES, (hd + 1) * LANES)
        q_ref[:, sl] = rope(q[:, sl]).astype(q_ref.dtype)
        k_ref[:, sl] = (kn[:, sl] + k_rope).astype(k_ref.dtype)
    v_ref[...] = _dot(ckvn, wv_ref[...]).astype(v_ref.dtype)


def _mla_proj(x2, pos2, freq, g, wd, qg, wq, kg, wk, wv, *, tm):
    t, d = x2.shape
    full = lambda a: pl.BlockSpec(a.shape, lambda i: (0,) * a.ndim)
    qk_w = MLA_HEADS * LANES
    v_w = MLA_HEADS * MLA_V_DIM
    return pl.pallas_call(
        _mla_proj_kernel,
        out_shape=(jax.ShapeDtypeStruct((t, qk_w), BF16),
                   jax.ShapeDtypeStruct((t, qk_w), BF16),
                   jax.ShapeDtypeStruct((t, v_w), BF16)),
        grid=(t // tm,),
        in_specs=[pl.BlockSpec((tm, d), lambda i: (i, 0)),
                  pl.BlockSpec((tm // 8, LANES), lambda i: (i, 0)),
                  full(freq), full(g), full(wd), full(qg), full(wq),
                  full(kg), full(wk), full(wv)],
        out_specs=(pl.BlockSpec((tm, qk_w), lambda i: (i, 0)),
                   pl.BlockSpec((tm, qk_w), lambda i: (i, 0)),
                   pl.BlockSpec((tm, v_w), lambda i: (i, 0))),
        compiler_params=_cparams(1),
        name="mla_proj",
    )(x2, pos2, freq, g, wd, qg, wq, kg, wk, wv)


def _mla_attn_kernel(q_ref, k_ref, v_ref, o_ref, va_ref, vb_ref, m_ref, acc_ref, *, tq, tk, unroll):
    qi = pl.program_id(2)
    lane_s = lax.broadcasted_iota(jnp.int32, va_ref.shape, 1)

    @pl.when(qi == 0)
    def _():
        v_all = v_ref[0]
        one = jnp.ones_like(v_all)
        va_ref[...] = jnp.where(lane_s < MLA_V_DIM, v_all, one)
        vb_ref[...] = jnp.where(lane_s < MLA_V_DIM, one, v_all)

    r = tq // tk
    lower = (lax.broadcasted_iota(jnp.int32, (tk, tk), 1)
             <= lax.broadcasted_iota(jnp.int32, (tk, tk), 0))
    c2 = (MLA_NOPE_DIM + MLA_ROPE_DIM) ** -0.5 * LOG2E
    vaug = (va_ref, vb_ref)

    m_ref[...] = jnp.full(m_ref.shape, -jnp.inf, F32)
    acc_ref[...] = jnp.zeros(acc_ref.shape, F32)

    def tile(hh, r0, nrows, j, nblk, masked):
        rows = pl.ds(r0, nrows)
        nkeys = nblk * tk
        q = q_ref[0, rows, hh * LANES:(hh + 1) * LANES]
        start = pl.multiple_of(j * tk, tk)
        kblk = k_ref[0, pl.ds(start, nkeys), hh * LANES:(hh + 1) * LANES]
        vblk = vaug[hh][pl.ds(start, nkeys), :]
        s2 = _dot_nt(q, kblk) * c2
        if masked:
            last = jnp.where(lower, s2[:, nkeys - tk:], -jnp.inf)
            s2 = last if nblk == 1 else jnp.concatenate([s2[:, :nkeys - tk], last], axis=1)
        m = m_ref[hh, rows, :]
        m_new = jnp.maximum(m, jnp.max(s2, axis=-1, keepdims=True))
        alpha = jnp.exp2(m - m_new)
        p = jnp.exp2(s2 - jnp.tile(m_new, (1, nkeys // LANES)))
        acc_ref[hh, rows, :] = alpha * acc_ref[hh, rows, :] + _dot(p.astype(BF16), vblk)
        m_ref[hh, rows, :] = m_new

    def body(jj, carry):
        for t in range(unroll):
            for hh in range(2):
                tile(hh, 0, tq, jj * unroll + t, 1, False)
        return carry

    lax.fori_loop(0, (qi * r) // unroll, body, 0)
    for pair in range(r // 2):
        j = qi * r + 2 * pair
        r0 = 2 * pair * tk
        for hh in range(2):
            tile(hh, r0, tk, j, 1, True)
            tile(hh, r0 + tk, tk, j, 2, True)
            if r0 + 2 * tk < tq:
                tile(hh, r0 + 2 * tk, tq - r0 - 2 * tk, j, 2, False)

    lane = lax.broadcasted_iota(jnp.int32, (tq, LANES), 1)
    acc_a = acc_ref[0]
    acc_b = acc_ref[1]
    out_a = acc_a / pltpu.roll(acc_a, MLA_V_DIM, 1)
    out_b = acc_b / pltpu.roll(acc_b, MLA_V_DIM, 1)
    o_ref[0] = jnp.where(lane < MLA_V_DIM, out_a, out_b).astype(o_ref.dtype)


def _mla_attention(q3, k3, v3, *, tq, tk, unroll):
    b, s, _ = q3.shape
    assert (tq // tk) % unroll == 0
    n_pairs = MLA_HEADS // 2
    return pl.pallas_call(
        functools.partial(_mla_attn_kernel, tq=tq, tk=tk, unroll=unroll),
        out_shape=jax.ShapeDtypeStruct((b, s, MLA_HEADS * MLA_V_DIM), BF16),
        grid=(b, n_pairs, s // tq),
        in_specs=[pl.BlockSpec((1, tq, 2 * LANES), lambda bi, p, i: (bi, i, p)),
                  pl.BlockSpec((1, s, 2 * LANES), lambda bi, p, i: (bi, 0, p)),
                  pl.BlockSpec((1, s, LANES), lambda bi, p, i: (bi, 0, p))],
        out_specs=pl.BlockSpec((1, tq, LANES), lambda bi, p, i: (bi, i, p)),
        scratch_shapes=[pltpu.VMEM((s, LANES), BF16), pltpu.VMEM((s, LANES), BF16),
                        pltpu.VMEM((2, tq, LANES), F32), pltpu.VMEM((2, tq, LANES), F32)],
        compiler_params=_cparams(3),
        name="mla_attention",
    )(q3, k3, v3)


def _head_lanes(nope, rope):
    half = MLA_ROPE_DIM // 2
    split = LANES // 2 - half
    zero = jnp.zeros(nope.shape[:-1] + (LANES - MLA_NOPE_DIM - MLA_ROPE_DIM,), nope.dtype)
    return jnp.concatenate([rope[..., :half], nope[..., :split], rope[..., half:], nope[..., split:], zero],
                           axis=-1)


def _mla_weight_layout(w_down, w_uq, w_ukv):
    d = w_down.shape[0]
    w_kr = w_down[:, MLA_Q_RANK + MLA_KV_RANK:]
    wd = jnp.concatenate([w_down[:, :MLA_Q_RANK + MLA_KV_RANK],
                          _head_lanes(jnp.zeros((d, MLA_NOPE_DIM), F32), w_kr)], axis=1)

    wq = w_uq.reshape(MLA_Q_RANK, MLA_HEADS, MLA_NOPE_DIM + MLA_ROPE_DIM)
    wq = _head_lanes(wq[..., :MLA_NOPE_DIM], wq[..., MLA_NOPE_DIM:]).reshape(MLA_Q_RANK, MLA_HEADS * LANES)

    wkv = w_ukv.reshape(MLA_KV_RANK, MLA_HEADS, MLA_NOPE_DIM + MLA_V_DIM)
    wk_nope, wv = wkv[..., :MLA_NOPE_DIM], wkv[..., MLA_NOPE_DIM:]
    wk = _head_lanes(wk_nope, jnp.zeros((MLA_KV_RANK, MLA_HEADS, MLA_ROPE_DIM), F32))
    wk = wk.reshape(MLA_KV_RANK, MLA_HEADS * LANES)
    wv = wv.reshape(MLA_KV_RANK, MLA_HEADS * MLA_V_DIM)
    return wd.astype(BF16), wq.astype(BF16), wk.astype(BF16), wv.astype(BF16)


def _rope_dense_layout(positions):
    half = MLA_ROPE_DIM // 2
    n_blk = LANES // half
    freqs = ROPE_THETA ** (-jnp.arange(half, dtype=F32) / half)
    order = np.array([(-b) % n_blk for b in range(n_blk)])
    pos_dense = jnp.repeat(positions.reshape(-1, n_blk)[:, order], half, axis=1)
    return pos_dense, jnp.tile(freqs, n_blk)[None, :]


def kernel(x, positions, attn_norm, ffn_norm, even_w_in, even_sinks, even_w_out, rel_bias_table, mla_w_down, mla_q_norm, mla_w_uq, mla_kv_norm, mla_w_ukv, mla_w_o, ffn_w_gate, ffn_w_up, ffn_w_down, final_norm):
    b, s, d = x.shape
    t = b * s
    x2 = x.reshape(t, d)
    row = lambda v: v.reshape(1, -1)

    proj = _norm_proj(x2, row(attn_norm[0]), even_w_in[0].astype(BF16), tm=1024)
    proj3 = proj.reshape(b, s, EVEN_IN_WIDTH)
    o_a = _sb_attention(proj3, sub=256, group=1024)
    o_b = _swa_attention(proj3, rel_bias_table, even_sinks[0].reshape(1, SW_HEADS), qb=8)
    wg, wu, wd_ffn = _to_bf16(ffn_w_gate, tr=256), _to_bf16(ffn_w_up, tr=256), _to_bf16(ffn_w_down, tr=704)
    x2 = _out_ffn(x2, [o_a.reshape(t, SB_WIDTH), o_b.reshape(t, SW_Q_WIDTH)], even_w_out[0].astype(BF16),
                  row(ffn_norm[0]), wg, wu, wd_ffn, row(final_norm), layer=0, tm=512, final_norm=False)

    wd, wq, wk, wv = _mla_weight_layout(mla_w_down[0], mla_w_uq[0], mla_w_ukv[0])
    pos_dense, freq_dense = _rope_dense_layout(positions)
    q, k, v = _mla_proj(x2, pos_dense, freq_dense, row(attn_norm[1]), wd,
                        row(mla_q_norm[0]), wq, row(mla_kv_norm[0]), wk, wv, tm=512)
    o = _mla_attention(q.reshape(b, s, -1), k.reshape(b, s, -1), v.reshape(b, s, -1), tq=4096, tk=512, unroll=2)
    x2 = _out_ffn(x2, [o.reshape(t, MLA_HEADS * MLA_V_DIM)], mla_w_o[0].astype(BF16),
                  row(ffn_norm[1]), wg, wu, wd_ffn, row(final_norm), layer=1, tm=512, final_norm=True)
    return x2.reshape(b, s, d)
```

```python
import functools
import math

import numpy as np
import jax
import jax.numpy as jnp
from jax import lax
from jax.experimental import pallas as pl
from jax.experimental.pallas import tpu as pltpu

F32 = jnp.float32
BF16 = jnp.bfloat16

D_MODEL = 1024
EPS = 1e-6
HEAD_DIM = 64
LANES = 128
SB_HEADS = 8
SW_HEADS = 8
SW_KV_HEADS = 2
WINDOW = 128
SB_WIDTH = SB_HEADS * HEAD_DIM
SW_Q_WIDTH = SW_HEADS * HEAD_DIM
SW_KV_WIDTH = SW_KV_HEADS * HEAD_DIM
EVEN_IN_WIDTH = 3 * SB_WIDTH + SW_Q_WIDTH + 2 * SW_KV_WIDTH
REL_BUCKETS = 32
REL_MAX_DIST = 128
MLA_HEADS = 16
MLA_NOPE_DIM = 64
MLA_ROPE_DIM = 32
MLA_V_DIM = 64
MLA_Q_RANK = 384
MLA_KV_RANK = 256
ROPE_THETA = 10000.0
LOG2E = math.log2(math.e)
SB_SATURATED = 128.0
VMEM_LIMIT_BYTES = 56 * 1024 * 1024


def _cparams(n_axes):
    return pltpu.CompilerParams(
        dimension_semantics=("arbitrary",) * n_axes,
        vmem_limit_bytes=VMEM_LIMIT_BYTES)


def _rms(x, g):
    return x * lax.rsqrt(jnp.mean(x * x, axis=-1, keepdims=True) + EPS) * g


def _dot(a, b):
    return jnp.dot(a, b, preferred_element_type=F32)


def _dot_nt(a, b):
    return lax.dot_general(a, b, (((1,), (1,)), ((), ())), preferred_element_type=F32)


def _norm_proj_kernel(x_ref, g_ref, w_ref, o_ref):
    h = _rms(x_ref[...], g_ref[...]).astype(BF16)
    o_ref[...] = _dot(h, w_ref[...]).astype(o_ref.dtype)


def _norm_proj(x2, g, w, *, tm):
    t, d = x2.shape
    n = w.shape[1]
    return pl.pallas_call(
        _norm_proj_kernel,
        out_shape=jax.ShapeDtypeStruct((t, n), BF16),
        grid=(t // tm,),
        in_specs=[pl.BlockSpec((tm, d), lambda i: (i, 0)),
                  pl.BlockSpec((1, d), lambda i: (0, 0)),
                  pl.BlockSpec((d, n), lambda i: (0, 0))],
        out_specs=pl.BlockSpec((tm, n), lambda i: (i, 0)),
        compiler_params=_cparams(1),
        name="norm_proj",
    )(x2, g, w)


def _softplus(z):
    return jnp.maximum(z, 0.0) + jnp.log(1.0 + jnp.exp2(jnp.abs(z) * (-LOG2E)))


def _sb_kernel(q_ref, k_ref, v_ref, o_ref, r_ref, acc_ref, *, sub, group):
    tq = q_ref.shape[1]
    r = tq // sub
    q2 = q_ref[0]
    lane = lax.broadcasted_iota(jnp.int32, (tq, LANES), 1)
    low = lane < HEAD_DIM
    zero = jnp.zeros_like(q2)
    scale = HEAD_DIM ** -0.5
    qm = (jnp.where(low, q2, zero) * scale, jnp.where(low, zero, q2) * scale)

    tri_row = lax.broadcasted_iota(jnp.int32, (sub, sub), 0)
    tri_col = lax.broadcasted_iota(jnp.int32, (sub, sub), 1)
    strictly_lower = tri_col < tri_row
    tri = strictly_lower.astype(BF16)

    def tile(hh, r0, nrows, j, masked, first_row=None, fresh=0):
        rows = pl.ds(r0, nrows)
        start = pl.multiple_of(j * sub, sub)
        kblk = k_ref[0, pl.ds(start, sub), :]
        vblk = v_ref[0, pl.ds(start, sub), :]
        z = _dot_nt(qm[hh][r0:r0 + nrows], kblk)
        if first_row is not None:
            row = lax.broadcasted_iota(jnp.int32, (nrows, sub), 0) + r0
            z = jnp.where(row >= first_row, z, -jnp.inf)
        if masked:
            top = jnp.where(strictly_lower, z[:sub], -jnp.inf)
            z = top if nrows == sub else jnp.concatenate([top, z[sub:]], axis=0)
        sp = _softplus(z)
        log_beta = z - sp
        old = pl.ds(r0 + fresh, nrows - fresh)
        later = r_ref[hh, old, :] if fresh < nrows else None
        if fresh:
            zeros = jnp.zeros((fresh, LANES), F32)
            later = zeros if later is None else jnp.concatenate([zeros, later], axis=0)
        e = _dot(sp.astype(BF16), tri) + jnp.tile(later, (1, sub // LANES))
        w = jnp.exp(log_beta - e)
        pv = _dot(w.astype(BF16), vblk)
        if fresh:
            acc_ref[hh, pl.ds(r0, fresh), :] = pv[:fresh]
        if fresh < nrows:
            acc_ref[hh, old, :] += pv[fresh:]
        r_ref[hh, rows, :] = jnp.broadcast_to(e[:, 0:1] + sp[:, 0:1], (nrows, LANES))

    def unsaturated(first_row):
        return jnp.min(r_ref[:, first_row:, :]) < SB_SATURATED

    band = 2 * sub
    for u in reversed(range(r)):
        for hh in range(2):
            tile(hh, u * sub, min(band, tq - u * sub), u, True, fresh=sub)

    @pl.when(unsaturated(band))
    def _():
        def body(jj, carry):
            j = r - 3 - jj
            first_row = (j + 2) * sub
            for g in range(tq // group):
                g0 = max(g * group, band)

                @pl.when(jnp.logical_and(first_row // group == g, unsaturated(g0)))
                def _():
                    for hh in range(2):
                        tile(hh, g * group, tq - g * group, j, False, first_row=first_row)
            return carry

        lax.fori_loop(0, r - 2, body, 0)

    o_ref[0] = jnp.where(low, acc_ref[0], acc_ref[1]).astype(o_ref.dtype)


def _sb_attention(proj3, *, sub, group):
    b, s, _ = proj3.shape
    n_pairs = SB_WIDTH // LANES
    k_off = SB_WIDTH // LANES
    v_off = 2 * SB_WIDTH // LANES
    seq = lambda c0: pl.BlockSpec((1, s, LANES), lambda bi, p: (bi, 0, c0 + p))
    return pl.pallas_call(
        functools.partial(_sb_kernel, sub=sub, group=group),
        out_shape=jax.ShapeDtypeStruct((b, s, SB_WIDTH), BF16),
        grid=(b, n_pairs),
        in_specs=[seq(0), seq(k_off), seq(v_off)],
        out_specs=seq(0),
        scratch_shapes=[pltpu.VMEM((2, s, LANES), F32), pltpu.VMEM((2, s, LANES), F32)],
        compiler_params=_cparams(2),
        name="sb_attention",
    )(proj3, proj3, proj3)


def _t5_bucket_table():
    max_exact = REL_BUCKETS // 2
    rel = WINDOW + np.arange(WINDOW)[:, None] - np.arange(2 * WINDOW)[None, :]
    rel = np.maximum(rel, 0)
    relf = np.maximum(rel, 1).astype(np.float32)
    large = max_exact + (np.log(relf / np.float32(max_exact)) / np.float32(math.log(REL_MAX_DIST / max_exact))
                         * np.float32(REL_BUCKETS - max_exact)).astype(np.int32)
    large = np.minimum(large, REL_BUCKETS - 1)
    return np.where(rel < max_exact, rel, large).astype(np.int32)


def _swap_halves(x):
    return jnp.concatenate([x[:, HEAD_DIM:], x[:, :HEAD_DIM]], axis=1)


def _swa_kernel(table_ref, sinks_ref, bucket_ref, q_ref, kp_ref, kc_ref, vp_ref, vc_ref,
                o_ref, bias_ref, *, qb):
    bi = pl.program_id(0)
    n = pl.program_id(1)
    blk = WINDOW
    t = lax.broadcasted_iota(jnp.int32, (blk, 2 * blk), 0)
    s = lax.broadcasted_iota(jnp.int32, (blk, 2 * blk), 1)

    @pl.when(jnp.logical_and(bi == 0, n == 0))
    def _():
        bucket = bucket_ref[...]
        rel = blk + t - s
        in_window = (rel >= 0) & (rel < WINDOW)
        for h in range(SW_HEADS):
            bias = jnp.zeros((blk, 2 * blk), F32)
            for bk in range(REL_BUCKETS):
                bias = jnp.where(bucket == bk, table_ref[bk, h], bias)
            bias_ref[h] = jnp.where(in_window, bias, -jnp.inf)

    kk = jnp.concatenate([kp_ref[0], kc_ref[0]], axis=0)
    vv = jnp.concatenate([vp_ref[0], vc_ref[0]], axis=0)
    kk_sw = _swap_halves(kk)
    vv_sw = _swap_halves(vv)
    lane = lax.broadcasted_iota(jnp.int32, (blk, LANES), 1)
    low = lane < HEAD_DIM
    scale = HEAD_DIM ** -0.5
    group = SW_HEADS // SW_KV_HEADS
    has_prev = (s >= blk) | (n > 0)

    for j in range(qb):
        rows = slice(j * blk, (j + 1) * blk)
        keys = slice(j * blk, (j + 2) * blk)
        for p in range(SW_HEADS // 2):
            q2 = q_ref[0, rows, p * LANES:(p + 1) * LANES]
            zero = jnp.zeros_like(q2)
            outs = []
            for half in range(2):
                h = 2 * p + half
                kv = h // group
                qh = (jnp.where(low, q2, zero) if half == 0 else jnp.where(low, zero, q2)) * scale
                k_use = (kk if kv == half else kk_sw)[keys]
                v_use = (vv if kv == half else vv_sw)[keys]
                logits = _dot_nt(qh, k_use) + bias_ref[h]
                if j == 0:
                    logits = jnp.where(has_prev, logits, -jnp.inf)
                sink = sinks_ref[0, h]
                m = jnp.maximum(jnp.max(logits, axis=-1, keepdims=True), sink)
                pr = jnp.exp(logits - m)
                denom = jnp.sum(pr, axis=-1, keepdims=True) + jnp.exp(sink - m)
                outs.append(_dot(pr.astype(BF16), v_use) / denom)
            o_ref[0, rows, p * LANES:(p + 1) * LANES] = jnp.where(low, outs[0], outs[1]).astype(o_ref.dtype)


def _swa_attention(proj3, table, sinks, *, qb):
    b, s, _ = proj3.shape
    rows = qb * WINDOW
    q_blk = (3 * SB_WIDTH) // SW_Q_WIDTH
    k_blk = (3 * SB_WIDTH + SW_Q_WIDTH) // LANES
    v_blk = k_blk + 1
    bucket = jnp.asarray(_t5_bucket_table())
    smem = pl.BlockSpec(memory_space=pltpu.SMEM)
    prev = lambda c: (lambda bi, n: (bi, jnp.maximum(n * qb - 1, 0), c))
    cur = lambda c: (lambda bi, n: (bi, n, c))
    return pl.pallas_call(
        functools.partial(_swa_kernel, qb=qb),
        out_shape=jax.ShapeDtypeStruct((b, s, SW_Q_WIDTH), BF16),
        grid=(b, s // rows),
        in_specs=[smem, smem,
                  pl.BlockSpec((WINDOW, 2 * WINDOW), lambda bi, n: (0, 0)),
                  pl.BlockSpec((1, rows, SW_Q_WIDTH), cur(q_blk)),
                  pl.BlockSpec((1, WINDOW, LANES), prev(k_blk)),
                  pl.BlockSpec((1, rows, LANES), cur(k_blk)),
                  pl.BlockSpec((1, WINDOW, LANES), prev(v_blk)),
                  pl.BlockSpec((1, rows, LANES), cur(v_blk))],
        out_specs=pl.BlockSpec((1, rows, SW_Q_WIDTH), lambda bi, n: (bi, n, 0)),
        scratch_shapes=[pltpu.VMEM((SW_HEADS, WINDOW, 2 * WINDOW), F32)],
        compiler_params=_cparams(2),
        name="swa_attention",
    )(table, sinks, bucket, proj3, proj3, proj3, proj3, proj3)


def _to_bf16_kernel(w_ref, o_ref):
    o_ref[...] = w_ref[...].astype(BF16)


def _to_bf16(w, *, tr):
    n, r, c = w.shape
    spec = pl.BlockSpec((1, tr, c), lambda l, i: (l, i, 0))
    return pl.pallas_call(
        _to_bf16_kernel,
        out_shape=jax.ShapeDtypeStruct(w.shape, BF16),
        grid=(n, r // tr),
        in_specs=[spec],
        out_specs=spec,
        compiler_params=_cparams(2),
        name="to_bf16",
    )(w)


def _out_ffn_kernel(*refs, n_parts, final_norm):
    x_ref = refs[0]
    part_refs = refs[1:1 + n_parts]
    wo_refs = refs[1 + n_parts:1 + 2 * n_parts]
    g_ref, wg_ref, wu_ref, wd_ref, fg_ref, o_ref = refs[1 + 2 * n_parts:]
    x = x_ref[...]
    for a_ref, w_ref in zip(part_refs, wo_refs):
        x = x + _dot(a_ref[...], w_ref[...])
    h = _rms(x, g_ref[...]).astype(BF16)
    gate = _dot(h, wg_ref[...])
    up = _dot(h, wu_ref[...])
    act = (gate / (1.0 + jnp.exp(-gate)) * up).astype(BF16)
    y = x + _dot(act, wd_ref[...])
    if final_norm:
        y = _rms(y, fg_ref[...])
    o_ref[...] = y


def _out_ffn(x2, parts, w_out, g, wg, wu, wd, fg, *, layer, tm, final_norm):
    t, d = x2.shape
    hid = wg.shape[2]
    n_parts = len(parts)
    kp = w_out.shape[0] // n_parts
    resident = dict(pipeline_mode=pl.Buffered(1))
    row_tile = lambda width: pl.BlockSpec((tm, width), lambda i: (i, 0))
    const = lambda r, c: pl.BlockSpec((None, r, c), lambda i: (layer, 0, 0), **resident)
    in_specs = [row_tile(d)] + [row_tile(kp) for _ in parts]
    in_specs += [pl.BlockSpec((kp, d), functools.partial(lambda i, c: (c, 0), c=c), **resident)
                 for c in range(n_parts)]
    in_specs += [pl.BlockSpec((1, d), lambda i: (0, 0)), const(d, hid), const(d, hid),
                 const(hid, d), pl.BlockSpec((1, d), lambda i: (0, 0))]
    return pl.pallas_call(
        functools.partial(_out_ffn_kernel, n_parts=n_parts, final_norm=final_norm),
        out_shape=jax.ShapeDtypeStruct((t, d), F32),
        grid=(t // tm,),
        in_specs=in_specs,
        out_specs=row_tile(d),
        compiler_params=_cparams(1),
        name="out_ffn_final" if final_norm else "out_ffn",
    )(x2, *parts, *([w_out] * n_parts), g, wg, wu, wd, fg)


def _mla_proj_kernel(x_ref, pos_ref, freq_ref, g_ref, wd_ref, qg_ref, wq_ref,
                     kg_ref, wk_ref, wv_ref, q_ref, k_ref, v_ref):
    tm = x_ref.shape[0]
    h = _rms(x_ref[...], g_ref[...]).astype(BF16)
    down = _dot(h, wd_ref[...])
    c_q = down[:, :MLA_Q_RANK]
    c_kv = down[:, MLA_Q_RANK:MLA_Q_RANK + MLA_KV_RANK]
    kr = down[:, MLA_Q_RANK + MLA_KV_RANK:]

    half = MLA_ROPE_DIM // 2
    ang = pos_ref[...].astype(F32) * freq_ref[...]
    lane = lax.broadcasted_iota(jnp.int32, (tm, LANES), 1)
    first = lane < half
    second = (lane >= LANES // 2) & (lane < LANES // 2 + half)

    def spread(dense):
        rows8 = jnp.broadcast_to(dense[:, None, :], (tm // 8, 8, LANES)).reshape(tm, LANES)
        y = pltpu.roll(rows8, 0, 1, stride=half, stride_axis=0)
        return y, pltpu.roll(y, LANES // 2, 1)

    cos_1, cos_2 = spread(jnp.cos(ang))
    sin_1, sin_2 = spread(jnp.sin(ang))
    cos_p = jnp.where(first, cos_1, jnp.where(second, cos_2, 1.0))
    sin_p = jnp.where(first, -sin_1, jnp.where(second, sin_2, 0.0))

    def rope(blk):
        return blk * cos_p + pltpu.roll(blk, LANES // 2, 1) * sin_p

    cqn = _rms(c_q, qg_ref[...]).astype(BF16)
    ckvn = _rms(c_kv, kg_ref[...]).astype(BF16)
    k_rope = rope(kr)
    q = _dot(cqn, wq_ref[...])
    kn = _dot(ckvn, wk_ref[...])
    for hd in range(MLA_HEADS):
        sl = slice(hd * LANES, (hd + 1) * LANES)
        q_ref[:, sl] = rope(q[:, sl]).astype(q_ref.dtype)
        k_ref[:, sl] = (kn[:, sl] + k_rope).astype(k_ref.dtype)
    v_ref[...] = _dot(ckvn, wv_ref[...]).astype(v_ref.dtype)


def _mla_proj(x2, pos2, freq, g, wd, qg, wq, kg, wk, wv, *, tm):
    t, d = x2.shape
    full = lambda a: pl.BlockSpec(a.shape, lambda i: (0,) * a.ndim)
    qk_w = MLA_HEADS * LANES
    v_w = MLA_HEADS * MLA_V_DIM
    return pl.pallas_call(
        _mla_proj_kernel,
        out_shape=(jax.ShapeDtypeStruct((t, qk_w), BF16),
                   jax.ShapeDtypeStruct((t, qk_w), BF16),
                   jax.ShapeDtypeStruct((t, v_w), BF16)),
        grid=(t // tm,),
        in_specs=[pl.BlockSpec((tm, d), lambda i: (i, 0)),
                  pl.BlockSpec((tm // 8, LANES), lambda i: (i, 0)),
                  full(freq), full(g), full(wd), full(qg), full(wq),
                  full(kg), full(wk), full(wv)],
        out_specs=(pl.BlockSpec((tm, qk_w), lambda i: (i, 0)),
                   pl.BlockSpec((tm, qk_w), lambda i: (i, 0)),
                   pl.BlockSpec((tm, v_w), lambda i: (i, 0))),
        compiler_params=_cparams(1),
        name="mla_proj",
    )(x2, pos2, freq, g, wd, qg, wq, kg, wk, wv)


def _mla_attn_kernel(q_ref, k_ref, v_ref, o_ref, va_ref, vb_ref, m_ref, acc_ref, *, tq, tk, unroll):
    qi = pl.program_id(2)
    lane_s = lax.broadcasted_iota(jnp.int32, va_ref.shape, 1)

    @pl.when(qi == 0)
    def _():
        v_all = v_ref[0]
        one = jnp.ones_like(v_all)
        va_ref[...] = jnp.where(lane_s < MLA_V_DIM, v_all, one)
        vb_ref[...] = jnp.where(lane_s < MLA_V_DIM, one, v_all)

    r = tq // tk
    lower = (lax.broadcasted_iota(jnp.int32, (tk, tk), 1)
             <= lax.broadcasted_iota(jnp.int32, (tk, tk), 0))
    c2 = (MLA_NOPE_DIM + MLA_ROPE_DIM) ** -0.5 * LOG2E
    vaug = (va_ref, vb_ref)

    whole = tq == k_ref.shape[1]
    if not whole:
        m_ref[...] = jnp.full(m_ref.shape, -jnp.inf, F32)
        acc_ref[...] = jnp.zeros(acc_ref.shape, F32)

    def tile(hh, r0, nrows, j, masked, first=False):
        rows = pl.ds(r0, nrows)
        q = q_ref[0, rows, hh * LANES:(hh + 1) * LANES]
        start = pl.multiple_of(j * tk, tk)
        kblk = k_ref[0, pl.ds(start, tk), hh * LANES:(hh + 1) * LANES]
        vblk = vaug[hh][pl.ds(start, tk), :]
        s2 = _dot_nt(q, kblk) * c2
        if masked:
            top = jnp.where(lower, s2[:tk], -jnp.inf)
            s2 = top if nrows == tk else jnp.concatenate([top, s2[tk:]], axis=0)
        row_max = jnp.max(s2, axis=-1, keepdims=True)
        if first:
            m_new = jnp.broadcast_to(row_max, (nrows, LANES))
            p = jnp.exp2(s2 - jnp.tile(m_new, (1, tk // LANES)))
            acc_ref[hh, rows, :] = _dot(p.astype(BF16), vblk)
        else:
            m = m_ref[hh, rows, :]
            m_new = jnp.maximum(m, row_max)
            alpha = jnp.exp2(m - m_new)
            p = jnp.exp2(s2 - jnp.tile(m_new, (1, tk // LANES)))
            acc_ref[hh, rows, :] = alpha * acc_ref[hh, rows, :] + _dot(p.astype(BF16), vblk)
        m_ref[hh, rows, :] = m_new

    def body(jj, carry):
        for t in range(unroll):
            for hh in range(2):
                tile(hh, 0, tq, jj * unroll + t, False)
        return carry

    if not whole:
        lax.fori_loop(0, (qi * r) // unroll, body, 0)
    for u in range(r):
        for hh in range(2):
            tile(hh, u * tk, tq - u * tk, qi * r + u, True, first=whole and u == 0)

    lane = lax.broadcasted_iota(jnp.int32, (tq, LANES), 1)
    acc_a = acc_ref[0]
    acc_b = acc_ref[1]
    out_a = acc_a / pltpu.roll(acc_a, MLA_V_DIM, 1)
    out_b = acc_b / pltpu.roll(acc_b, MLA_V_DIM, 1)
    o_ref[0] = jnp.where(lane < MLA_V_DIM, out_a, out_b).astype(o_ref.dtype)


def _mla_attention(q3, k3, v3, *, tq, tk, unroll):
    b, s, _ = q3.shape
    assert (tq // tk) % unroll == 0
    n_pairs = MLA_HEADS // 2
    return pl.pallas_call(
        functools.partial(_mla_attn_kernel, tq=tq, tk=tk, unroll=unroll),
        out_shape=jax.ShapeDtypeStruct((b, s, MLA_HEADS * MLA_V_DIM), BF16),
        grid=(b, n_pairs, s // tq),
        in_specs=[pl.BlockSpec((1, tq, 2 * LANES), lambda bi, p, i: (bi, i, p)),
                  pl.BlockSpec((1, s, 2 * LANES), lambda bi, p, i: (bi, 0, p)),
                  pl.BlockSpec((1, s, LANES), lambda bi, p, i: (bi, 0, p))],
        out_specs=pl.BlockSpec((1, tq, LANES), lambda bi, p, i: (bi, i, p)),
        scratch_shapes=[pltpu.VMEM((s, LANES), BF16), pltpu.VMEM((s, LANES), BF16),
                        pltpu.VMEM((2, tq, LANES), F32), pltpu.VMEM((2, tq, LANES), F32)],
        compiler_params=_cparams(3),
        name="mla_attention",
    )(q3, k3, v3)


def _head_lanes(nope, rope):
    half = MLA_ROPE_DIM // 2
    split = LANES // 2 - half
    zero = jnp.zeros(nope.shape[:-1] + (LANES - MLA_NOPE_DIM - MLA_ROPE_DIM,), nope.dtype)
    return jnp.concatenate([rope[..., :half], nope[..., :split], rope[..., half:], nope[..., split:], zero],
                           axis=-1)


def _mla_weight_layout(w_down, w_uq, w_ukv):
    d = w_down.shape[0]
    w_kr = w_down[:, MLA_Q_RANK + MLA_KV_RANK:]
    wd = jnp.concatenate([w_down[:, :MLA_Q_RANK + MLA_KV_RANK],
                          _head_lanes(jnp.zeros((d, MLA_NOPE_DIM), F32), w_kr)], axis=1)

    wq = w_uq.reshape(MLA_Q_RANK, MLA_HEADS, MLA_NOPE_DIM + MLA_ROPE_DIM)
    wq = _head_lanes(wq[..., :MLA_NOPE_DIM], wq[..., MLA_NOPE_DIM:]).reshape(MLA_Q_RANK, MLA_HEADS * LANES)

    wkv = w_ukv.reshape(MLA_KV_RANK, MLA_HEADS, MLA_NOPE_DIM + MLA_V_DIM)
    wk_nope, wv = wkv[..., :MLA_NOPE_DIM], wkv[..., MLA_NOPE_DIM:]
    wk = _head_lanes(wk_nope, jnp.zeros((MLA_KV_RANK, MLA_HEADS, MLA_ROPE_DIM), F32))
    wk = wk.reshape(MLA_KV_RANK, MLA_HEADS * LANES)
    wv = wv.reshape(MLA_KV_RANK, MLA_HEADS * MLA_V_DIM)
    return wd.astype(BF16), wq.astype(BF16), wk.astype(BF16), wv.astype(BF16)


def _rope_dense_layout(positions):
    half = MLA_ROPE_DIM // 2
    n_blk = LANES // half
    freqs = ROPE_THETA ** (-jnp.arange(half, dtype=F32) / half)
    order = np.array([(-b) % n_blk for b in range(n_blk)])
    pos_dense = jnp.repeat(positions.reshape(-1, n_blk)[:, order], half, axis=1)
    return pos_dense, jnp.tile(freqs, n_blk)[None, :]


def kernel(x, positions, attn_norm, ffn_norm, even_w_in, even_sinks, even_w_out, rel_bias_table, mla_w_down, mla_q_norm, mla_w_uq, mla_kv_norm, mla_w_ukv, mla_w_o, ffn_w_gate, ffn_w_up, ffn_w_down, final_norm):
    b, s, d = x.shape
    t = b * s
    x2 = x.reshape(t, d)
    row = lambda v: v.reshape(1, -1)

    proj = _norm_proj(x2, row(attn_norm[0]), even_w_in[0].astype(BF16), tm=1024)
    proj3 = proj.reshape(b, s, EVEN_IN_WIDTH)
    o_a = _sb_attention(proj3, sub=256, group=1024)
    o_b = _swa_attention(proj3, rel_bias_table, even_sinks[0].reshape(1, SW_HEADS), qb=8)
    wg, wu, wd_ffn = _to_bf16(ffn_w_gate, tr=256), _to_bf16(ffn_w_up, tr=256), _to_bf16(ffn_w_down, tr=704)
    x2 = _out_ffn(x2, [o_a.reshape(t, SB_WIDTH), o_b.reshape(t, SW_Q_WIDTH)], even_w_out[0].astype(BF16),
                  row(ffn_norm[0]), wg, wu, wd_ffn, row(final_norm), layer=0, tm=512, final_norm=False)

    wd, wq, wk, wv = _mla_weight_layout(mla_w_down[0], mla_w_uq[0], mla_w_ukv[0])
    pos_dense, freq_dense = _rope_dense_layout(positions)
    q, k, v = _mla_proj(x2, pos_dense, freq_dense, row(attn_norm[1]), wd,
                        row(mla_q_norm[0]), wq, row(mla_kv_norm[0]), wk, wv, tm=512)
    o = _mla_attention(q.reshape(b, s, -1), k.reshape(b, s, -1), v.reshape(b, s, -1), tq=4096, tk=512, unroll=2)
    x2 = _out_ffn(x2, [o.reshape(t, MLA_HEADS * MLA_V_DIM)], mla_w_o[0].astype(BF16),
                  row(ffn_norm[1]), wg, wu, wd_ffn, row(final_norm), layer=1, tm=512, final_norm=True)
    return x2.reshape(b, s, d)
```

```python
import functools
import math

import numpy as np
import jax
import jax.numpy as jnp
from jax import lax
from jax.experimental import pallas as pl
from jax.experimental.pallas import tpu as pltpu

F32 = jnp.float32
BF16 = jnp.bfloat16

D_MODEL = 1024
EPS = 1e-6
HEAD_DIM = 64
LANES = 128
SB_HEADS = 8
SW_HEADS = 8
SW_KV_HEADS = 2
WINDOW = 128
SB_WIDTH = SB_HEADS * HEAD_DIM
SW_Q_WIDTH = SW_HEADS * HEAD_DIM
SW_KV_WIDTH = SW_KV_HEADS * HEAD_DIM
EVEN_IN_WIDTH = 3 * SB_WIDTH + SW_Q_WIDTH + 2 * SW_KV_WIDTH
REL_BUCKETS = 32
REL_MAX_DIST = 128
MLA_HEADS = 16
MLA_NOPE_DIM = 64
MLA_ROPE_DIM = 32
MLA_V_DIM = 64
MLA_Q_RANK = 384
MLA_KV_RANK = 256
ROPE_THETA = 10000.0
LOG2E = math.log2(math.e)
SB_SATURATED = 128.0
VMEM_LIMIT_BYTES = 56 * 1024 * 1024


def _cparams(n_axes):
    return pltpu.CompilerParams(
        dimension_semantics=("arbitrary",) * n_axes,
        vmem_limit_bytes=VMEM_LIMIT_BYTES)


def _rms(x, g):
    return x * lax.rsqrt(jnp.mean(x * x, axis=-1, keepdims=True) + EPS) * g


def _dot(a, b):
    return jnp.dot(a, b, preferred_element_type=F32)


def _dot_nt(a, b):
    return lax.dot_general(a, b, (((1,), (1,)), ((), ())), preferred_element_type=F32)


def _norm_proj_kernel(x_ref, g_ref, w_ref, o_ref, wb_ref):
    @pl.when(pl.program_id(0) == 0)
    def _():
        wb_ref[...] = w_ref[...].astype(BF16)

    h = _rms(x_ref[...], g_ref[...]).astype(BF16)
    o_ref[...] = _dot(h, wb_ref[...]).astype(o_ref.dtype)


def _norm_proj(x2, g, w, *, tm):
    t, d = x2.shape
    n = w.shape[1]
    return pl.pallas_call(
        _norm_proj_kernel,
        out_shape=jax.ShapeDtypeStruct((t, n), BF16),
        grid=(t // tm,),
        in_specs=[pl.BlockSpec((tm, d), lambda i: (i, 0)),
                  pl.BlockSpec((1, d), lambda i: (0, 0)),
                  pl.BlockSpec((d, n), lambda i: (0, 0), pipeline_mode=pl.Buffered(1))],
        out_specs=pl.BlockSpec((tm, n), lambda i: (i, 0)),
        scratch_shapes=[pltpu.VMEM((d, n), BF16)],
        compiler_params=_cparams(1),
        name="norm_proj",
    )(x2, g, w)


def _softplus(z):
    return jnp.maximum(z, 0.0) + jnp.log(1.0 + jnp.exp2(jnp.abs(z) * (-LOG2E)))


def _sb_kernel(q_ref, k_ref, v_ref, o_ref, r_ref, acc_ref, *, sub, group):
    tq = q_ref.shape[1]
    r = tq // sub
    heads = 2 * (q_ref.shape[2] // LANES)
    lane = lax.broadcasted_iota(jnp.int32, (tq, LANES), 1)
    low = lane < HEAD_DIM
    scale = HEAD_DIM ** -0.5
    qm = []
    for pp in range(heads // 2):
        q2 = q_ref[0, :, pp * LANES:(pp + 1) * LANES]
        zero = jnp.zeros_like(q2)
        qm += [jnp.where(low, q2, zero) * scale, jnp.where(low, zero, q2) * scale]

    tri_row = lax.broadcasted_iota(jnp.int32, (sub, sub), 0)
    tri_col = lax.broadcasted_iota(jnp.int32, (sub, sub), 1)
    strictly_lower = tri_col < tri_row
    tri = strictly_lower.astype(BF16)

    def tile(hh, r0, nrows, j, masked, first_row=None, fresh=0):
        rows = pl.ds(r0, nrows)
        start = pl.multiple_of(j * sub, sub)
        pair_lanes = slice((hh // 2) * LANES, (hh // 2 + 1) * LANES)
        kblk = k_ref[0, pl.ds(start, sub), pair_lanes]
        vblk = v_ref[0, pl.ds(start, sub), pair_lanes]
        z = _dot_nt(qm[hh][r0:r0 + nrows], kblk)
        if first_row is not None:
            row = lax.broadcasted_iota(jnp.int32, (nrows, sub), 0) + r0
            z = jnp.where(row >= first_row, z, -jnp.inf)
        if masked:
            top = jnp.where(strictly_lower, z[:sub], -jnp.inf)
            z = top if nrows == sub else jnp.concatenate([top, z[sub:]], axis=0)
        sp = _softplus(z)
        log_beta = z - sp
        old = pl.ds(r0 + fresh, nrows - fresh)
        later = r_ref[hh, old, :] if fresh < nrows else None
        if fresh:
            zeros = jnp.zeros((fresh, LANES), F32)
            later = zeros if later is None else jnp.concatenate([zeros, later], axis=0)
        e = _dot(sp.astype(BF16), tri) + jnp.tile(later, (1, sub // LANES))
        w = jnp.exp(log_beta - e)
        pv = _dot(w.astype(BF16), vblk)
        if fresh:
            acc_ref[hh, pl.ds(r0, fresh), :] = pv[:fresh]
        if fresh < nrows:
            acc_ref[hh, old, :] += pv[fresh:]
        r_ref[hh, rows, :] = jnp.broadcast_to(e[:, 0:1] + sp[:, 0:1], (nrows, LANES))

    def unsaturated(first_row):
        return jnp.min(r_ref[:, first_row:, :]) < SB_SATURATED

    band = 2 * sub
    for u in reversed(range(r)):
        for hh in range(heads):
            tile(hh, u * sub, min(band, tq - u * sub), u, True, fresh=sub)

    @pl.when(unsaturated(band))
    def _():
        def body(jj, carry):
            j = r - 3 - jj
            first_row = (j + 2) * sub
            for g in range(tq // group):
                g0 = max(g * group, band)

                @pl.when(jnp.logical_and(first_row // group == g, unsaturated(g0)))
                def _():
                    for hh in range(heads):
                        tile(hh, g * group, tq - g * group, j, False, first_row=first_row)
            return carry

        lax.fori_loop(0, r - 2, body, 0)

    for pp in range(heads // 2):
        o_ref[0, :, pp * LANES:(pp + 1) * LANES] = jnp.where(
            low, acc_ref[2 * pp], acc_ref[2 * pp + 1]).astype(o_ref.dtype)


def _sb_attention(proj3, *, sub, group, pairs):
    b, s, _ = proj3.shape
    width = pairs * LANES
    k_off = SB_WIDTH // width
    v_off = 2 * SB_WIDTH // width
    seq = lambda c0: pl.BlockSpec((1, s, width), lambda bi, p: (bi, 0, c0 + p))
    return pl.pallas_call(
        functools.partial(_sb_kernel, sub=sub, group=group),
        out_shape=jax.ShapeDtypeStruct((b, s, SB_WIDTH), BF16),
        grid=(b, SB_WIDTH // width),
        in_specs=[seq(0), seq(k_off), seq(v_off)],
        out_specs=seq(0),
        scratch_shapes=[pltpu.VMEM((2 * pairs, s, LANES), F32), pltpu.VMEM((2 * pairs, s, LANES), F32)],
        compiler_params=_cparams(2),
        name="sb_attention",
    )(proj3, proj3, proj3)


def _t5_bucket_table():
    max_exact = REL_BUCKETS // 2
    rel = WINDOW + np.arange(WINDOW)[:, None] - np.arange(2 * WINDOW)[None, :]
    rel = np.maximum(rel, 0)
    relf = np.maximum(rel, 1).astype(np.float32)
    large = max_exact + (np.log(relf / np.float32(max_exact)) / np.float32(math.log(REL_MAX_DIST / max_exact))
                         * np.float32(REL_BUCKETS - max_exact)).astype(np.int32)
    large = np.minimum(large, REL_BUCKETS - 1)
    return np.where(rel < max_exact, rel, large).astype(np.int32)


def _swap_halves(x):
    return jnp.concatenate([x[:, HEAD_DIM:], x[:, :HEAD_DIM]], axis=1)


def _swa_kernel(table_ref, sinks_ref, bucket_ref, q_ref, kp_ref, kc_ref, vp_ref, vc_ref,
                o_ref, bias_ref, *, qb):
    bi = pl.program_id(0)
    n = pl.program_id(1)
    blk = WINDOW
    t = lax.broadcasted_iota(jnp.int32, (blk, 2 * blk), 0)
    s = lax.broadcasted_iota(jnp.int32, (blk, 2 * blk), 1)

    @pl.when(jnp.logical_and(bi == 0, n == 0))
    def _():
        bucket = bucket_ref[...]
        rel = blk + t - s
        in_window = (rel >= 0) & (rel < WINDOW)
        for h in range(SW_HEADS):
            bias = jnp.zeros((blk, 2 * blk), F32)
            for bk in range(REL_BUCKETS):
                bias = jnp.where(bucket == bk, table_ref[bk, h], bias)
            bias_ref[h] = jnp.where(in_window, bias, -jnp.inf)

    kk = jnp.concatenate([kp_ref[0], kc_ref[0]], axis=0)
    vv = jnp.concatenate([vp_ref[0], vc_ref[0]], axis=0)
    kk_sw = _swap_halves(kk)
    vv_sw = _swap_halves(vv)
    lane = lax.broadcasted_iota(jnp.int32, (blk, LANES), 1)
    low = lane < HEAD_DIM
    scale = HEAD_DIM ** -0.5
    group = SW_HEADS // SW_KV_HEADS
    has_prev = (s >= blk) | (n > 0)

    for j in range(qb):
        rows = slice(j * blk, (j + 1) * blk)
        keys = slice(j * blk, (j + 2) * blk)
        for p in range(SW_HEADS // 2):
            q2 = q_ref[0, rows, p * LANES:(p + 1) * LANES]
            zero = jnp.zeros_like(q2)
            outs = []
            for half in range(2):
                h = 2 * p + half
                kv = h // group
                qh = (jnp.where(low, q2, zero) if half == 0 else jnp.where(low, zero, q2)) * scale
                k_use = (kk if kv == half else kk_sw)[keys]
                v_use = (vv if kv == half else vv_sw)[keys]
                logits = _dot_nt(qh, k_use) + bias_ref[h]
                if j == 0:
                    logits = jnp.where(has_prev, logits, -jnp.inf)
                sink = sinks_ref[0, h]
                m = jnp.maximum(jnp.max(logits, axis=-1, keepdims=True), sink)
                pr = jnp.exp(logits - m)
                denom = jnp.sum(pr, axis=-1, keepdims=True) + jnp.exp(sink - m)
                outs.append(_dot(pr.astype(BF16), v_use) / denom)
            o_ref[0, rows, p * LANES:(p + 1) * LANES] = jnp.where(low, outs[0], outs[1]).astype(o_ref.dtype)


def _swa_attention(proj3, table, sinks, *, qb):
    b, s, _ = proj3.shape
    rows = qb * WINDOW
    q_blk = (3 * SB_WIDTH) // SW_Q_WIDTH
    k_blk = (3 * SB_WIDTH + SW_Q_WIDTH) // LANES
    v_blk = k_blk + 1
    bucket = jnp.asarray(_t5_bucket_table())
    smem = pl.BlockSpec(memory_space=pltpu.SMEM)
    prev = lambda c: (lambda bi, n: (bi, jnp.maximum(n * qb - 1, 0), c))
    cur = lambda c: (lambda bi, n: (bi, n, c))
    return pl.pallas_call(
        functools.partial(_swa_kernel, qb=qb),
        out_shape=jax.ShapeDtypeStruct((b, s, SW_Q_WIDTH), BF16),
        grid=(b, s // rows),
        in_specs=[smem, smem,
                  pl.BlockSpec((WINDOW, 2 * WINDOW), lambda bi, n: (0, 0)),
                  pl.BlockSpec((1, rows, SW_Q_WIDTH), cur(q_blk)),
                  pl.BlockSpec((1, WINDOW, LANES), prev(k_blk)),
                  pl.BlockSpec((1, rows, LANES), cur(k_blk)),
                  pl.BlockSpec((1, WINDOW, LANES), prev(v_blk)),
                  pl.BlockSpec((1, rows, LANES), cur(v_blk))],
        out_specs=pl.BlockSpec((1, rows, SW_Q_WIDTH), lambda bi, n: (bi, n, 0)),
        scratch_shapes=[pltpu.VMEM((SW_HEADS, WINDOW, 2 * WINDOW), F32)],
        compiler_params=_cparams(2),
        name="swa_attention",
    )(table, sinks, bucket, proj3, proj3, proj3, proj3, proj3)


def _to_bf16_kernel(*refs):
    n = len(refs) // 2
    for w_ref, o_ref in zip(refs[:n], refs[n:]):
        o_ref[...] = w_ref[...].astype(BF16)


def _to_bf16(ws, *, steps):
    n = ws[0].shape[0]
    specs = [pl.BlockSpec((1, w.shape[1] // steps, w.shape[2]), lambda l, i: (l, i, 0)) for w in ws]
    return pl.pallas_call(
        _to_bf16_kernel,
        out_shape=[jax.ShapeDtypeStruct(w.shape, BF16) for w in ws],
        grid=(n, steps),
        in_specs=specs,
        out_specs=specs,
        compiler_params=_cparams(2),
        name="to_bf16",
    )(*ws)


def _out_ffn_kernel(*refs, n_parts, final_norm):
    x_ref = refs[0]
    part_refs = refs[1:1 + n_parts]
    wo_refs = refs[1 + n_parts:1 + 2 * n_parts]
    g_ref, wg_ref, wu_ref, wd_ref, fg_ref, o_ref = refs[1 + 2 * n_parts:]
    x = x_ref[...]
    for a_ref, w_ref in zip(part_refs, wo_refs):
        x = x + _dot(a_ref[...], w_ref[...])
    h = _rms(x, g_ref[...]).astype(BF16)
    gate = _dot(h, wg_ref[...])
    up = _dot(h, wu_ref[...])
    act = (gate / (1.0 + jnp.exp(-gate)) * up).astype(BF16)
    y = x + _dot(act, wd_ref[...])
    if final_norm:
        y = _rms(y, fg_ref[...])
    o_ref[...] = y


def _out_ffn(x2, parts, w_out, g, wg, wu, wd, fg, *, layer, tm, final_norm):
    t, d = x2.shape
    hid = wg.shape[2]
    n_parts = len(parts)
    kp = w_out.shape[0] // n_parts
    resident = dict(pipeline_mode=pl.Buffered(1))
    row_tile = lambda width: pl.BlockSpec((tm, width), lambda i: (i, 0))
    const = lambda r, c: pl.BlockSpec((None, r, c), lambda i: (layer, 0, 0), **resident)
    in_specs = [row_tile(d)] + [row_tile(kp) for _ in parts]
    in_specs += [pl.BlockSpec((kp, d), functools.partial(lambda i, c: (c, 0), c=c), **resident)
                 for c in range(n_parts)]
    in_specs += [pl.BlockSpec((1, d), lambda i: (0, 0)), const(d, hid), const(d, hid),
                 const(hid, d), pl.BlockSpec((1, d), lambda i: (0, 0))]
    return pl.pallas_call(
        functools.partial(_out_ffn_kernel, n_parts=n_parts, final_norm=final_norm),
        out_shape=jax.ShapeDtypeStruct((t, d), F32),
        grid=(t // tm,),
        in_specs=in_specs,
        out_specs=row_tile(d),
        compiler_params=_cparams(1),
        name="out_ffn_final" if final_norm else "out_ffn",
    )(x2, *parts, *([w_out] * n_parts), g, wg, wu, wd, fg)


def _mla_proj_kernel(x_ref, pos_ref, freq_ref, g_ref, wd_ref, qg_ref, wq_ref,
                     kg_ref, wk_ref, wv_ref, q_ref, k_ref, v_ref):
    tm = x_ref.shape[0]
    h = _rms(x_ref[...], g_ref[...]).astype(BF16)
    down = _dot(h, wd_ref[...])
    c_q = down[:, :MLA_Q_RANK]
    c_kv = down[:, MLA_Q_RANK:MLA_Q_RANK + MLA_KV_RANK]
    kr = down[:, MLA_Q_RANK + MLA_KV_RANK:]

    half = MLA_ROPE_DIM // 2
    ang = pos_ref[...].astype(F32) * freq_ref[...]
    lane = lax.broadcasted_iota(jnp.int32, (tm, LANES), 1)
    first = lane < half
    second = (lane >= LANES // 2) & (lane < LANES // 2 + half)

    def spread(dense):
        rows8 = jnp.broadcast_to(dense[:, None, :], (tm // 8, 8, LANES)).reshape(tm, LANES)
        y = pltpu.roll(rows8, 0, 1, stride=half, stride_axis=0)
        return y, pltpu.roll(y, LANES // 2, 1)

    cos_1, cos_2 = spread(jnp.cos(ang))
    sin_1, sin_2 = spread(jnp.sin(ang))
    cos_p = jnp.where(first, cos_1, jnp.where(second, cos_2, 1.0))
    sin_p = jnp.where(first, -sin_1, jnp.where(second, sin_2, 0.0))

    def rope(blk):
        return blk * cos_p + pltpu.roll(blk, LANES // 2, 1) * sin_p

    cqn = _rms(c_q, qg_ref[...]).astype(BF16)
    ckvn = _rms(c_kv, kg_ref[...]).astype(BF16)
    k_rope = rope(kr)
    q = _dot(cqn, wq_ref[...])
    kn = _dot(ckvn, wk_ref[...])
    for hd in range(MLA_HEADS):
        sl = slice(hd * LANES, (hd + 1) * LANES)
        q_ref[:, sl] = rope(q[:, sl]).astype(q_ref.dtype)
        k_ref[:, sl] = (kn[:, sl] + k_rope).astype(k_ref.dtype)
    v_ref[...] = _dot(ckvn, wv_ref[...]).astype(v_ref.dtype)


def _mla_proj(x2, pos2, freq, g, wd, qg, wq, kg, wk, wv, *, tm):
    t, d = x2.shape
    full = lambda a: pl.BlockSpec(a.shape, lambda i: (0,) * a.ndim)
    qk_w = MLA_HEADS * LANES
    v_w = MLA_HEADS * MLA_V_DIM
    return pl.pallas_call(
        _mla_proj_kernel,
        out_shape=(jax.ShapeDtypeStruct((t, qk_w), BF16),
                   jax.ShapeDtypeStruct((t, qk_w), BF16),
                   jax.ShapeDtypeStruct((t, v_w), BF16)),
        grid=(t // tm,),
        in_specs=[pl.BlockSpec((tm, d), lambda i: (i, 0)),
                  pl.BlockSpec((tm // 8, LANES), lambda i: (i, 0)),
                  full(freq), full(g), full(wd), full(qg), full(wq),
                  full(kg), full(wk), full(wv)],
        out_specs=(pl.BlockSpec((tm, qk_w), lambda i: (i, 0)),
                   pl.BlockSpec((tm, qk_w), lambda i: (i, 0)),
                   pl.BlockSpec((tm, v_w), lambda i: (i, 0))),
        compiler_params=_cparams(1),
        name="mla_proj",
    )(x2, pos2, freq, g, wd, qg, wq, kg, wk, wv)


def _mla_attn_kernel(q_ref, k_ref, v_ref, o_ref, va_ref, vb_ref, m_ref, acc_ref, *, tq, tk, unroll):
    qi = pl.program_id(2)
    lane_s = lax.broadcasted_iota(jnp.int32, va_ref.shape, 1)

    @pl.when(qi == 0)
    def _():
        v_all = v_ref[0]
        one = jnp.ones_like(v_all)
        va_ref[...] = jnp.where(lane_s < MLA_V_DIM, v_all, one)
        vb_ref[...] = jnp.where(lane_s < MLA_V_DIM, one, v_all)

    r = tq // tk
    lower = (lax.broadcasted_iota(jnp.int32, (tk, tk), 1)
             <= lax.broadcasted_iota(jnp.int32, (tk, tk), 0))
    c2 = (MLA_NOPE_DIM + MLA_ROPE_DIM) ** -0.5 * LOG2E
    vaug = (va_ref, vb_ref)

    m_ref[...] = jnp.full(m_ref.shape, -jnp.inf, F32)
    acc_ref[...] = jnp.zeros(acc_ref.shape, F32)

    def tile(hh, r0, nrows, j, masked):
        rows = pl.ds(r0, nrows)
        q = q_ref[0, rows, hh * LANES:(hh + 1) * LANES]
        start = pl.multiple_of(j * tk, tk)
        kblk = k_ref[0, pl.ds(start, tk), hh * LANES:(hh + 1) * LANES]
        vblk = vaug[hh][pl.ds(start, tk), :]
        s2 = _dot_nt(q, kblk) * c2
        if masked:
            top = jnp.where(lower, s2[:tk], -jnp.inf)
            s2 = top if nrows == tk else jnp.concatenate([top, s2[tk:]], axis=0)
        m = m_ref[hh, rows, :]
        m_new = jnp.maximum(m, jnp.max(s2, axis=-1, keepdims=True))
        alpha = jnp.exp2(m - m_new)
        p = jnp.exp2(s2 - jnp.tile(m_new, (1, tk // LANES)))
        acc_ref[hh, rows, :] = alpha * acc_ref[hh, rows, :] + _dot(p.astype(BF16), vblk)
        m_ref[hh, rows, :] = m_new

    def body(jj, carry):
        for t in range(unroll):
            for hh in range(2):
                tile(hh, 0, tq, jj * unroll + t, False)
        return carry

    lax.fori_loop(0, (qi * r) // unroll, body, 0)
    for u in range(r):
        for hh in range(2):
            tile(hh, u * tk, tq - u * tk, qi * r + u, True)

    lane = lax.broadcasted_iota(jnp.int32, (tq, LANES), 1)
    acc_a = acc_ref[0]
    acc_b = acc_ref[1]
    out_a = acc_a / pltpu.roll(acc_a, MLA_V_DIM, 1)
    out_b = acc_b / pltpu.roll(acc_b, MLA_V_DIM, 1)
    o_ref[0] = jnp.where(lane < MLA_V_DIM, out_a, out_b).astype(o_ref.dtype)


def _mla_attention(q3, k3, v3, *, tq, tk, unroll):
    b, s, _ = q3.shape
    assert (tq // tk) % unroll == 0
    n_pairs = MLA_HEADS // 2
    return pl.pallas_call(
        functools.partial(_mla_attn_kernel, tq=tq, tk=tk, unroll=unroll),
        out_shape=jax.ShapeDtypeStruct((b, s, MLA_HEADS * MLA_V_DIM), BF16),
        grid=(b, n_pairs, s // tq),
        in_specs=[pl.BlockSpec((1, tq, 2 * LANES), lambda bi, p, i: (bi, i, p)),
                  pl.BlockSpec((1, s, 2 * LANES), lambda bi, p, i: (bi, 0, p)),
                  pl.BlockSpec((1, s, LANES), lambda bi, p, i: (bi, 0, p))],
        out_specs=pl.BlockSpec((1, tq, LANES), lambda bi, p, i: (bi, i, p)),
        scratch_shapes=[pltpu.VMEM((s, LANES), BF16), pltpu.VMEM((s, LANES), BF16),
                        pltpu.VMEM((2, tq, LANES), F32), pltpu.VMEM((2, tq, LANES), F32)],
        compiler_params=_cparams(3),
        name="mla_attention",
    )(q3, k3, v3)


def _head_lanes(nope, rope):
    half = MLA_ROPE_DIM // 2
    split = LANES // 2 - half
    zero = jnp.zeros(nope.shape[:-1] + (LANES - MLA_NOPE_DIM - MLA_ROPE_DIM,), nope.dtype)
    return jnp.concatenate([rope[..., :half], nope[..., :split], rope[..., half:], nope[..., split:], zero],
                           axis=-1)


def _mla_weight_layout(w_down, w_uq, w_ukv):
    d = w_down.shape[0]
    w_kr = w_down[:, MLA_Q_RANK + MLA_KV_RANK:]
    wd = jnp.concatenate([w_down[:, :MLA_Q_RANK + MLA_KV_RANK],
                          _head_lanes(jnp.zeros((d, MLA_NOPE_DIM), F32), w_kr)], axis=1)

    wq = w_uq.reshape(MLA_Q_RANK, MLA_HEADS, MLA_NOPE_DIM + MLA_ROPE_DIM)
    wq = _head_lanes(wq[..., :MLA_NOPE_DIM], wq[..., MLA_NOPE_DIM:]).reshape(MLA_Q_RANK, MLA_HEADS * LANES)

    wkv = w_ukv.reshape(MLA_KV_RANK, MLA_HEADS, MLA_NOPE_DIM + MLA_V_DIM)
    wk_nope, wv = wkv[..., :MLA_NOPE_DIM], wkv[..., MLA_NOPE_DIM:]
    wk = _head_lanes(wk_nope, jnp.zeros((MLA_KV_RANK, MLA_HEADS, MLA_ROPE_DIM), F32))
    wk = wk.reshape(MLA_KV_RANK, MLA_HEADS * LANES)
    wv = wv.reshape(MLA_KV_RANK, MLA_HEADS * MLA_V_DIM)
    return wd.astype(BF16), wq.astype(BF16), wk.astype(BF16), wv.astype(BF16)


def _rope_dense_layout(positions):
    half = MLA_ROPE_DIM // 2
    n_blk = LANES // half
    freqs = ROPE_THETA ** (-jnp.arange(half, dtype=F32) / half)
    order = np.array([(-b) % n_blk for b in range(n_blk)])
    pos_dense = jnp.repeat(positions.reshape(-1, n_blk)[:, order], half, axis=1)
    return pos_dense, jnp.tile(freqs, n_blk)[None, :]


def kernel(x, positions, attn_norm, ffn_norm, even_w_in, even_sinks, even_w_out, rel_bias_table, mla_w_down, mla_q_norm, mla_w_uq, mla_kv_norm, mla_w_ukv, mla_w_o, ffn_w_gate, ffn_w_up, ffn_w_down, final_norm):
    b, s, d = x.shape
    t = b * s
    x2 = x.reshape(t, d)
    row = lambda v: v.reshape(1, -1)

    proj = _norm_proj(x2, row(attn_norm[0]), even_w_in[0], tm=1024)
    proj3 = proj.reshape(b, s, EVEN_IN_WIDTH)
    o_a = _sb_attention(proj3, sub=256, group=1024, pairs=2)
    o_b = _swa_attention(proj3, rel_bias_table, even_sinks[0].reshape(1, SW_HEADS), qb=8)
    wg, wu, wd_ffn = _to_bf16([ffn_w_gate, ffn_w_up, ffn_w_down], steps=4)
    x2 = _out_ffn(x2, [o_a.reshape(t, SB_WIDTH), o_b.reshape(t, SW_Q_WIDTH)], even_w_out[0].astype(BF16),
                  row(ffn_norm[0]), wg, wu, wd_ffn, row(final_norm), layer=0, tm=512, final_norm=False)

    wd, wq, wk, wv = _mla_weight_layout(mla_w_down[0], mla_w_uq[0], mla_w_ukv[0])
    pos_dense, freq_dense = _rope_dense_layout(positions)
    q, k, v = _mla_proj(x2, pos_dense, freq_dense, row(attn_norm[1]), wd,
                        row(mla_q_norm[0]), wq, row(mla_kv_norm[0]), wk, wv, tm=512)
    o = _mla_attention(q.reshape(b, s, -1), k.reshape(b, s, -1), v.reshape(b, s, -1), tq=4096, tk=512, unroll=2)
    x2 = _out_ffn(x2, [o.reshape(t, MLA_HEADS * MLA_V_DIM)], mla_w_o[0].astype(BF16),
                  row(ffn_norm[1]), wg, wu, wd_ffn, row(final_norm), layer=1, tm=512, final_norm=True)
    return x2.reshape(b, s, d)
```

```python
import functools
import math

import numpy as np
import jax
import jax.numpy as jnp
from jax import lax
from jax.experimental import pallas as pl
from jax.experimental.pallas import tpu as pltpu

F32 = jnp.float32
BF16 = jnp.bfloat16

D_MODEL = 1024
EPS = 1e-6
HEAD_DIM = 64
LANES = 128
SB_HEADS = 8
SW_HEADS = 8
SW_KV_HEADS = 2
WINDOW = 128
SB_WIDTH = SB_HEADS * HEAD_DIM
SW_Q_WIDTH = SW_HEADS * HEAD_DIM
SW_KV_WIDTH = SW_KV_HEADS * HEAD_DIM
EVEN_IN_WIDTH = 3 * SB_WIDTH + SW_Q_WIDTH + 2 * SW_KV_WIDTH
REL_BUCKETS = 32
REL_MAX_DIST = 128
MLA_HEADS = 16
MLA_NOPE_DIM = 64
MLA_ROPE_DIM = 32
MLA_V_DIM = 64
MLA_Q_RANK = 384
MLA_KV_RANK = 256
ROPE_THETA = 10000.0
LOG2E = math.log2(math.e)
SB_SATURATED = 128.0
VMEM_LIMIT_BYTES = 56 * 1024 * 1024


def _cparams(n_axes):
    return pltpu.CompilerParams(
        dimension_semantics=("arbitrary",) * n_axes,
        vmem_limit_bytes=VMEM_LIMIT_BYTES)


def _rms(x, g):
    return x * lax.rsqrt(jnp.mean(x * x, axis=-1, keepdims=True) + EPS) * g


def _dot(a, b):
    return jnp.dot(a, b, preferred_element_type=F32)


def _dot_nt(a, b):
    return lax.dot_general(a, b, (((1,), (1,)), ((), ())), preferred_element_type=F32)


def _norm_proj_kernel(x_ref, g_ref, w_ref, o_ref, wb_ref):
    @pl.when(pl.program_id(0) == 0)
    def _():
        wb_ref[...] = w_ref[...].astype(BF16)

    h = _rms(x_ref[...], g_ref[...]).astype(BF16)
    o_ref[...] = _dot(h, wb_ref[...]).astype(o_ref.dtype)


def _norm_proj(x2, g, w, *, tm):
    t, d = x2.shape
    n = w.shape[1]
    return pl.pallas_call(
        _norm_proj_kernel,
        out_shape=jax.ShapeDtypeStruct((t, n), BF16),
        grid=(t // tm,),
        in_specs=[pl.BlockSpec((tm, d), lambda i: (i, 0)),
                  pl.BlockSpec((1, d), lambda i: (0, 0)),
                  pl.BlockSpec((d, n), lambda i: (0, 0), pipeline_mode=pl.Buffered(1))],
        out_specs=pl.BlockSpec((tm, n), lambda i: (i, 0)),
        scratch_shapes=[pltpu.VMEM((d, n), BF16)],
        compiler_params=_cparams(1),
        name="norm_proj",
    )(x2, g, w)


def _softplus(z):
    return jnp.maximum(z, 0.0) + jnp.log(1.0 + jnp.exp2(jnp.abs(z) * (-LOG2E)))


def _sb_kernel(q_ref, k_ref, v_ref, o_ref, r_ref, acc_ref, *, sub, group):
    tq = q_ref.shape[1]
    r = tq // sub
    heads = 2 * (q_ref.shape[2] // LANES)
    lane = lax.broadcasted_iota(jnp.int32, (tq, LANES), 1)
    low = lane < HEAD_DIM
    scale = HEAD_DIM ** -0.5
    qm = []
    for pp in range(heads // 2):
        q2 = q_ref[0, :, pp * LANES:(pp + 1) * LANES]
        zero = jnp.zeros_like(q2)
        qm += [jnp.where(low, q2, zero) * scale, jnp.where(low, zero, q2) * scale]

    tri_row = lax.broadcasted_iota(jnp.int32, (sub, sub), 0)
    tri_col = lax.broadcasted_iota(jnp.int32, (sub, sub), 1)
    strictly_lower = tri_col < tri_row
    tri = strictly_lower.astype(BF16)
    pick = (lax.broadcasted_iota(jnp.int32, (LANES, sub), 0) == 0).astype(BF16)
    tri_carry = jnp.concatenate([tri, pick, pick], axis=0)

    def tile(hh, r0, nrows, j, masked, first_row=None, fresh=0):
        rows = pl.ds(r0, nrows)
        start = pl.multiple_of(j * sub, sub)
        pair_lanes = slice((hh // 2) * LANES, (hh // 2 + 1) * LANES)
        kblk = k_ref[0, pl.ds(start, sub), pair_lanes]
        vblk = v_ref[0, pl.ds(start, sub), pair_lanes]
        z = _dot_nt(qm[hh][r0:r0 + nrows], kblk)
        if first_row is not None:
            row = lax.broadcasted_iota(jnp.int32, (nrows, sub), 0) + r0
            z = jnp.where(row >= first_row, z, -jnp.inf)
        if masked:
            top = jnp.where(strictly_lower, z[:sub], -jnp.inf)
            z = top if nrows == sub else jnp.concatenate([top, z[sub:]], axis=0)
        sp = _softplus(z)
        log_beta = z - sp
        old = pl.ds(r0 + fresh, nrows - fresh)
        later = r_ref[hh, old, :] if fresh < nrows else None
        if fresh:
            zeros = jnp.zeros((fresh, LANES), F32)
            later = zeros if later is None else jnp.concatenate([zeros, later], axis=0)
        if fresh == nrows:
            e = _dot(sp.astype(BF16), tri)
        else:
            hi = later.astype(BF16)
            lo = (later - hi.astype(F32)).astype(BF16)
            e = _dot(jnp.concatenate([sp.astype(BF16), hi, lo], axis=1), tri_carry)
        w = jnp.exp(log_beta - e)
        pv = _dot(w.astype(BF16), vblk)
        if fresh:
            acc_ref[hh, pl.ds(r0, fresh), :] = pv[:fresh]
        if fresh < nrows:
            acc_ref[hh, old, :] += pv[fresh:]
        r_ref[hh, rows, :] = jnp.broadcast_to(e[:, 0:1] + sp[:, 0:1], (nrows, LANES))

    def unsaturated(first_row):
        return jnp.min(r_ref[:, first_row:, :]) < SB_SATURATED

    band = 2 * sub
    for u in reversed(range(r)):
        for hh in range(heads):
            tile(hh, u * sub, min(band, tq - u * sub), u, True, fresh=sub)

    @pl.when(unsaturated(band))
    def _():
        def body(jj, carry):
            j = r - 3 - jj
            first_row = (j + 2) * sub
            for g in range(tq // group):
                g0 = max(g * group, band)

                @pl.when(jnp.logical_and(first_row // group == g, unsaturated(g0)))
                def _():
                    for hh in range(heads):
                        tile(hh, g * group, tq - g * group, j, False, first_row=first_row)
            return carry

        lax.fori_loop(0, r - 2, body, 0)

    for pp in range(heads // 2):
        o_ref[0, :, pp * LANES:(pp + 1) * LANES] = jnp.where(
            low, acc_ref[2 * pp], acc_ref[2 * pp + 1]).astype(o_ref.dtype)


def _sb_attention(proj3, *, sub, group, pairs):
    b, s, _ = proj3.shape
    width = pairs * LANES
    k_off = SB_WIDTH // width
    v_off = 2 * SB_WIDTH // width
    seq = lambda c0: pl.BlockSpec((1, s, width), lambda bi, p: (bi, 0, c0 + p))
    return pl.pallas_call(
        functools.partial(_sb_kernel, sub=sub, group=group),
        out_shape=jax.ShapeDtypeStruct((b, s, SB_WIDTH), BF16),
        grid=(b, SB_WIDTH // width),
        in_specs=[seq(0), seq(k_off), seq(v_off)],
        out_specs=seq(0),
        scratch_shapes=[pltpu.VMEM((2 * pairs, s, LANES), F32), pltpu.VMEM((2 * pairs, s, LANES), F32)],
        compiler_params=_cparams(2),
        name="sb_attention",
    )(proj3, proj3, proj3)


def _t5_bucket_table():
    max_exact = REL_BUCKETS // 2
    rel = WINDOW + np.arange(WINDOW)[:, None] - np.arange(2 * WINDOW)[None, :]
    rel = np.maximum(rel, 0)
    relf = np.maximum(rel, 1).astype(np.float32)
    large = max_exact + (np.log(relf / np.float32(max_exact)) / np.float32(math.log(REL_MAX_DIST / max_exact))
                         * np.float32(REL_BUCKETS - max_exact)).astype(np.int32)
    large = np.minimum(large, REL_BUCKETS - 1)
    return np.where(rel < max_exact, rel, large).astype(np.int32)


def _swap_halves(x):
    return jnp.concatenate([x[:, HEAD_DIM:], x[:, :HEAD_DIM]], axis=1)


def _swa_kernel(table_ref, sinks_ref, bucket_ref, q_ref, kp_ref, kc_ref, vp_ref, vc_ref,
                o_ref, bias_ref, *, qb):
    bi = pl.program_id(0)
    n = pl.program_id(1)
    blk = WINDOW
    t = lax.broadcasted_iota(jnp.int32, (blk, 2 * blk), 0)
    s = lax.broadcasted_iota(jnp.int32, (blk, 2 * blk), 1)

    @pl.when(jnp.logical_and(bi == 0, n == 0))
    def _():
        bucket = bucket_ref[...]
        rel = blk + t - s
        in_window = (rel >= 0) & (rel < WINDOW)
        for h in range(SW_HEADS):
            bias = jnp.zeros((blk, 2 * blk), F32)
            for bk in range(REL_BUCKETS):
                bias = jnp.where(bucket == bk, table_ref[bk, h], bias)
            bias_ref[h] = jnp.where(in_window, bias, -jnp.inf)

    kk = jnp.concatenate([kp_ref[0], kc_ref[0]], axis=0)
    vv = jnp.concatenate([vp_ref[0], vc_ref[0]], axis=0)
    kk_sw = _swap_halves(kk)
    vv_sw = _swap_halves(vv)
    lane = lax.broadcasted_iota(jnp.int32, (blk, LANES), 1)
    low = lane < HEAD_DIM
    scale = HEAD_DIM ** -0.5
    group = SW_HEADS // SW_KV_HEADS
    has_prev = (s >= blk) | (n > 0)

    for j in range(qb):
        rows = slice(j * blk, (j + 1) * blk)
        keys = slice(j * blk, (j + 2) * blk)
        for p in range(SW_HEADS // 2):
            q2 = q_ref[0, rows, p * LANES:(p + 1) * LANES]
            zero = jnp.zeros_like(q2)
            outs = []
            for half in range(2):
                h = 2 * p + half
                kv = h // group
                qh = (jnp.where(low, q2, zero) if half == 0 else jnp.where(low, zero, q2)) * scale
                k_use = (kk if kv == half else kk_sw)[keys]
                v_use = (vv if kv == half else vv_sw)[keys]
                logits = _dot_nt(qh, k_use) + bias_ref[h]
                if j == 0:
                    logits = jnp.where(has_prev, logits, -jnp.inf)
                sink = sinks_ref[0, h]
                m = jnp.maximum(jnp.max(logits, axis=-1, keepdims=True), sink)
                pr = jnp.exp(logits - m)
                denom = jnp.sum(pr, axis=-1, keepdims=True) + jnp.exp(sink - m)
                outs.append(_dot(pr.astype(BF16), v_use) / denom)
            o_ref[0, rows, p * LANES:(p + 1) * LANES] = jnp.where(low, outs[0], outs[1]).astype(o_ref.dtype)


def _swa_attention(proj3, table, sinks, *, qb):
    b, s, _ = proj3.shape
    rows = qb * WINDOW
    q_blk = (3 * SB_WIDTH) // SW_Q_WIDTH
    k_blk = (3 * SB_WIDTH + SW_Q_WIDTH) // LANES
    v_blk = k_blk + 1
    bucket = jnp.asarray(_t5_bucket_table())
    smem = pl.BlockSpec(memory_space=pltpu.SMEM)
    prev = lambda c: (lambda bi, n: (bi, jnp.maximum(n * qb - 1, 0), c))
    cur = lambda c: (lambda bi, n: (bi, n, c))
    return pl.pallas_call(
        functools.partial(_swa_kernel, qb=qb),
        out_shape=jax.ShapeDtypeStruct((b, s, SW_Q_WIDTH), BF16),
        grid=(b, s // rows),
        in_specs=[smem, smem,
                  pl.BlockSpec((WINDOW, 2 * WINDOW), lambda bi, n: (0, 0)),
                  pl.BlockSpec((1, rows, SW_Q_WIDTH), cur(q_blk)),
                  pl.BlockSpec((1, WINDOW, LANES), prev(k_blk)),
                  pl.BlockSpec((1, rows, LANES), cur(k_blk)),
                  pl.BlockSpec((1, WINDOW, LANES), prev(v_blk)),
                  pl.BlockSpec((1, rows, LANES), cur(v_blk))],
        out_specs=pl.BlockSpec((1, rows, SW_Q_WIDTH), lambda bi, n: (bi, n, 0)),
        scratch_shapes=[pltpu.VMEM((SW_HEADS, WINDOW, 2 * WINDOW), F32)],
        compiler_params=_cparams(2),
        name="swa_attention",
    )(table, sinks, bucket, proj3, proj3, proj3, proj3, proj3)


def _to_bf16_kernel(*refs):
    n = len(refs) // 2
    for w_ref, o_ref in zip(refs[:n], refs[n:]):
        o_ref[...] = w_ref[...].astype(BF16)


def _to_bf16(ws, *, steps):
    n = ws[0].shape[0]
    specs = [pl.BlockSpec((1, w.shape[1] // steps, w.shape[2]), lambda l, i: (l, i, 0)) for w in ws]
    return pl.pallas_call(
        _to_bf16_kernel,
        out_shape=[jax.ShapeDtypeStruct(w.shape, BF16) for w in ws],
        grid=(n, steps),
        in_specs=specs,
        out_specs=specs,
        compiler_params=_cparams(2),
        name="to_bf16",
    )(*ws)


def _out_ffn_kernel(*refs, n_parts, final_norm):
    x_ref = refs[0]
    part_refs = refs[1:1 + n_parts]
    wo_refs = refs[1 + n_parts:1 + 2 * n_parts]
    g_ref, wg_ref, wu_ref, wd_ref, fg_ref, o_ref = refs[1 + 2 * n_parts:]
    x = x_ref[...]
    for a_ref, w_ref in zip(part_refs, wo_refs):
        x = x + _dot(a_ref[...], w_ref[...])
    h = _rms(x, g_ref[...]).astype(BF16)
    gate = _dot(h, wg_ref[...])
    up = _dot(h, wu_ref[...])
    act = (gate / (1.0 + jnp.exp(-gate)) * up).astype(BF16)
    y = x + _dot(act, wd_ref[...])
    if final_norm:
        y = _rms(y, fg_ref[...])
    o_ref[...] = y


def _out_ffn(x2, parts, w_out, g, wg, wu, wd, fg, *, layer, tm, final_norm):
    t, d = x2.shape
    hid = wg.shape[2]
    n_parts = len(parts)
    kp = w_out.shape[0] // n_parts
    resident = dict(pipeline_mode=pl.Buffered(1))
    row_tile = lambda width: pl.BlockSpec((tm, width), lambda i: (i, 0))
    const = lambda r, c: pl.BlockSpec((None, r, c), lambda i: (layer, 0, 0), **resident)
    in_specs = [row_tile(d)] + [row_tile(kp) for _ in parts]
    in_specs += [pl.BlockSpec((kp, d), functools.partial(lambda i, c: (c, 0), c=c), **resident)
                 for c in range(n_parts)]
    in_specs += [pl.BlockSpec((1, d), lambda i: (0, 0)), const(d, hid), const(d, hid),
                 const(hid, d), pl.BlockSpec((1, d), lambda i: (0, 0))]
    return pl.pallas_call(
        functools.partial(_out_ffn_kernel, n_parts=n_parts, final_norm=final_norm),
        out_shape=jax.ShapeDtypeStruct((t, d), F32),
        grid=(t // tm,),
        in_specs=in_specs,
        out_specs=row_tile(d),
        compiler_params=_cparams(1),
        name="out_ffn_final" if final_norm else "out_ffn",
    )(x2, *parts, *([w_out] * n_parts), g, wg, wu, wd, fg)


def _mla_proj_kernel(x_ref, pos_ref, freq_ref, g_ref, wd_ref, qg_ref, wq_ref,
                     kg_ref, wk_ref, wv_ref, q_ref, k_ref, v_ref):
    tm = x_ref.shape[0]
    h = _rms(x_ref[...], g_ref[...]).astype(BF16)
    down = _dot(h, wd_ref[...])
    c_q = down[:, :MLA_Q_RANK]
    c_kv = down[:, MLA_Q_RANK:MLA_Q_RANK + MLA_KV_RANK]
    kr = down[:, MLA_Q_RANK + MLA_KV_RANK:]

    half = MLA_ROPE_DIM // 2
    ang = pos_ref[...].astype(F32) * freq_ref[...]
    lane = lax.broadcasted_iota(jnp.int32, (tm, LANES), 1)
    first = lane < half
    second = (lane >= LANES // 2) & (lane < LANES // 2 + half)

    def spread(dense):
        rows8 = jnp.broadcast_to(dense[:, None, :], (tm // 8, 8, LANES)).reshape(tm, LANES)
        y = pltpu.roll(rows8, 0, 1, stride=half, stride_axis=0)
        return y, pltpu.roll(y, LANES // 2, 1)

    cos_1, cos_2 = spread(jnp.cos(ang))
    sin_1, sin_2 = spread(jnp.sin(ang))
    cos_p = jnp.where(first, cos_1, jnp.where(second, cos_2, 1.0))
    sin_p = jnp.where(first, -sin_1, jnp.where(second, sin_2, 0.0))

    def rope(blk):
        return blk * cos_p + pltpu.roll(blk, LANES // 2, 1) * sin_p

    cqn = _rms(c_q, qg_ref[...]).astype(BF16)
    ckvn = _rms(c_kv, kg_ref[...]).astype(BF16)
    k_rope = rope(kr)
    q = _dot(cqn, wq_ref[...])
    kn = _dot(ckvn, wk_ref[...])
    for hd in range(MLA_HEADS):
        sl = slice(hd * LANES, (hd + 1) * LANES)
        q_ref[:, sl] = rope(q[:, sl]).astype(q_ref.dtype)
        k_ref[:, sl] = (kn[:, sl] + k_rope).astype(k_ref.dtype)
    v_ref[...] = _dot(ckvn, wv_ref[...]).astype(v_ref.dtype)


def _mla_proj(x2, pos2, freq, g, wd, qg, wq, kg, wk, wv, *, tm):
    t, d = x2.shape
    full = lambda a: pl.BlockSpec(a.shape, lambda i: (0,) * a.ndim)
    qk_w = MLA_HEADS * LANES
    v_w = MLA_HEADS * MLA_V_DIM
    return pl.pallas_call(
        _mla_proj_kernel,
        out_shape=(jax.ShapeDtypeStruct((t, qk_w), BF16),
                   jax.ShapeDtypeStruct((t, qk_w), BF16),
                   jax.ShapeDtypeStruct((t, v_w), BF16)),
        grid=(t // tm,),
        in_specs=[pl.BlockSpec((tm, d), lambda i: (i, 0)),
                  pl.BlockSpec((tm // 8, LANES), lambda i: (i, 0)),
                  full(freq), full(g), full(wd), full(qg), full(wq),
                  full(kg), full(wk), full(wv)],
        out_specs=(pl.BlockSpec((tm, qk_w), lambda i: (i, 0)),
                   pl.BlockSpec((tm, qk_w), lambda i: (i, 0)),
                   pl.BlockSpec((tm, v_w), lambda i: (i, 0))),
        compiler_params=_cparams(1),
        name="mla_proj",
    )(x2, pos2, freq, g, wd, qg, wq, kg, wk, wv)


def _mla_attn_kernel(q_ref, k_ref, v_ref, o_ref, va_ref, vb_ref, m_ref, acc_ref, *, tq, tk, unroll):
    qi = pl.program_id(2)
    lane_s = lax.broadcasted_iota(jnp.int32, va_ref.shape, 1)

    @pl.when(qi == 0)
    def _():
        v_all = v_ref[0]
        one = jnp.ones_like(v_all)
        va_ref[...] = jnp.where(lane_s < MLA_V_DIM, v_all, one)
        vb_ref[...] = jnp.where(lane_s < MLA_V_DIM, one, v_all)

    r = tq // tk
    lower = (lax.broadcasted_iota(jnp.int32, (tk, tk), 1)
             <= lax.broadcasted_iota(jnp.int32, (tk, tk), 0))
    c2 = (MLA_NOPE_DIM + MLA_ROPE_DIM) ** -0.5 * LOG2E
    vaug = (va_ref, vb_ref)

    m_ref[...] = jnp.full(m_ref.shape, -jnp.inf, F32)
    acc_ref[...] = jnp.zeros(acc_ref.shape, F32)

    def tile(hh, r0, nrows, j, masked):
        rows = pl.ds(r0, nrows)
        q = q_ref[0, rows, hh * LANES:(hh + 1) * LANES]
        start = pl.multiple_of(j * tk, tk)
        kblk = k_ref[0, pl.ds(start, tk), hh * LANES:(hh + 1) * LANES]
        vblk = vaug[hh][pl.ds(start, tk), :]
        s2 = _dot_nt(q, kblk) * c2
        if masked:
            top = jnp.where(lower, s2[:tk], -jnp.inf)
            s2 = top if nrows == tk else jnp.concatenate([top, s2[tk:]], axis=0)
        m = m_ref[hh, rows, :]
        m_new = jnp.maximum(m, jnp.max(s2, axis=-1, keepdims=True))
        alpha = jnp.exp2(m - m_new)
        p = jnp.exp2(s2 - jnp.tile(m_new, (1, tk // LANES)))
        acc_ref[hh, rows, :] = alpha * acc_ref[hh, rows, :] + _dot(p.astype(BF16), vblk)
        m_ref[hh, rows, :] = m_new

    def body(jj, carry):
        for t in range(unroll):
            for hh in range(2):
                tile(hh, 0, tq, jj * unroll + t, False)
        return carry

    lax.fori_loop(0, (qi * r) // unroll, body, 0)
    for u in range(r):
        for hh in range(2):
            tile(hh, u * tk, tq - u * tk, qi * r + u, True)

    lane = lax.broadcasted_iota(jnp.int32, (tq, LANES), 1)
    acc_a = acc_ref[0]
    acc_b = acc_ref[1]
    out_a = acc_a / pltpu.roll(acc_a, MLA_V_DIM, 1)
    out_b = acc_b / pltpu.roll(acc_b, MLA_V_DIM, 1)
    o_ref[0] = jnp.where(lane < MLA_V_DIM, out_a, out_b).astype(o_ref.dtype)


def _mla_attention(q3, k3, v3, *, tq, tk, unroll):
    b, s, _ = q3.shape
    assert (tq // tk) % unroll == 0
    n_pairs = MLA_HEADS // 2
    return pl.pallas_call(
        functools.partial(_mla_attn_kernel, tq=tq, tk=tk, unroll=unroll),
        out_shape=jax.ShapeDtypeStruct((b, s, MLA_HEADS * MLA_V_DIM), BF16),
        grid=(b, n_pairs, s // tq),
        in_specs=[pl.BlockSpec((1, tq, 2 * LANES), lambda bi, p, i: (bi, i, p)),
                  pl.BlockSpec((1, s, 2 * LANES), lambda bi, p, i: (bi, 0, p)),
                  pl.BlockSpec((1, s, LANES), lambda bi, p, i: (bi, 0, p))],
        out_specs=pl.BlockSpec((1, tq, LANES), lambda bi, p, i: (bi, i, p)),
        scratch_shapes=[pltpu.VMEM((s, LANES), BF16), pltpu.VMEM((s, LANES), BF16),
                        pltpu.VMEM((2, tq, LANES), F32), pltpu.VMEM((2, tq, LANES), F32)],
        compiler_params=_cparams(3),
        name="mla_attention",
    )(q3, k3, v3)


def _head_lanes(nope, rope):
    half = MLA_ROPE_DIM // 2
    split = LANES // 2 - half
    zero = jnp.zeros(nope.shape[:-1] + (LANES - MLA_NOPE_DIM - MLA_ROPE_DIM,), nope.dtype)
    return jnp.concatenate([rope[..., :half], nope[..., :split], rope[..., half:], nope[..., split:], zero],
                           axis=-1)


def _mla_weight_layout(w_down, w_uq, w_ukv):
    d = w_down.shape[0]
    w_kr = w_down[:, MLA_Q_RANK + MLA_KV_RANK:]
    wd = jnp.concatenate([w_down[:, :MLA_Q_RANK + MLA_KV_RANK],
                          _head_lanes(jnp.zeros((d, MLA_NOPE_DIM), F32), w_kr)], axis=1)

    wq = w_uq.reshape(MLA_Q_RANK, MLA_HEADS, MLA_NOPE_DIM + MLA_ROPE_DIM)
    wq = _head_lanes(wq[..., :MLA_NOPE_DIM], wq[..., MLA_NOPE_DIM:]).reshape(MLA_Q_RANK, MLA_HEADS * LANES)

    wkv = w_ukv.reshape(MLA_KV_RANK, MLA_HEADS, MLA_NOPE_DIM + MLA_V_DIM)
    wk_nope, wv = wkv[..., :MLA_NOPE_DIM], wkv[..., MLA_NOPE_DIM:]
    wk = _head_lanes(wk_nope, jnp.zeros((MLA_KV_RANK, MLA_HEADS, MLA_ROPE_DIM), F32))
    wk = wk.reshape(MLA_KV_RANK, MLA_HEADS * LANES)
    wv = wv.reshape(MLA_KV_RANK, MLA_HEADS * MLA_V_DIM)
    return wd.astype(BF16), wq.astype(BF16), wk.astype(BF16), wv.astype(BF16)


def _rope_dense_layout(positions):
    half = MLA_ROPE_DIM // 2
    n_blk = LANES // half
    freqs = ROPE_THETA ** (-jnp.arange(half, dtype=F32) / half)
    order = np.array([(-b) % n_blk for b in range(n_blk)])
    pos_dense = jnp.repeat(positions.reshape(-1, n_blk)[:, order], half, axis=1)
    return pos_dense, jnp.tile(freqs, n_blk)[None, :]


def kernel(x, positions, attn_norm, ffn_norm, even_w_in, even_sinks, even_w_out, rel_bias_table, mla_w_down, mla_q_norm, mla_w_uq, mla_kv_norm, mla_w_ukv, mla_w_o, ffn_w_gate, ffn_w_up, ffn_w_down, final_norm):
    b, s, d = x.shape
    t = b * s
    x2 = x.reshape(t, d)
    row = lambda v: v.reshape(1, -1)

    proj = _norm_proj(x2, row(attn_norm[0]), even_w_in[0], tm=1024)
    proj3 = proj.reshape(b, s, EVEN_IN_WIDTH)
    o_a = _sb_attention(proj3, sub=256, group=1024, pairs=1)
    o_b = _swa_attention(proj3, rel_bias_table, even_sinks[0].reshape(1, SW_HEADS), qb=8)
    wg, wu, wd_ffn = _to_bf16([ffn_w_gate, ffn_w_up, ffn_w_down], steps=4)
    x2 = _out_ffn(x2, [o_a.reshape(t, SB_WIDTH), o_b.reshape(t, SW_Q_WIDTH)], even_w_out[0].astype(BF16),
                  row(ffn_norm[0]), wg, wu, wd_ffn, row(final_norm), layer=0, tm=512, final_norm=False)

    wd, wq, wk, wv = _mla_weight_layout(mla_w_down[0], mla_w_uq[0], mla_w_ukv[0])
    pos_dense, freq_dense = _rope_dense_layout(positions)
    q, k, v = _mla_proj(x2, pos_dense, freq_dense, row(attn_norm[1]), wd,
                        row(mla_q_norm[0]), wq, row(mla_kv_norm[0]), wk, wv, tm=512)
    o = _mla_attention(q.reshape(b, s, -1), k.reshape(b, s, -1), v.reshape(b, s, -1), tq=4096, tk=512, unroll=2)
    x2 = _out_ffn(x2, [o.reshape(t, MLA_HEADS * MLA_V_DIM)], mla_w_o[0].astype(BF16),
                  row(ffn_norm[1]), wg, wu, wd_ffn, row(final_norm), layer=1, tm=512, final_norm=True)
    return x2.reshape(b, s, d)
```
